```python
import jax, jax.numpy as jnp
from jax import lax
import numpy as np

D_MODEL = 1024
BATCH = 8
SEQ = 4096
DEPTH = 4

N_HEADS = 8
HEAD_DIM = 64
KV_RANK = 256
IDX_HEADS = 8
IDX_DIM = 64
TOPK_MAX = 256
Q_BLOCK = 128
POOL_WINDOWS = (2, 4, 8, 16)
POOL_WIDTH = 512
POOL_GROUP = POOL_WIDTH // len(POOL_WINDOWS)
SSD_HEADS = 16
SSD_HEAD_DIM = 64
D_INNER = SSD_HEADS * SSD_HEAD_DIM
N_GROUPS = 2
D_STATE = 128
CONV_WIDTH = 4
CONV_DIM = D_INNER + 2 * N_GROUPS * D_STATE
CHUNK = 128
D_FF = 2816
PLE_DIM = 256
N_BRANCHES = 3
EPS = 1e-6
IN_SPLITS = (N_HEADS * HEAD_DIM, KV_RANK, IDX_HEADS * IDX_DIM, IDX_HEADS, IDX_DIM,
             POOL_WIDTH, D_INNER, CONV_DIM, SSD_HEADS, N_BRANCHES * D_MODEL)
IN_COLS = sum(IN_SPLITS)

kernel_name = "hybrid_dsa_pool_ssd_macaron_trunk"


def rmsnorm(x, g):
    x32 = x.astype(jnp.float32)
    y = x32 * lax.rsqrt(jnp.mean(x32 * x32, axis=-1, keepdims=True) + EPS)
    return (y * g.astype(jnp.float32)).astype(x.dtype)


def swiglu(x, w_gate, w_up, w_down):
    return (jax.nn.silu(x @ w_gate) * (x @ w_up)) @ w_down


def dsa_attention(q, c_kv, q_idx, w_idx, k_idx, w_uk, w_uv):
    b, s = q.shape[:2]
    n_blk = s // Q_BLOCK
    k_top = min(TOPK_MAX, s // 4)
    q_lat = jnp.einsum("bshd,rhd->bshr", q, w_uk)

    def to_blocks(a):
        return jnp.moveaxis(a.reshape((b, n_blk, Q_BLOCK) + a.shape[2:]), 1, 0)

    key_pos = jnp.arange(s)
    gather = jax.vmap(lambda c, i: c[i])

    def one_block(args):
        j, ql, qi, wi = args
        q_pos = j * Q_BLOCK + jnp.arange(Q_BLOCK)
        causal = key_pos[None, :] <= q_pos[:, None]
        idx = jnp.einsum("bqh,bqhs->bqs", wi,
                         jax.nn.relu(jnp.einsum("bqhd,bsd->bqhs", qi, k_idx)))
        idx = jnp.where(causal[None], idx.astype(jnp.float32), -jnp.inf)
        _, sel = lax.top_k(idx, k_top)
        valid = sel <= q_pos[None, :, None]
        c_sel = gather(c_kv, sel)
        logits = jnp.einsum("bqhr,bqkr->bqhk", ql, c_sel).astype(jnp.float32) * (HEAD_DIM ** -0.5)
        logits = jnp.where(valid[:, :, None, :], logits, -jnp.inf)
        probs = jax.nn.softmax(logits, axis=-1).astype(c_sel.dtype)
        return jnp.einsum("bqhk,bqkr->bqhr", probs, c_sel)

    o_lat = lax.map(one_block, (jnp.arange(n_blk), to_blocks(q_lat), to_blocks(q_idx), to_blocks(w_idx)))
    o_lat = jnp.moveaxis(o_lat, 0, 1).reshape(b, s, N_HEADS, KV_RANK)
    o = jnp.einsum("bshr,rhd->bshd", o_lat, w_uv)
    return o.reshape(b, s, N_HEADS * HEAD_DIM)


def pool_mixer(xp, w_pool, scale):
    b, s, _ = xp.shape
    x32 = xp.astype(jnp.float32)
    cs = jnp.concatenate([jnp.zeros((b, 1, POOL_WIDTH), jnp.float32), jnp.cumsum(x32, axis=1)], axis=1)
    pos = jnp.arange(s)
    outs = []
    for g, w in enumerate(POOL_WINDOWS):
        sl = slice(g * POOL_GROUP, (g + 1) * POOL_GROUP)
        lo = jnp.maximum(pos + 1 - w, 0)
        win_sum = cs[:, 1:, sl] - cs[:, lo, sl]
        count = jnp.minimum(pos + 1, w).astype(jnp.float32)[None, :, None]
        outs.append(win_sum / count - x32[:, :, sl])
    pooled = jnp.stack(outs, axis=2).astype(xp.dtype)
    mixed = jnp.einsum("bsgc,gcd->bsgd", pooled, w_pool).reshape(b, s, POOL_WIDTH)
    return mixed * scale


def causal_depthwise_conv(x, w, bias):
    k = w.shape[0]
    y = lax.conv_general_dilated(x, w[:, None, :], window_strides=(1,), padding=[(k - 1, 0)],
                                 dimension_numbers=("NWC", "WIO", "NWC"),
                                 feature_group_count=x.shape[-1])
    return y + bias


def segsum(a):
    t = a.shape[-1]
    rep = jnp.broadcast_to(a[..., None], a.shape + (t,))
    rep = jnp.where(jnp.tril(jnp.ones((t, t), bool), -1), rep, 0.0)
    ss = jnp.cumsum(rep, axis=-2)
    return jnp.where(jnp.tril(jnp.ones((t, t), bool), 0), ss, -jnp.inf)


def ssd_chunked(xdt, da, bm, cm):
    b, s, h, p = xdt.shape
    g, n = bm.shape[2], bm.shape[3]
    e = h // g
    c = s // CHUNK
    dt_ = xdt.dtype
    X = xdt.reshape(b, c, CHUNK, g, e, p)
    A = da.reshape(b, c, CHUNK, g, e).transpose(0, 3, 4, 1, 2)
    Bc = bm.reshape(b, c, CHUNK, g, n)
    Cc = cm.reshape(b, c, CHUNK, g, n)
    a_cs = jnp.cumsum(A, axis=-1)
    L = jnp.exp(segsum(A)).astype(dt_)
    cb = jnp.einsum("bclgn,bcsgn->bgcls", Cc, Bc)
    y_diag = jnp.einsum("bgecls,bcsgep->bclgep", cb[:, :, None] * L, X)
    decay_states = jnp.exp(a_cs[..., -1:] - a_cs).astype(dt_)
    states = jnp.einsum("bclgn,bgecl,bclgep->bcgepn", Bc, decay_states, X)
    chunk_decay = jnp.exp(a_cs[..., -1]).astype(dt_)

    def step(carry, inp):
        st, dec = inp
        return carry * dec[..., None, None] + st, carry

    init = jnp.zeros((b, g, e, p, n), dt_)
    _, prev = lax.scan(step, init, (jnp.moveaxis(states, 1, 0), jnp.moveaxis(chunk_decay, 3, 0)))
    prev = jnp.moveaxis(prev, 0, 1)
    y_off = jnp.einsum("bclgn,bcgepn,bgecl->bclgep", Cc, prev, jnp.exp(a_cs).astype(dt_))
    return (y_diag + y_off).reshape(b, s, h, p)


def ssd_mixer(z, xbc, dt_raw, conv_w, conv_b, dt_bias, a_log, d_skip, norm_g):
    b, s, _ = z.shape
    xbc = jax.nn.silu(causal_depthwise_conv(xbc, conv_w, conv_b))
    xs, bm, cm = jnp.split(xbc, [D_INNER, D_INNER + N_GROUPS * D_STATE], axis=-1)
    xs = xs.reshape(b, s, SSD_HEADS, SSD_HEAD_DIM)
    bm = bm.reshape(b, s, N_GROUPS, D_STATE)
    cm = cm.reshape(b, s, N_GROUPS, D_STATE)
    dt = jax.nn.softplus((dt_raw + dt_bias).astype(jnp.float32))
    da = dt * (-jnp.exp(a_log.astype(jnp.float32)))
    y = ssd_chunked(xs * dt[..., None].astype(xs.dtype), da, bm, cm)
    y = y + xs * d_skip[:, None]
    y = y.reshape(b, s, D_INNER) * jax.nn.silu(z)
    return rmsnorm(y, norm_g)


def setup_inputs(seed: int = 0) -> dict:
    key = jax.random.key(seed)
    ks = iter(jax.random.split(key, 40))

    def w(shape, fan_in):
        return jax.random.normal(next(ks), shape, jnp.float32) * (fan_in ** -0.5)

    def gain(shape):
        return 1.0 + 0.02 * jax.random.normal(next(ks), shape, jnp.float32)

    L = DEPTH
    x = jax.random.normal(next(ks), (BATCH, SEQ, D_MODEL), jnp.float32)
    p = jax.random.normal(next(ks), (DEPTH, BATCH, SEQ, PLE_DIM), jnp.float32)
    dt0 = jnp.exp(jax.random.uniform(next(ks), (L, SSD_HEADS), jnp.float32,
                                     jnp.log(1e-3), jnp.log(1e-1)))
    dt_bias = dt0 + jnp.log(-jnp.expm1(-dt0))
    a_log = jnp.log(jax.random.uniform(next(ks), (L, SSD_HEADS), jnp.float32, 1.0, 16.0))
    return {
        "x": x,
        "p": p,
        "ffn1_norm": gain((L, D_MODEL)),
        "ffn1_w_gate": w((L, D_MODEL, D_FF), D_MODEL),
        "ffn1_w_up": w((L, D_MODEL, D_FF), D_MODEL),
        "ffn1_w_down": w((L, D_FF, D_MODEL), D_FF),
        "mix_norm": gain((L, D_MODEL)),
        "w_in": w((L, D_MODEL, IN_COLS), D_MODEL),
        "kv_norm": gain((L, KV_RANK)),
        "idx_k_norm": gain((L, IDX_DIM)),
        "w_uk": w((L, KV_RANK, N_HEADS, HEAD_DIM), KV_RANK),
        "w_uv": w((L, KV_RANK, N_HEADS, HEAD_DIM), KV_RANK),
        "pool_w": w((L, len(POOL_WINDOWS), POOL_GROUP, POOL_GROUP), POOL_GROUP),
        "pool_scale": gain((L, POOL_WIDTH)),
        "conv_w": w((L, CONV_WIDTH, CONV_DIM), CONV_WIDTH),
        "conv_b": 0.01 * jax.random.normal(next(ks), (L, CONV_DIM), jnp.float32),
        "dt_bias": dt_bias,
        "a_log": a_log,
        "d_skip": gain((L, SSD_HEADS)),
        "ssd_norm": gain((L, D_INNER)),
        "w_br_attn": w((L, N_HEADS * HEAD_DIM, D_MODEL), N_HEADS * HEAD_DIM),
        "w_br_pool": w((L, POOL_WIDTH, D_MODEL), POOL_WIDTH),
        "w_br_ssd": w((L, D_INNER, D_MODEL), D_INNER),
        "w_out": w((L, D_MODEL, D_MODEL), D_MODEL),
        "ffn2_norm": gain((L, D_MODEL)),
        "ffn2_w_gate": w((L, D_MODEL, D_FF), D_MODEL),
        "ffn2_w_up": w((L, D_MODEL, D_FF), D_MODEL),
        "ffn2_w_down": w((L, D_FF, D_MODEL), D_FF),
        "ple_norm": gain((L, D_MODEL)),
        "ple_w_gate": w((L, D_MODEL, D_MODEL), D_MODEL),
        "ple_w_proj": w((L, PLE_DIM, D_MODEL), PLE_DIM),
        "final_norm": gain((D_MODEL,)),
    }


def reference(x, p, ffn1_norm, ffn1_w_gate, ffn1_w_up, ffn1_w_down, mix_norm, w_in,
              kv_norm, idx_k_norm, w_uk, w_uv, pool_w, pool_scale, conv_w, conv_b,
              dt_bias, a_log, d_skip, ssd_norm, w_br_attn, w_br_pool, w_br_ssd, w_out,
              ffn2_norm, ffn2_w_gate, ffn2_w_up, ffn2_w_down, ple_norm, ple_w_gate,
              ple_w_proj, final_norm):
    b, s, _ = x.shape
    offs = [int(v) for v in np.cumsum(IN_SPLITS)[:-1]]
    h = x
    for i in range(DEPTH):
        h = h + 0.5 * swiglu(rmsnorm(h, ffn1_norm[i]), ffn1_w_gate[i], ffn1_w_up[i], ffn1_w_down[i])
        u = rmsnorm(h, mix_norm[i])
        (q, c_kv, q_idx, w_idx, k_idx, x_pool, z, xbc, dt_raw, gate_raw) = jnp.split(u @ w_in[i], offs, axis=-1)
        y_attn = dsa_attention(q.reshape(b, s, N_HEADS, HEAD_DIM), rmsnorm(c_kv, kv_norm[i]),
                               q_idx.reshape(b, s, IDX_HEADS, IDX_DIM), w_idx,
                               rmsnorm(k_idx, idx_k_norm[i]), w_uk[i], w_uv[i])
        y_pool = pool_mixer(x_pool, pool_w[i], pool_scale[i])
        y_ssd = ssd_mixer(z, xbc, dt_raw, conv_w[i], conv_b[i], dt_bias[i], a_log[i], d_skip[i], ssd_norm[i])
        g_attn, g_pool, g_ssd = jnp.split(jax.nn.sigmoid(gate_raw), N_BRANCHES, axis=-1)
        merged = (g_attn * (y_attn @ w_br_attn[i]) + g_pool * (y_pool @ w_br_pool[i])
                  + g_ssd * (y_ssd @ w_br_ssd[i]))
        h = h + merged @ w_out[i]
        h = h + 0.5 * swiglu(rmsnorm(h, ffn2_norm[i]), ffn2_w_gate[i], ffn2_w_up[i], ffn2_w_down[i])
        h = h + jax.nn.sigmoid(rmsnorm(h, ple_norm[i]) @ ple_w_gate[i]) * (p[i] @ ple_w_proj[i])
    return rmsnorm(h, final_norm)
```

```python
import functools

import jax
import jax.numpy as jnp
from jax import lax
from jax.experimental import pallas as pl
from jax.experimental.pallas import tpu as pltpu

_MXU = jnp.bfloat16
F32 = jnp.float32

D_MODEL = 1024
N_HEADS = 8
HEAD_DIM = 64
KV_RANK = 256
IDX_HEADS = 8
IDX_DIM = 64
TOPK_MAX = 256
Q_BLOCK = 128
POOL_WINDOWS = (2, 4, 8, 16)
POOL_WIDTH = 512
POOL_GROUP = 128
SSD_HEADS = 16
SSD_HEAD_DIM = 64
D_INNER = 1024
N_GROUPS = 2
D_STATE = 128
CONV_WIDTH = 4
CONV_DIM = D_INNER + 2 * N_GROUPS * D_STATE
CHUNK = 128
D_FF = 2816
PLE_DIM = 256
EPS = 1e-6

C_XBC = 0
C_Q = 1536
C_Z = 2048
C_GATE = 3072
C_QIDX = 6144
C_POOL = 6656
C_CKV = 7168
C_SMALL = 7424
PROJ_COLS = 7680
SM_KIDX = 0
SM_WIDX = 64
SM_DT = 72

INT_MIN = -2147483648
NEG_BIG = -1e30
VMEM_LIMIT = 56 * 1024 * 1024


def _cparams(sem):
    return pltpu.CompilerParams(dimension_semantics=sem, vmem_limit_bytes=VMEM_LIMIT)


def _rms(x, g):
    return x * lax.rsqrt(jnp.mean(x * x, axis=-1, keepdims=True) + EPS) * g


def _dot(a, b):
    return jnp.dot(a, b, preferred_element_type=F32)


def _sigmoid(x):
    return 1.0 / (1.0 + jnp.exp(-x))


def _silu(x):
    return x * _sigmoid(x)


def _ffn_kernel(h_ref, g_ref, wg_ref, wu_ref, wd_ref, o_ref, xn_ref, acc_ref):
    k = pl.program_id(1)

    @pl.when(k == 0)
    def _():
        xn_ref[...] = _rms(h_ref[...], g_ref[...]).astype(xn_ref.dtype)
        acc_ref[...] = jnp.zeros_like(acc_ref)

    xn = xn_ref[...]
    g = _dot(xn, wg_ref[...])
    u = _dot(xn, wu_ref[...])
    a = _silu(g) * u
    acc_ref[...] += _dot(a.astype(_MXU), wd_ref[...])

    @pl.when(k == pl.num_programs(1) - 1)
    def _():
        o_ref[...] = h_ref[...] + 0.5 * acc_ref[...]


def _ffn(h, g, wg, wu, wd, *, tm, tf):
    t, d = h.shape
    ff = wg.shape[1]
    return pl.pallas_call(
        _ffn_kernel,
        grid=(t // tm, ff // tf),
        in_specs=[
            pl.BlockSpec((tm, d), lambda i, k: (i, 0)),
            pl.BlockSpec((1, d), lambda i, k: (0, 0)),
            pl.BlockSpec((d, tf), lambda i, k: (0, k)),
            pl.BlockSpec((d, tf), lambda i, k: (0, k)),
            pl.BlockSpec((tf, d), lambda i, k: (k, 0)),
        ],
        out_specs=pl.BlockSpec((tm, d), lambda i, k: (i, 0)),
        out_shape=jax.ShapeDtypeStruct((t, d), F32),
        scratch_shapes=[pltpu.VMEM((tm, d), _MXU), pltpu.VMEM((tm, d), F32)],
        compiler_params=_cparams(("parallel", "arbitrary")),
        name="ffn",
    )(h, g, wg, wu, wd)


def _inproj_kernel(h_ref, g_ref, w_ref, o_ref, xn_ref):
    @pl.when(pl.program_id(1) == 0)
    def _():
        xn_ref[...] = _rms(h_ref[...], g_ref[...]).astype(xn_ref.dtype)

    o_ref[...] = _dot(xn_ref[...], w_ref[...])


def _inproj(h, g, w, *, tm, tn):
    t, d = h.shape
    n = w.shape[1]
    return pl.pallas_call(
        _inproj_kernel,
        grid=(t // tm, n // tn),
        in_specs=[
            pl.BlockSpec((tm, d), lambda i, j: (i, 0)),
            pl.BlockSpec((1, d), lambda i, j: (0, 0)),
            pl.BlockSpec((d, tn), lambda i, j: (0, j)),
        ],
        out_specs=pl.BlockSpec((tm, tn), lambda i, j: (i, j)),
        out_shape=jax.ShapeDtypeStruct((t, n), F32),
        scratch_shapes=[pltpu.VMEM((tm, d), _MXU)],
        compiler_params=_cparams(("parallel", "arbitrary")),
        name="inproj",
    )(h, g, w)


def _kvprep_kernel(ckv_ref, sm_ref, gkv_ref, gk_ref, ckv_o, ckvT_o, kidx_o):
    c = _rms(ckv_ref[...], gkv_ref[...])
    ckv_o[0] = c.astype(ckv_o.dtype)
    ckvT_o[0, 0] = c.T.astype(ckvT_o.dtype)
    k = sm_ref[...][:, SM_KIDX:SM_KIDX + IDX_DIM]
    kidx_o[0] = _rms(k, gk_ref[...]).astype(kidx_o.dtype)


def _kvprep(proj, gkv, gk, *, b, s, kb):
    nk = s // kb
    return pl.pallas_call(
        _kvprep_kernel,
        grid=(b, nk),
        in_specs=[
            pl.BlockSpec((kb, KV_RANK), lambda bi, i: (bi * nk + i, C_CKV // KV_RANK)),
            pl.BlockSpec((kb, 128), lambda bi, i: (bi * nk + i, C_SMALL // 128)),
            pl.BlockSpec((1, KV_RANK), lambda bi, i: (0, 0)),
            pl.BlockSpec((1, IDX_DIM), lambda bi, i: (0, 0)),
        ],
        out_specs=[
            pl.BlockSpec((1, kb, KV_RANK), lambda bi, i: (bi, i, 0)),
            pl.BlockSpec((1, 1, KV_RANK, kb), lambda bi, i: (bi, i, 0, 0)),
            pl.BlockSpec((1, kb, IDX_DIM), lambda bi, i: (bi, i, 0)),
        ],
        out_shape=[
            jax.ShapeDtypeStruct((b, s, KV_RANK), _MXU),
            jax.ShapeDtypeStruct((b, nk, KV_RANK, kb), _MXU),
            jax.ShapeDtypeStruct((b, s, IDX_DIM), _MXU),
        ],
        compiler_params=_cparams(("parallel", "parallel")),
        name="kvprep",
    )(proj, proj, gkv, gk)


def _attn_kernel(q_ref, qi_ref, sm_ref, ckv_ref, ckvT_ref, kidx_ref, wuk_ref, wuvT_ref, o_ref,
                 key_ref, acc_ref, m_ref, l_ref, *, k_top, kb):
    j = pl.program_id(1)
    nblk = (j * Q_BLOCK + Q_BLOCK + kb - 1) // kb
    nh = N_HEADS
    qb = Q_BLOCK

    qT = q_ref[...].T
    qiT = qi_ref[...].T
    smT = sm_ref[...].T
    w_row = jnp.concatenate([smT[SM_WIDX + h:SM_WIDX + h + 1, :] for h in range(nh)], axis=1)
    qiT_stack = jnp.concatenate(
        [qiT[h * IDX_DIM:(h + 1) * IDX_DIM, :] for h in range(nh)], axis=1).astype(_MXU)
    scale = HEAD_DIM ** -0.5
    qlatT = jnp.concatenate(
        [_dot(wuk_ref[h], qT[h * HEAD_DIM:(h + 1) * HEAD_DIM, :].astype(_MXU)) for h in range(nh)],
        axis=1)
    qlatT = (qlatT * scale).astype(_MXU)

    qpos = j * qb + lax.broadcasted_iota(jnp.int32, (kb, qb), 1)

    def idx_body(bi, carry):
        off = pl.multiple_of(bi * kb, kb)
        kblk = kidx_ref[0, pl.ds(off, kb), :]
        r = jnp.maximum(_dot(kblk, qiT_stack), 0.0) * w_row
        sc = r[:, 0:qb]
        for h in range(1, nh):
            sc = sc + r[:, h * qb:(h + 1) * qb]
        bits = pltpu.bitcast(sc, jnp.int32)
        key = jnp.where(bits < 0, bits ^ jnp.int32(0x7FFFFFFF), bits)
        kpos = off + lax.broadcasted_iota(jnp.int32, (kb, qb), 0)
        key_ref[pl.ds(off, kb), :] = jnp.where(kpos <= qpos, key, jnp.int32(INT_MIN))
        return carry

    lax.fori_loop(0, nblk, idx_body, 0)

    def bit_body(i, prefix):
        cand_u = prefix | jnp.left_shift(jnp.int32(1), 31 - i)
        cand_s = cand_u ^ jnp.int32(INT_MIN)

        def cnt_body(bi, c):
            off = pl.multiple_of(bi * kb, kb)
            ge = jnp.where(key_ref[pl.ds(off, kb), :] >= cand_s, 1, 0).astype(jnp.int32)
            return c + ge.reshape(kb // 8, 8, qb).sum(axis=0)

        c8 = lax.fori_loop(0, nblk, cnt_body, jnp.zeros((8, qb), jnp.int32))
        cnt = c8.sum(axis=0, keepdims=True)
        return jnp.where(cnt >= k_top, cand_u, prefix)

    prefix = lax.fori_loop(0, 32, bit_body, jnp.zeros((1, qb), jnp.int32))
    thr = jnp.maximum(prefix ^ jnp.int32(INT_MIN), jnp.int32(INT_MIN + 1))

    m_ref[...] = jnp.full(m_ref.shape, NEG_BIG, F32)
    l_ref[...] = jnp.zeros_like(l_ref)
    acc_ref[...] = jnp.zeros_like(acc_ref)

    def att_body(bi, carry):
        off = pl.multiple_of(bi * kb, kb)
        ck = ckv_ref[0, pl.ds(off, kb), :]
        ckT = ckvT_ref[0, bi]
        s = _dot(ck, qlatT)
        mask = key_ref[pl.ds(off, kb), :] >= thr
        s = jnp.concatenate(
            [jnp.where(mask, s[:, h * qb:(h + 1) * qb], NEG_BIG) for h in range(nh)], axis=1)
        m_old = m_ref[...]
        m_new = jnp.maximum(m_old, jnp.max(s, axis=0, keepdims=True))
        alpha = jnp.exp(m_old - m_new)
        p = jnp.exp(s - m_new)
        l_ref[...] = alpha * l_ref[...] + jnp.sum(p, axis=0, keepdims=True)
        acc_ref[...] = alpha * acc_ref[...] + _dot(ckT, p.astype(_MXU))
        m_ref[...] = m_new
        return carry

    lax.fori_loop(0, nblk, att_body, 0)

    o = (acc_ref[...] / l_ref[...]).astype(_MXU)
    yT = jnp.concatenate([_dot(wuvT_ref[h], o[:, h * qb:(h + 1) * qb]) for h in range(nh)], axis=0)
    o_ref[...] = yT.T


def _attention(proj, ckv, ckvT, kidx, wuk, wuvT, *, b, s, kb):
    nq = s // Q_BLOCK
    k_top = min(TOPK_MAX, s // 4)
    kern = functools.partial(_attn_kernel, k_top=k_top, kb=kb)
    hq = N_HEADS * Q_BLOCK
    return pl.pallas_call(
        kern,
        grid=(b, nq),
        in_specs=[
            pl.BlockSpec((Q_BLOCK, 512), lambda bi, j: (bi * nq + j, C_Q // 512)),
            pl.BlockSpec((Q_BLOCK, 512), lambda bi, j: (bi * nq + j, C_QIDX // 512)),
            pl.BlockSpec((Q_BLOCK, 128), lambda bi, j: (bi * nq + j, C_SMALL // 128)),
            pl.BlockSpec((1, s, KV_RANK), lambda bi, j: (bi, 0, 0)),
            pl.BlockSpec((1, s // kb, KV_RANK, kb), lambda bi, j: (bi, 0, 0, 0)),
            pl.BlockSpec((1, s, IDX_DIM), lambda bi, j: (bi, 0, 0)),
            pl.BlockSpec((N_HEADS, KV_RANK, HEAD_DIM), lambda bi, j: (0, 0, 0)),
            pl.BlockSpec((N_HEADS, HEAD_DIM, KV_RANK), lambda bi, j: (0, 0, 0)),
        ],
        out_specs=pl.BlockSpec((Q_BLOCK, N_HEADS * HEAD_DIM), lambda bi, j: (bi * nq + j, 0)),
        out_shape=jax.ShapeDtypeStruct((b * s, N_HEADS * HEAD_DIM), F32),
        scratch_shapes=[
            pltpu.VMEM((s, Q_BLOCK), jnp.int32),
            pltpu.VMEM((KV_RANK, hq), F32),
            pltpu.VMEM((1, hq), F32),
            pltpu.VMEM((1, hq), F32),
        ],
        compiler_params=_cparams(("parallel", "arbitrary")),
        name="dsa_attn",
    )(proj, proj, proj, ckv, ckvT, kidx, wuk, wuvT)


def _pool_kernel(x_ref, halo_ref, pw_ref, ps_ref, o_ref, *, tm):
    i = pl.program_id(1)
    x = x_ref[...]
    halo = jnp.where(i > 0, halo_ref[...], 0.0)
    xe = jnp.concatenate([halo, x], axis=0)
    pos = i * tm + lax.broadcasted_iota(jnp.int32, (tm, POOL_GROUP), 0)
    outs = []
    for g, w in enumerate(POOL_WINDOWS):
        sl = slice(g * POOL_GROUP, (g + 1) * POOL_GROUP)
        cur = xe[:, sl]
        span = 1
        while span < w:
            n = cur.shape[0]
            cur = cur[span:, :] + cur[:n - span, :]
            span *= 2
        win = cur[cur.shape[0] - tm:, :]
        count = jnp.minimum(pos + 1, w).astype(F32)
        pooled = win / count - x[:, sl]
        outs.append(_dot(pooled.astype(_MXU), pw_ref[g]))
    o_ref[...] = jnp.concatenate(outs, axis=1) * ps_ref[...]


def _pool(proj, pw, ps, *, b, s, tm):
    nt = s // tm
    kern = functools.partial(_pool_kernel, tm=tm)
    return pl.pallas_call(
        kern,
        grid=(b, nt),
        in_specs=[
            pl.BlockSpec((tm, POOL_WIDTH), lambda bi, i: (bi * nt + i, C_POOL // POOL_WIDTH)),
            pl.BlockSpec((16, POOL_WIDTH),
                         lambda bi, i: (jnp.maximum((bi * s + i * tm) // 16 - 1, 0), C_POOL // POOL_WIDTH)),
            pl.BlockSpec((len(POOL_WINDOWS), POOL_GROUP, POOL_GROUP), lambda bi, i: (0, 0, 0)),
            pl.BlockSpec((1, POOL_WIDTH), lambda bi, i: (0, 0)),
        ],
        out_specs=pl.BlockSpec((tm, POOL_WIDTH), lambda bi, i: (bi * nt + i, 0)),
        out_shape=jax.ShapeDtypeStruct((b * s, POOL_WIDTH), F32),
        compiler_params=_cparams(("parallel", "parallel")),
        name="pool",
    )(proj, proj, pw, ps)


def _ssd_kernel(xbc_ref, z_ref, sm_ref, cw_ref, cb_ref, dtb_row_ref, dtb_col_ref, a_x_ref, a_col_ref,
                dsk_x_ref, expand_ref, ng_ref, o_ref, state_ref, tail_ref):
    c = pl.program_id(1)
    L = CHUNK
    P = SSD_HEAD_DIM

    @pl.when(c == 0)
    def _():
        state_ref[...] = jnp.zeros_like(state_ref)
        tail_ref[...] = jnp.zeros_like(tail_ref)

    x_raw = xbc_ref[...]
    xe = jnp.concatenate([tail_ref[...], x_raw], axis=0)
    cw = cw_ref[...]
    conv = cb_ref[...] + cw[3:4, :] * x_raw
    for k in range(CONV_WIDTH - 1):
        sh = CONV_WIDTH - 1 - k
        conv = conv + cw[k:k + 1, :] * xe[8 - sh:8 - sh + L, :]
    tail_ref[...] = x_raw[L - 8:, :]
    xc = _silu(conv)
    xs = xc[:, :D_INNER]
    bm = xc[:, D_INNER:D_INNER + N_GROUPS * D_STATE]
    cm = xc[:, D_INNER + N_GROUPS * D_STATE:]

    sm = sm_ref[...]
    smT = sm.T
    dt = jax.nn.softplus(sm[:, SM_DT:SM_DT + SSD_HEADS] + dtb_row_ref[...])
    dtT = jax.nn.softplus(smT[SM_DT:SM_DT + SSD_HEADS, :] + dtb_col_ref[...])
    hi = lax.Precision.HIGHEST
    dtx = jnp.dot(dt, expand_ref[...], precision=hi, preferred_element_type=F32)
    dax = dtx * a_x_ref[...]
    daT = dtT * a_col_ref[...]
    row = lax.broadcasted_iota(jnp.int32, (L, L), 0)
    col = lax.broadcasted_iota(jnp.int32, (L, L), 1)
    lower = row >= col
    tril = jnp.where(lower, 1.0, 0.0).astype(F32)
    triu = jnp.where(row <= col, 1.0, 0.0).astype(F32)
    acs_x = jnp.dot(tril, dax, precision=hi, preferred_element_type=F32)
    acsT = jnp.dot(daT, triu, precision=hi, preferred_element_type=F32)
    a_last = acs_x[L - 1:L, :]
    ea_x = jnp.exp(acs_x)
    decay_x = jnp.exp(a_last - acs_x)
    cd_x = jnp.exp(a_last)

    X = xs * dtx
    Xb = X.astype(_MXU)
    Xd = (X * decay_x).astype(_MXU)
    state = state_ref[...]
    ys = []
    hpg = SSD_HEADS // N_GROUPS
    for g in range(N_GROUPS):
        bg = bm[:, g * D_STATE:(g + 1) * D_STATE]
        cg = cm[:, g * D_STATE:(g + 1) * D_STATE].astype(_MXU)
        bgT = bg.T.astype(_MXU)
        cb = _dot(cg, bgT)
        gs = slice(g * hpg * P, (g + 1) * hpg * P)
        y_off = _dot(cg, state[:, gs].astype(_MXU)) * ea_x[:, gs]
        state_ref[:, gs] = state[:, gs] * cd_x[:, gs] + _dot(bgT, Xd[:, gs])
        for e in range(hpg):
            h = g * hpg + e
            hs = slice(h * P, (h + 1) * P)
            diff = acs_x[:, h * P:h * P + 1] - acsT[h:h + 1, :]
            lmat = jnp.exp(jnp.where(lower, diff, -jnp.inf))
            m = (cb * lmat).astype(_MXU)
            ys.append(_dot(m, Xb[:, hs]) + y_off[:, e * P:(e + 1) * P])
    y = jnp.concatenate(ys, axis=1) + xs * dsk_x_ref[...]
    y = y * _silu(z_ref[...])
    o_ref[...] = _rms(y, ng_ref[...])


def _ssd(proj, cw, cb, dtb, a_log, dsk, ng, *, b, s):
    nc = s // CHUNK
    rep = lambda v: jnp.repeat(v, SSD_HEAD_DIM)[None, :]
    a = -jnp.exp(a_log)
    expand = jnp.repeat(jnp.eye(SSD_HEADS, dtype=F32), SSD_HEAD_DIM, axis=1)
    const2 = lambda bi, c: (0, 0)
    return pl.pallas_call(
        _ssd_kernel,
        grid=(b, nc),
        in_specs=[
            pl.BlockSpec((CHUNK, CONV_DIM), lambda bi, c: (bi * nc + c, C_XBC // CONV_DIM)),
            pl.BlockSpec((CHUNK, D_INNER), lambda bi, c: (bi * nc + c, C_Z // D_INNER)),
            pl.BlockSpec((CHUNK, 128), lambda bi, c: (bi * nc + c, C_SMALL // 128)),
            pl.BlockSpec((CONV_WIDTH, CONV_DIM), const2),
            pl.BlockSpec((1, CONV_DIM), const2),
            pl.BlockSpec((1, SSD_HEADS), const2),
            pl.BlockSpec((SSD_HEADS, 1), const2),
            pl.BlockSpec((1, D_INNER), const2),
            pl.BlockSpec((SSD_HEADS, 1), const2),
            pl.BlockSpec((1, D_INNER), const2),
            pl.BlockSpec((SSD_HEADS, D_INNER), const2),
            pl.BlockSpec((1, D_INNER), const2),
        ],
        out_specs=pl.BlockSpec((CHUNK, D_INNER), lambda bi, c: (bi * nc + c, 0)),
        out_shape=jax.ShapeDtypeStruct((b * s, D_INNER), F32),
        scratch_shapes=[pltpu.VMEM((D_STATE, D_INNER), F32), pltpu.VMEM((8, CONV_DIM), F32)],
        compiler_params=_cparams(("parallel", "arbitrary")),
        name="ssd",
    )(proj, proj, proj, cw, cb[None, :], dtb[None, :], dtb[:, None], rep(a), a[:, None],
      rep(dsk), expand, ng[None, :])


def _merge_kernel(ya_ref, yp_ref, ys_ref, ga_ref, gp_ref, gs_ref, h_ref, wa_ref, wp_ref, ws_ref,
                  wo_ref, o_ref):
    m = _sigmoid(ga_ref[...]) * _dot(ya_ref[...].astype(_MXU), wa_ref[...])
    m = m + _sigmoid(gp_ref[...]) * _dot(yp_ref[...].astype(_MXU), wp_ref[...])
    m = m + _sigmoid(gs_ref[...]) * _dot(ys_ref[...].astype(_MXU), ws_ref[...])
    o_ref[...] = h_ref[...] + _dot(m.astype(_MXU), wo_ref[...])


def _merge(ya, yp, ys, proj, h, wa, wp, ws, wo, *, tm):
    t, d = h.shape
    row = lambda i: (i, 0)
    const = lambda i: (0, 0)
    gate = lambda k: (lambda i: (i, C_GATE // d + k))
    return pl.pallas_call(
        _merge_kernel,
        grid=(t // tm,),
        in_specs=[
            pl.BlockSpec((tm, ya.shape[1]), row),
            pl.BlockSpec((tm, yp.shape[1]), row),
            pl.BlockSpec((tm, ys.shape[1]), row),
            pl.BlockSpec((tm, d), gate(0)),
            pl.BlockSpec((tm, d), gate(1)),
            pl.BlockSpec((tm, d), gate(2)),
            pl.BlockSpec((tm, d), row),
            pl.BlockSpec(wa.shape, const),
            pl.BlockSpec(wp.shape, const),
            pl.BlockSpec(ws.shape, const),
            pl.BlockSpec(wo.shape, const),
        ],
        out_specs=pl.BlockSpec((tm, d), row),
        out_shape=jax.ShapeDtypeStruct((t, d), F32),
        compiler_params=_cparams(("parallel",)),
        name="merge",
    )(ya, yp, ys, proj, proj, proj, h, wa, wp, ws, wo)


def _ple_kernel(h_ref, p_ref, g_ref, wg_ref, wp_ref, fg_ref, o_ref, *, final):
    h = h_ref[...]
    gate = _sigmoid(_dot(_rms(h, g_ref[...]).astype(_MXU), wg_ref[...]))
    h = h + gate * _dot(p_ref[...].astype(_MXU), wp_ref[...])
    if final:
        h = _rms(h, fg_ref[...])
    o_ref[...] = h


def _ple(h, p, g, wg, wp, fg, *, tm, final):
    t, d = h.shape
    row = lambda i: (i, 0)
    const = lambda i: (0, 0)
    return pl.pallas_call(
        functools.partial(_ple_kernel, final=final),
        grid=(t // tm,),
        in_specs=[
            pl.BlockSpec((tm, d), row),
            pl.BlockSpec((tm, p.shape[1]), row),
            pl.BlockSpec((1, d), const),
            pl.BlockSpec(wg.shape, const),
            pl.BlockSpec(wp.shape, const),
            pl.BlockSpec((1, d), const),
        ],
        out_specs=pl.BlockSpec((tm, d), row),
        out_shape=jax.ShapeDtypeStruct((t, d), F32),
        compiler_params=_cparams(("parallel",)),
        name="ple",
    )(h, p, g, wg, wp, fg)


def _reorder_w_in(w):
    o_q, o_ckv, o_qidx, o_widx, o_kidx, o_pool, o_z, o_xbc, o_dt, o_gate = (
        0, 512, 768, 1280, 1288, 1352, 1864, 2888, 4424, 4440)
    d = w.shape[0]
    small = jnp.concatenate([
        w[:, o_kidx:o_kidx + IDX_DIM], w[:, o_widx:o_widx + IDX_HEADS], w[:, o_dt:o_dt + SSD_HEADS],
        jnp.zeros((d, 128 - IDX_DIM - IDX_HEADS - SSD_HEADS), w.dtype)], axis=1)
    cols = [
        w[:, o_xbc:o_xbc + CONV_DIM], w[:, o_q:o_q + 512], w[:, o_z:o_z + D_INNER],
        w[:, o_gate:o_gate + 3 * D_MODEL], w[:, o_qidx:o_qidx + 512], w[:, o_pool:o_pool + POOL_WIDTH],
        w[:, o_ckv:o_ckv + KV_RANK], small,
    ]
    out = jnp.concatenate(cols, axis=1)
    return jnp.pad(out, ((0, 0), (0, PROJ_COLS - out.shape[1])))


def kernel(x, p, ffn1_norm, ffn1_w_gate, ffn1_w_up, ffn1_w_down, mix_norm, w_in, kv_norm, idx_k_norm, w_uk, w_uv, pool_w, pool_scale, conv_w, conv_b, dt_bias, a_log, d_skip, ssd_norm, w_br_attn, w_br_pool, w_br_ssd, w_out, ffn2_norm, ffn2_w_gate, ffn2_w_up, ffn2_w_down, ple_norm, ple_w_gate, ple_w_proj, final_norm):
    b, s, d = x.shape
    depth = w_in.shape[0]
    t = b * s
    tm = min(1024, t)
    tm_small = min(512, t)
    kb = min(512, s)
    tf = 1408
    mx = lambda a: a.astype(_MXU)

    h = x.reshape(t, d)
    for i in range(depth):
        h = _ffn(h, ffn1_norm[i][None], mx(ffn1_w_gate[i]), mx(ffn1_w_up[i]), mx(ffn1_w_down[i]),
                 tm=tm_small, tf=tf)
        proj = _inproj(h, mix_norm[i][None], mx(_reorder_w_in(w_in[i])), tm=tm, tn=512)
        ckv, ckvT, kidx = _kvprep(proj, kv_norm[i][None], idx_k_norm[i][None], b=b, s=s, kb=kb)
        wuk = mx(jnp.transpose(w_uk[i], (1, 0, 2)))
        wuvT = mx(jnp.transpose(w_uv[i], (1, 2, 0)))
        y_attn = _attention(proj, ckv, ckvT, kidx, wuk, wuvT, b=b, s=s, kb=kb)
        y_pool = _pool(proj, mx(pool_w[i]), pool_scale[i][None], b=b, s=s, tm=tm_small)
        y_ssd = _ssd(proj, conv_w[i], conv_b[i], dt_bias[i], a_log[i], d_skip[i], ssd_norm[i], b=b, s=s)
        h = _merge(y_attn, y_pool, y_ssd, proj, h, mx(w_br_attn[i]), mx(w_br_pool[i]),
                   mx(w_br_ssd[i]), mx(w_out[i]), tm=tm_small)
        h = _ffn(h, ffn2_norm[i][None], mx(ffn2_w_gate[i]), mx(ffn2_w_up[i]), mx(ffn2_w_down[i]),
                 tm=tm_small, tf=tf)
        h = _ple(h, p[i].reshape(t, -1), ple_norm[i][None], mx(ple_w_gate[i]), mx(ple_w_proj[i]),
                 final_norm[None], tm=tm_small, final=(i == depth - 1))
    return h.reshape(b, s, d)
```

```python
import functools

import jax
import jax.numpy as jnp
from jax import lax
from jax.experimental import pallas as pl
from jax.experimental.pallas import tpu as pltpu

_MXU = jnp.bfloat16
F32 = jnp.float32

D_MODEL = 1024
N_HEADS = 8
HEAD_DIM = 64
KV_RANK = 256
IDX_HEADS = 8
IDX_DIM = 64
TOPK_MAX = 256
Q_BLOCK = 128
POOL_WINDOWS = (2, 4, 8, 16)
POOL_WIDTH = 512
POOL_GROUP = 128
SSD_HEADS = 16
SSD_HEAD_DIM = 64
D_INNER = 1024
N_GROUPS = 2
D_STATE = 128
CONV_WIDTH = 4
CONV_DIM = D_INNER + 2 * N_GROUPS * D_STATE
CHUNK = 128
D_FF = 2816
PLE_DIM = 256
EPS = 1e-6

C_XBC = 0
C_Q = 1536
C_Z = 2048
C_GATE = 3072
C_QIDX = 6144
C_POOL = 6656
C_CKV = 7168
C_SMALL = 7424
PROJ_COLS = 7680
SM_KIDX = 0
SM_WIDX = 64
SM_DT = 72

INT_MIN = -2147483648
LOG2E = 1.4426950408889634
RC = 32
ONES_ROWS = 16
NEG_BIG = -1e30
VMEM_LIMIT = 56 * 1024 * 1024


def _cparams(sem):
    return pltpu.CompilerParams(dimension_semantics=sem, vmem_limit_bytes=VMEM_LIMIT)


def _rms(x, g):
    return x * lax.rsqrt(jnp.mean(x * x, axis=-1, keepdims=True) + EPS) * g


def _dot(a, b):
    return jnp.dot(a, b, preferred_element_type=F32)


def _sigmoid(x):
    return 1.0 / (1.0 + jnp.exp(-x))


def _silu(x):
    return x * _sigmoid(x)


def _ffn_kernel(h_ref, g_ref, wg_ref, wu_ref, wd_ref, o_ref, xn_ref, acc_ref):
    k = pl.program_id(1)

    @pl.when(k == 0)
    def _():
        xn_ref[...] = _rms(h_ref[...], g_ref[...]).astype(xn_ref.dtype)
        acc_ref[...] = jnp.zeros_like(acc_ref)

    xn = xn_ref[...]
    g = _dot(xn, wg_ref[...])
    u = _dot(xn, wu_ref[...])
    a = _silu(g) * u
    acc_ref[...] += _dot(a.astype(_MXU), wd_ref[...])

    @pl.when(k == pl.num_programs(1) - 1)
    def _():
        o_ref[...] = h_ref[...] + 0.5 * acc_ref[...]


def _ffn(h, g, wg, wu, wd, *, tm, tf):
    t, d = h.shape
    ff = wg.shape[1]
    return pl.pallas_call(
        _ffn_kernel,
        grid=(t // tm, ff // tf),
        in_specs=[
            pl.BlockSpec((tm, d), lambda i, k: (i, 0)),
            pl.BlockSpec((1, d), lambda i, k: (0, 0)),
            pl.BlockSpec((d, tf), lambda i, k: (0, k)),
            pl.BlockSpec((d, tf), lambda i, k: (0, k)),
            pl.BlockSpec((tf, d), lambda i, k: (k, 0)),
        ],
        out_specs=pl.BlockSpec((tm, d), lambda i, k: (i, 0)),
        out_shape=jax.ShapeDtypeStruct((t, d), F32),
        scratch_shapes=[pltpu.VMEM((tm, d), _MXU), pltpu.VMEM((tm, d), F32)],
        compiler_params=_cparams(("parallel", "arbitrary")),
        name="ffn",
    )(h, g, wg, wu, wd)


def _inproj_kernel(h_ref, g_ref, w_ref, o_ref, xn_ref):
    @pl.when(pl.program_id(1) == 0)
    def _():
        xn_ref[...] = _rms(h_ref[...], g_ref[...]).astype(xn_ref.dtype)

    o_ref[...] = _dot(xn_ref[...], w_ref[...])


def _inproj(h, g, w, *, tm, tn):
    t, d = h.shape
    n = w.shape[1]
    return pl.pallas_call(
        _inproj_kernel,
        grid=(t // tm, n // tn),
        in_specs=[
            pl.BlockSpec((tm, d), lambda i, j: (i, 0)),
            pl.BlockSpec((1, d), lambda i, j: (0, 0)),
            pl.BlockSpec((d, tn), lambda i, j: (0, j)),
        ],
        out_specs=pl.BlockSpec((tm, tn), lambda i, j: (i, j)),
        out_shape=jax.ShapeDtypeStruct((t, n), F32),
        scratch_shapes=[pltpu.VMEM((tm, d), _MXU)],
        compiler_params=_cparams(("parallel", "arbitrary")),
        name="inproj",
    )(h, g, w)


def _kvprep_kernel(ckv_ref, sm_ref, gkv_ref, gk_ref, ckv_o, ckvT_o, kidx_o):
    c = _rms(ckv_ref[...], gkv_ref[...])
    ckv_o[0] = c.astype(ckv_o.dtype)
    ones = jnp.ones((ONES_ROWS, c.shape[0]), F32)
    ckvT_o[0, 0] = jnp.concatenate([c.T, ones], axis=0).astype(ckvT_o.dtype)
    k = sm_ref[...][:, SM_KIDX:SM_KIDX + IDX_DIM]
    kidx_o[0] = _rms(k, gk_ref[...]).astype(kidx_o.dtype)


def _kvprep(proj, gkv, gk, *, b, s, kb):
    nk = s // kb
    return pl.pallas_call(
        _kvprep_kernel,
        grid=(b, nk),
        in_specs=[
            pl.BlockSpec((kb, KV_RANK), lambda bi, i: (bi * nk + i, C_CKV // KV_RANK)),
            pl.BlockSpec((kb, 128), lambda bi, i: (bi * nk + i, C_SMALL // 128)),
            pl.BlockSpec((1, KV_RANK), lambda bi, i: (0, 0)),
            pl.BlockSpec((1, IDX_DIM), lambda bi, i: (0, 0)),
        ],
        out_specs=[
            pl.BlockSpec((1, kb, KV_RANK), lambda bi, i: (bi, i, 0)),
            pl.BlockSpec((1, 1, KV_RANK + ONES_ROWS, kb), lambda bi, i: (bi, i, 0, 0)),
            pl.BlockSpec((1, kb, IDX_DIM), lambda bi, i: (bi, i, 0)),
        ],
        out_shape=[
            jax.ShapeDtypeStruct((b, s, KV_RANK), _MXU),
            jax.ShapeDtypeStruct((b, nk, KV_RANK + ONES_ROWS, kb), _MXU),
            jax.ShapeDtypeStruct((b, s, IDX_DIM), _MXU),
        ],
        compiler_params=_cparams(("parallel", "parallel")),
        name="kvprep",
    )(proj, proj, gkv, gk)


def _attn_kernel(q_ref, qi_ref, sm_ref, ckv_ref, ckvT_ref, kidx_ref, wuk_ref, wuvT_ref, o_ref,
                 key_ref, hi_ref, lo_ref, qlat_ref, s_ref, p_ref, acc_ref, m_ref, alpha_ref,
                 *, k_top, kb):
    j = pl.program_id(1)
    nblk = (j * Q_BLOCK + Q_BLOCK + kb - 1) // kb
    nh = N_HEADS
    qb = Q_BLOCK

    qT = q_ref[...].T
    qiT = qi_ref[...].T
    smT = sm_ref[...].T
    w_row = jnp.concatenate([smT[SM_WIDX + h:SM_WIDX + h + 1, :] for h in range(nh)], axis=1)
    qiT_stack = jnp.concatenate(
        [qiT[h * IDX_DIM:(h + 1) * IDX_DIM, :] for h in range(nh)], axis=1).astype(_MXU)
    scale = HEAD_DIM ** -0.5
    qlatT = jnp.concatenate(
        [_dot(wuk_ref[h], qT[h * HEAD_DIM:(h + 1) * HEAD_DIM, :].astype(_MXU)) for h in range(nh)],
        axis=1)
    qlat_ref[...] = (qlatT * (scale * LOG2E)).astype(_MXU)

    qpos = j * qb + lax.broadcasted_iota(jnp.int32, (kb, qb), 1)

    def idx_body(bi, carry):
        off = pl.multiple_of(bi * kb, kb)
        kblk = kidx_ref[0, pl.ds(off, kb), :]
        r = jnp.maximum(_dot(kblk, qiT_stack), 0.0) * w_row
        sc = r[:, 0:qb]
        for h in range(1, nh):
            sc = sc + r[:, h * qb:(h + 1) * qb]
        bits = pltpu.bitcast(sc, jnp.int32)
        key = jnp.where(bits < 0, bits ^ jnp.int32(0x7FFFFFFF), bits)
        kpos = off + lax.broadcasted_iota(jnp.int32, (kb, qb), 0)
        key = jnp.where(kpos <= qpos, key, jnp.int32(INT_MIN))
        key_ref[pl.ds(off, kb), :] = key
        hi_ref[pl.ds(off, kb), :] = (key >> 16).astype(jnp.int16)
        lo_ref[pl.ds(off, kb), :] = ((key & 0xFFFF) - 32768).astype(jnp.int16)
        return carry

    lax.fori_loop(0, nblk, idx_body, 0)

    one_b = jnp.ones((16, qb), jnp.bfloat16)
    zero_b = jnp.zeros((16, qb), jnp.bfloat16)
    n_acc = 4

    def count_ge(ref, cand):
        cb = jnp.broadcast_to(cand, (16, qb)).astype(jnp.int16)

        def cnt_body(bi, tot):
            off = pl.multiple_of(bi * kb, kb)
            accs = [zero_b] * n_acc
            blk = ref[pl.ds(off, kb), :]
            for r in range(kb // 16):
                v = blk[r * 16:(r + 1) * 16, :]
                accs[r % n_acc] = accs[r % n_acc] + jnp.where(v >= cb, one_b, zero_b)
            return tot + ((accs[0] + accs[1]) + (accs[2] + accs[3])).astype(F32)

        tot = lax.fori_loop(0, nblk, cnt_body, jnp.zeros((16, qb), F32))
        return jnp.sum(tot, axis=0, keepdims=True)

    def bit_search(ref, need):
        def bit_body(i, prefix):
            cand_u = prefix | jnp.left_shift(jnp.int32(1), 15 - i)
            cnt = count_ge(ref, cand_u - 32768)
            return jnp.where(cnt >= need, cand_u, prefix)

        return lax.fori_loop(0, 16, bit_body, jnp.zeros((1, qb), jnp.int32))

    hi_s = bit_search(hi_ref, float(k_top)) - 32768
    n_above = jnp.where(hi_s >= 32767, 0.0, count_ge(hi_ref, jnp.minimum(hi_s + 1, 32767)))

    def tie_body(bi, carry):
        off = pl.multiple_of(bi * kb, kb)
        sl = pl.ds(off, kb)
        hib_k = jnp.broadcast_to(hi_s, (kb, qb)).astype(jnp.int16)
        lo_ref[sl, :] = jnp.where(hi_ref[sl, :] == hib_k, lo_ref[sl, :], jnp.int16(-32768))
        return carry

    lax.fori_loop(0, nblk, tie_body, 0)
    lo_u = bit_search(lo_ref, float(k_top) - n_above)
    thr = jnp.maximum(hi_s * 65536 + lo_u, jnp.int32(INT_MIN + 1))
    thr_b = jnp.broadcast_to(thr, (RC, qb))

    m_ref[...] = jnp.full(m_ref.shape, NEG_BIG, F32)
    alpha_ref[...] = jnp.ones_like(alpha_ref)
    acc_ref[...] = jnp.zeros_like(acc_ref)
    p_ref[1] = jnp.zeros(p_ref.shape[1:], p_ref.dtype)

    def qk(bi, slot):
        off = pl.multiple_of(jnp.minimum(bi, nblk - 1) * kb, kb)
        s_ref[slot] = _dot(ckv_ref[0, pl.ds(off, kb), :], qlat_ref[...])

    def pv(bi, slot):
        acc_ref[...] = alpha_ref[...] * acc_ref[...] + _dot(ckvT_ref[0, jnp.maximum(bi, 0)], p_ref[slot])

    def softmax(bi, slot):
        off = pl.multiple_of(bi * kb, kb)
        nch = kb // RC

        def masks(c):
            return key_ref[pl.ds(off + c * RC, RC), :] >= thr_b

        mx = [jnp.full((8, qb), NEG_BIG, F32)] * nh
        for c in range(nch):
            mk = masks(c)
            for h in range(nh):
                sc = jnp.where(mk, s_ref[slot, c * RC:(c + 1) * RC, h * qb:(h + 1) * qb], NEG_BIG)
                mx[h] = jnp.maximum(mx[h], jnp.max(sc.reshape(RC // 8, 8, qb), axis=0))
        m_new = []
        for h in range(nh):
            hs = slice(h * qb, (h + 1) * qb)
            m_old = m_ref[:, hs]
            mn = jnp.maximum(m_old, jnp.max(mx[h], axis=0, keepdims=True))
            alpha_ref[:, hs] = jnp.exp2(m_old - mn)
            m_ref[:, hs] = mn
            m_new.append(jnp.broadcast_to(mn, (RC, qb)))
        for c in range(nch):
            mk = masks(c)
            for h in range(nh):
                sc = s_ref[slot, c * RC:(c + 1) * RC, h * qb:(h + 1) * qb]
                pr = jnp.where(mk, jnp.exp2(sc - m_new[h]), 0.0)
                p_ref[slot, c * RC:(c + 1) * RC, h * qb:(h + 1) * qb] = pr.astype(p_ref.dtype)

    def step(bi, cur):
        qk(bi + 1, 1 - cur)
        pv(bi - 1, 1 - cur)
        softmax(bi, cur)

    qk(0, 0)

    def pair_body(i2, carry):
        step(2 * i2, 0)
        step(2 * i2 + 1, 1)
        return carry

    lax.fori_loop(0, nblk // 2, pair_body, 0)
    odd = nblk % 2 == 1

    @pl.when(odd)
    def _():
        step(nblk - 1, 0)
        pv(nblk - 1, 0)

    @pl.when(jnp.logical_not(odd))
    def _():
        pv(nblk - 1, 1)

    o = (acc_ref[:KV_RANK, :] / acc_ref[KV_RANK:KV_RANK + 1, :]).astype(_MXU)
    yT = jnp.concatenate([_dot(wuvT_ref[h], o[:, h * qb:(h + 1) * qb]) for h in range(nh)], axis=0)
    o_ref[...] = yT.T


def _attention(proj, ckv, ckvT, kidx, wuk, wuvT, *, b, s, kb):
    nq = s // Q_BLOCK
    k_top = min(TOPK_MAX, s // 4)
    kern = functools.partial(_attn_kernel, k_top=k_top, kb=kb)
    hq = N_HEADS * Q_BLOCK
    return pl.pallas_call(
        kern,
        grid=(b, nq),
        in_specs=[
            pl.BlockSpec((Q_BLOCK, 512), lambda bi, j: (bi * nq + j, C_Q // 512)),
            pl.BlockSpec((Q_BLOCK, 512), lambda bi, j: (bi * nq + j, C_QIDX // 512)),
            pl.BlockSpec((Q_BLOCK, 128), lambda bi, j: (bi * nq + j, C_SMALL // 128)),
            pl.BlockSpec((1, s, KV_RANK), lambda bi, j: (bi, 0, 0)),
            pl.BlockSpec((1, s // kb, KV_RANK + ONES_ROWS, kb), lambda bi, j: (bi, 0, 0, 0)),
            pl.BlockSpec((1, s, IDX_DIM), lambda bi, j: (bi, 0, 0)),
            pl.BlockSpec((N_HEADS, KV_RANK, HEAD_DIM), lambda bi, j: (0, 0, 0)),
            pl.BlockSpec((N_HEADS, HEAD_DIM, KV_RANK), lambda bi, j: (0, 0, 0)),
        ],
        out_specs=pl.BlockSpec((Q_BLOCK, N_HEADS * HEAD_DIM), lambda bi, j: (bi * nq + j, 0)),
        out_shape=jax.ShapeDtypeStruct((b * s, N_HEADS * HEAD_DIM), F32),
        scratch_shapes=[
            pltpu.VMEM((s, Q_BLOCK), jnp.int32),
            pltpu.VMEM((s, Q_BLOCK), jnp.int16),
            pltpu.VMEM((s, Q_BLOCK), jnp.int16),
            pltpu.VMEM((KV_RANK, hq), _MXU),
            pltpu.VMEM((2, kb, hq), F32),
            pltpu.VMEM((2, kb, hq), _MXU),
            pltpu.VMEM((KV_RANK + ONES_ROWS, hq), F32),
            pltpu.VMEM((1, hq), F32),
            pltpu.VMEM((1, hq), F32),
        ],
        compiler_params=_cparams(("parallel", "arbitrary")),
        name="dsa_attn",
    )(proj, proj, proj, ckv, ckvT, kidx, wuk, wuvT)


def _pool_kernel(x_ref, halo_ref, pw_ref, ps_ref, o_ref, *, tm):
    i = pl.program_id(1)
    x = x_ref[...]
    halo = jnp.where(i > 0, halo_ref[...], 0.0)
    xe = jnp.concatenate([halo, x], axis=0)
    pos = i * tm + lax.broadcasted_iota(jnp.int32, (tm, POOL_GROUP), 0)
    outs = []
    for g, w in enumerate(POOL_WINDOWS):
        sl = slice(g * POOL_GROUP, (g + 1) * POOL_GROUP)
        cur = xe[:, sl]
        span = 1
        while span < w:
            n = cur.shape[0]
            cur = cur[span:, :] + cur[:n - span, :]
            span *= 2
        win = cur[cur.shape[0] - tm:, :]
        count = jnp.minimum(pos + 1, w).astype(F32)
        pooled = win / count - x[:, sl]
        outs.append(_dot(pooled.astype(_MXU), pw_ref[g]))
    o_ref[...] = jnp.concatenate(outs, axis=1) * ps_ref[...]


def _pool(proj, pw, ps, *, b, s, tm):
    nt = s // tm
    kern = functools.partial(_pool_kernel, tm=tm)
    return pl.pallas_call(
        kern,
        grid=(b, nt),
        in_specs=[
            pl.BlockSpec((tm, POOL_WIDTH), lambda bi, i: (bi * nt + i, C_POOL // POOL_WIDTH)),
            pl.BlockSpec((16, POOL_WIDTH),
                         lambda bi, i: (jnp.maximum((bi * s + i * tm) // 16 - 1, 0), C_POOL // POOL_WIDTH)),
            pl.BlockSpec((len(POOL_WINDOWS), POOL_GROUP, POOL_GROUP), lambda bi, i: (0, 0, 0)),
            pl.BlockSpec((1, POOL_WIDTH), lambda bi, i: (0, 0)),
        ],
        out_specs=pl.BlockSpec((tm, POOL_WIDTH), lambda bi, i: (bi * nt + i, 0)),
        out_shape=jax.ShapeDtypeStruct((b * s, POOL_WIDTH), F32),
        compiler_params=_cparams(("parallel", "parallel")),
        name="pool",
    )(proj, proj, pw, ps)


def _ssd_kernel(xbc_ref, z_ref, sm_ref, cw_ref, cb_ref, dtb_row_ref, dtb_col_ref, a_x_ref, a_col_ref,
                dsk_x_ref, expand_ref, ng_ref, o_ref, state_ref, tail_ref):
    c = pl.program_id(1)
    L = CHUNK
    P = SSD_HEAD_DIM

    @pl.when(c == 0)
    def _():
        state_ref[...] = jnp.zeros_like(state_ref)
        tail_ref[...] = jnp.zeros_like(tail_ref)

    x_raw = xbc_ref[...]
    xe = jnp.concatenate([tail_ref[...], x_raw], axis=0)
    cw = cw_ref[...]
    conv = cb_ref[...] + cw[3:4, :] * x_raw
    for k in range(CONV_WIDTH - 1):
        sh = CONV_WIDTH - 1 - k
        conv = conv + cw[k:k + 1, :] * xe[8 - sh:8 - sh + L, :]
    tail_ref[...] = x_raw[L - 8:, :]
    xc = _silu(conv)
    xs = xc[:, :D_INNER]
    bm = xc[:, D_INNER:D_INNER + N_GROUPS * D_STATE]
    cm = xc[:, D_INNER + N_GROUPS * D_STATE:]

    sm = sm_ref[...]
    smT = sm.T
    dt = jax.nn.softplus(sm[:, SM_DT:SM_DT + SSD_HEADS] + dtb_row_ref[...])
    dtT = jax.nn.softplus(smT[SM_DT:SM_DT + SSD_HEADS, :] + dtb_col_ref[...])
    hi = lax.Precision.HIGHEST
    dtx = jnp.dot(dt, expand_ref[...], precision=hi, preferred_element_type=F32)
    dax = dtx * a_x_ref[...]
    daT = dtT * a_col_ref[...]
    row = lax.broadcasted_iota(jnp.int32, (L, L), 0)
    col = lax.broadcasted_iota(jnp.int32, (L, L), 1)
    lower = row >= col
    tril = jnp.where(lower, 1.0, 0.0).astype(F32)
    triu = jnp.where(row <= col, 1.0, 0.0).astype(F32)
    acs_x = jnp.dot(tril, dax, precision=hi, preferred_element_type=F32)
    acsT = jnp.dot(daT, triu, precision=hi, preferred_element_type=F32)
    a_last = acs_x[L - 1:L, :]
    ea_x = jnp.exp(acs_x)
    decay_x = jnp.exp(a_last - acs_x)
    cd_x = jnp.exp(a_last)

    X = xs * dtx
    Xb = X.astype(_MXU)
    Xd = (X * decay_x).astype(_MXU)
    state = state_ref[...]
    ys = []
    hpg = SSD_HEADS // N_GROUPS
    for g in range(N_GROUPS):
        bg = bm[:, g * D_STATE:(g + 1) * D_STATE]
        cg = cm[:, g * D_STATE:(g + 1) * D_STATE].astype(_MXU)
        bgT = bg.T.astype(_MXU)
        cb = _dot(cg, bgT)
        gs = slice(g * hpg * P, (g + 1) * hpg * P)
        y_off = _dot(cg, state[:, gs].astype(_MXU)) * ea_x[:, gs]
        state_ref[:, gs] = state[:, gs] * cd_x[:, gs] + _dot(bgT, Xd[:, gs])
        for e in range(hpg):
            h = g * hpg + e
            hs = slice(h * P, (h + 1) * P)
            diff = acs_x[:, h * P:h * P + 1] - acsT[h:h + 1, :]
            lmat = jnp.exp(jnp.where(lower, diff, -jnp.inf))
            m = (cb * lmat).astype(_MXU)
            ys.append(_dot(m, Xb[:, hs]) + y_off[:, e * P:(e + 1) * P])
    y = jnp.concatenate(ys, axis=1) + xs * dsk_x_ref[...]
    y = y * _silu(z_ref[...])
    o_ref[...] = _rms(y, ng_ref[...])


def _ssd(proj, cw, cb, dtb, a_log, dsk, ng, *, b, s):
    nc = s // CHUNK
    rep = lambda v: jnp.repeat(v, SSD_HEAD_DIM)[None, :]
    a = -jnp.exp(a_log)
    expand = jnp.repeat(jnp.eye(SSD_HEADS, dtype=F32), SSD_HEAD_DIM, axis=1)
    const2 = lambda bi, c: (0, 0)
    return pl.pallas_call(
        _ssd_kernel,
        grid=(b, nc),
        in_specs=[
            pl.BlockSpec((CHUNK, CONV_DIM), lambda bi, c: (bi * nc + c, C_XBC // CONV_DIM)),
            pl.BlockSpec((CHUNK, D_INNER), lambda bi, c: (bi * nc + c, C_Z // D_INNER)),
            pl.BlockSpec((CHUNK, 128), lambda bi, c: (bi * nc + c, C_SMALL // 128)),
            pl.BlockSpec((CONV_WIDTH, CONV_DIM), const2),
            pl.BlockSpec((1, CONV_DIM), const2),
            pl.BlockSpec((1, SSD_HEADS), const2),
            pl.BlockSpec((SSD_HEADS, 1), const2),
            pl.BlockSpec((1, D_INNER), const2),
            pl.BlockSpec((SSD_HEADS, 1), const2),
            pl.BlockSpec((1, D_INNER), const2),
            pl.BlockSpec((SSD_HEADS, D_INNER), const2),
            pl.BlockSpec((1, D_INNER), const2),
        ],
        out_specs=pl.BlockSpec((CHUNK, D_INNER), lambda bi, c: (bi * nc + c, 0)),
        out_shape=jax.ShapeDtypeStruct((b * s, D_INNER), F32),
        scratch_shapes=[pltpu.VMEM((D_STATE, D_INNER), F32), pltpu.VMEM((8, CONV_DIM), F32)],
        compiler_params=_cparams(("parallel", "arbitrary")),
        name="ssd",
    )(proj, proj, proj, cw, cb[None, :], dtb[None, :], dtb[:, None], rep(a), a[:, None],
      rep(dsk), expand, ng[None, :])


def _merge_kernel(ya_ref, yp_ref, ys_ref, ga_ref, gp_ref, gs_ref, h_ref, wa_ref, wp_ref, ws_ref,
                  wo_ref, o_ref):
    m = _sigmoid(ga_ref[...]) * _dot(ya_ref[...].astype(_MXU), wa_ref[...])
    m = m + _sigmoid(gp_ref[...]) * _dot(yp_ref[...].astype(_MXU), wp_ref[...])
    m = m + _sigmoid(gs_ref[...]) * _dot(ys_ref[...].astype(_MXU), ws_ref[...])
    o_ref[...] = h_ref[...] + _dot(m.astype(_MXU), wo_ref[...])


def _merge(ya, yp, ys, proj, h, wa, wp, ws, wo, *, tm):
    t, d = h.shape
    row = lambda i: (i, 0)
    const = lambda i: (0, 0)
    gate = lambda k: (lambda i: (i, C_GATE // d + k))
    return pl.pallas_call(
        _merge_kernel,
        grid=(t // tm,),
        in_specs=[
            pl.BlockSpec((tm, ya.shape[1]), row),
            pl.BlockSpec((tm, yp.shape[1]), row),
            pl.BlockSpec((tm, ys.shape[1]), row),
            pl.BlockSpec((tm, d), gate(0)),
            pl.BlockSpec((tm, d), gate(1)),
            pl.BlockSpec((tm, d), gate(2)),
            pl.BlockSpec((tm, d), row),
            pl.BlockSpec(wa.shape, const),
            pl.BlockSpec(wp.shape, const),
            pl.BlockSpec(ws.shape, const),
            pl.BlockSpec(wo.shape, const),
        ],
        out_specs=pl.BlockSpec((tm, d), row),
        out_shape=jax.ShapeDtypeStruct((t, d), F32),
        compiler_params=_cparams(("parallel",)),
        name="merge",
    )(ya, yp, ys, proj, proj, proj, h, wa, wp, ws, wo)


def _ple_kernel(h_ref, p_ref, g_ref, wg_ref, wp_ref, fg_ref, o_ref, *, final):
    h = h_ref[...]
    gate = _sigmoid(_dot(_rms(h, g_ref[...]).astype(_MXU), wg_ref[...]))
    h = h + gate * _dot(p_ref[...].astype(_MXU), wp_ref[...])
    if final:
        h = _rms(h, fg_ref[...])
    o_ref[...] = h


def _ple(h, p, g, wg, wp, fg, *, tm, final):
    t, d = h.shape
    row = lambda i: (i, 0)
    const = lambda i: (0, 0)
    return pl.pallas_call(
        functools.partial(_ple_kernel, final=final),
        grid=(t // tm,),
        in_specs=[
            pl.BlockSpec((tm, d), row),
            pl.BlockSpec((tm, p.shape[1]), row),
            pl.BlockSpec((1, d), const),
            pl.BlockSpec(wg.shape, const),
            pl.BlockSpec(wp.shape, const),
            pl.BlockSpec((1, d), const),
        ],
        out_specs=pl.BlockSpec((tm, d), row),
        out_shape=jax.ShapeDtypeStruct((t, d), F32),
        compiler_params=_cparams(("parallel",)),
        name="ple",
    )(h, p, g, wg, wp, fg)


def _reorder_w_in(w):
    o_q, o_ckv, o_qidx, o_widx, o_kidx, o_pool, o_z, o_xbc, o_dt, o_gate = (
        0, 512, 768, 1280, 1288, 1352, 1864, 2888, 4424, 4440)
    d = w.shape[0]
    small = jnp.concatenate([
        w[:, o_kidx:o_kidx + IDX_DIM], w[:, o_widx:o_widx + IDX_HEADS], w[:, o_dt:o_dt + SSD_HEADS],
        jnp.zeros((d, 128 - IDX_DIM - IDX_HEADS - SSD_HEADS), w.dtype)], axis=1)
    cols = [
        w[:, o_xbc:o_xbc + CONV_DIM], w[:, o_q:o_q + 512], w[:, o_z:o_z + D_INNER],
        w[:, o_gate:o_gate + 3 * D_MODEL], w[:, o_qidx:o_qidx + 512], w[:, o_pool:o_pool + POOL_WIDTH],
        w[:, o_ckv:o_ckv + KV_RANK], small,
    ]
    out = jnp.concatenate(cols, axis=1)
    return jnp.pad(out, ((0, 0), (0, PROJ_COLS - out.shape[1])))


def kernel(x, p, ffn1_norm, ffn1_w_gate, ffn1_w_up, ffn1_w_down, mix_norm, w_in, kv_norm, idx_k_norm, w_uk, w_uv, pool_w, pool_scale, conv_w, conv_b, dt_bias, a_log, d_skip, ssd_norm, w_br_attn, w_br_pool, w_br_ssd, w_out, ffn2_norm, ffn2_w_gate, ffn2_w_up, ffn2_w_down, ple_norm, ple_w_gate, ple_w_proj, final_norm):
    b, s, d = x.shape
    depth = w_in.shape[0]
    t = b * s
    tm = min(1024, t)
    tm_small = min(512, t)
    kb = min(512, s)
    tf = 1408
    mx = lambda a: a.astype(_MXU)

    h = x.reshape(t, d)
    for i in range(depth):
        h = _ffn(h, ffn1_norm[i][None], mx(ffn1_w_gate[i]), mx(ffn1_w_up[i]), mx(ffn1_w_down[i]),
                 tm=tm_small, tf=tf)
        proj = _inproj(h, mix_norm[i][None], mx(_reorder_w_in(w_in[i])), tm=tm, tn=512)
        ckv, ckvT, kidx = _kvprep(proj, kv_norm[i][None], idx_k_norm[i][None], b=b, s=s, kb=kb)
        wuk = mx(jnp.transpose(w_uk[i], (1, 0, 2)))
        wuvT = mx(jnp.transpose(w_uv[i], (1, 2, 0)))
        y_attn = _attention(proj, ckv, ckvT, kidx, wuk, wuvT, b=b, s=s, kb=kb)
        y_pool = _pool(proj, mx(pool_w[i]), pool_scale[i][None], b=b, s=s, tm=tm_small)
        y_ssd = _ssd(proj, conv_w[i], conv_b[i], dt_bias[i], a_log[i], d_skip[i], ssd_norm[i], b=b, s=s)
        h = _merge(y_attn, y_pool, y_ssd, proj, h, mx(w_br_attn[i]), mx(w_br_pool[i]),
                   mx(w_br_ssd[i]), mx(w_out[i]), tm=tm_small)
        h = _ffn(h, ffn2_norm[i][None], mx(ffn2_w_gate[i]), mx(ffn2_w_up[i]), mx(ffn2_w_down[i]),
                 tm=tm_small, tf=tf)
        h = _ple(h, p[i].reshape(t, -1), ple_norm[i][None], mx(ple_w_gate[i]), mx(ple_w_proj[i]),
                 final_norm[None], tm=tm_small, final=(i == depth - 1))
    return h.reshape(b, s, d)
```

```python
import functools

import jax
import jax.numpy as jnp
from jax import lax
from jax.experimental import pallas as pl
from jax.experimental.pallas import tpu as pltpu

_MXU = jnp.bfloat16
F32 = jnp.float32

D_MODEL = 1024
N_HEADS = 8
HEAD_DIM = 64
KV_RANK = 256
IDX_HEADS = 8
IDX_DIM = 64
TOPK_MAX = 256
Q_BLOCK = 128
POOL_WINDOWS = (2, 4, 8, 16)
POOL_WIDTH = 512
POOL_GROUP = 128
SSD_HEADS = 16
SSD_HEAD_DIM = 64
D_INNER = 1024
N_GROUPS = 2
D_STATE = 128
CONV_WIDTH = 4
CONV_DIM = D_INNER + 2 * N_GROUPS * D_STATE
CHUNK = 128
D_FF = 2816
PLE_DIM = 256
EPS = 1e-6

C_XBC = 0
C_Q = 1536
C_Z = 2048
C_GATE = 3072
C_QIDX = 6144
C_POOL = 6656
C_CKV = 7168
C_SMALL = 7424
PROJ_COLS = 7680
SM_KIDX = 0
SM_WIDX = 64
SM_DT = 72

INT_MIN = -2147483648
LOG2E = 1.4426950408889634
RC = 32
ONES_ROWS = 16
NEG_BIG = -1e30
VMEM_LIMIT = 56 * 1024 * 1024


def _cparams(sem):
    return pltpu.CompilerParams(dimension_semantics=sem, vmem_limit_bytes=VMEM_LIMIT)


def _rms(x, g):
    return x * lax.rsqrt(jnp.mean(x * x, axis=-1, keepdims=True) + EPS) * g


def _dot(a, b):
    return jnp.dot(a, b, preferred_element_type=F32)


def _sigmoid(x):
    return 1.0 / (1.0 + jnp.exp(-x))


def _silu(x):
    return x * _sigmoid(x)


def _ffn_kernel(h_ref, g_ref, wg_ref, wu_ref, wd_ref, o_ref, xn_ref, acc_ref):
    k = pl.program_id(1)

    @pl.when(k == 0)
    def _():
        xn_ref[...] = _rms(h_ref[...], g_ref[...]).astype(xn_ref.dtype)
        acc_ref[...] = jnp.zeros_like(acc_ref)

    xn = xn_ref[...]
    g = _dot(xn, wg_ref[...])
    u = _dot(xn, wu_ref[...])
    a = _silu(g) * u
    acc_ref[...] += _dot(a.astype(_MXU), wd_ref[...])

    @pl.when(k == pl.num_programs(1) - 1)
    def _():
        o_ref[...] = h_ref[...] + 0.5 * acc_ref[...]


def _ffn(h, g, wg, wu, wd, *, tm, tf):
    t, d = h.shape
    ff = wg.shape[1]
    return pl.pallas_call(
        _ffn_kernel,
        grid=(t // tm, ff // tf),
        in_specs=[
            pl.BlockSpec((tm, d), lambda i, k: (i, 0)),
            pl.BlockSpec((1, d), lambda i, k: (0, 0)),
            pl.BlockSpec((d, tf), lambda i, k: (0, k)),
            pl.BlockSpec((d, tf), lambda i, k: (0, k)),
            pl.BlockSpec((tf, d), lambda i, k: (k, 0)),
        ],
        out_specs=pl.BlockSpec((tm, d), lambda i, k: (i, 0)),
        out_shape=jax.ShapeDtypeStruct((t, d), F32),
        scratch_shapes=[pltpu.VMEM((tm, d), _MXU), pltpu.VMEM((tm, d), F32)],
        compiler_params=_cparams(("parallel", "arbitrary")),
        name="ffn",
    )(h, g, wg, wu, wd)


def _inproj_kernel(h_ref, g_ref, w_ref, o_ref, xn_ref):
    @pl.when(pl.program_id(1) == 0)
    def _():
        xn_ref[...] = _rms(h_ref[...], g_ref[...]).astype(xn_ref.dtype)

    o_ref[...] = _dot(xn_ref[...], w_ref[...])


def _inproj(h, g, w, *, tm, tn):
    t, d = h.shape
    n = w.shape[1]
    return pl.pallas_call(
        _inproj_kernel,
        grid=(t // tm, n // tn),
        in_specs=[
            pl.BlockSpec((tm, d), lambda i, j: (i, 0)),
            pl.BlockSpec((1, d), lambda i, j: (0, 0)),
            pl.BlockSpec((d, tn), lambda i, j: (0, j)),
        ],
        out_specs=pl.BlockSpec((tm, tn), lambda i, j: (i, j)),
        out_shape=jax.ShapeDtypeStruct((t, n), F32),
        scratch_shapes=[pltpu.VMEM((tm, d), _MXU)],
        compiler_params=_cparams(("parallel", "arbitrary")),
        name="inproj",
    )(h, g, w)


def _kvprep_kernel(ckv_ref, sm_ref, gkv_ref, gk_ref, ckv_o, ckvT_o, kidx_o):
    c = _rms(ckv_ref[...], gkv_ref[...])
    ckv_o[0] = c.astype(ckv_o.dtype)
    ones = jnp.ones((ONES_ROWS, c.shape[0]), F32)
    ckvT_o[0, 0] = jnp.concatenate([c.T, ones], axis=0).astype(ckvT_o.dtype)
    k = sm_ref[...][:, SM_KIDX:SM_KIDX + IDX_DIM]
    kidx_o[0] = _rms(k, gk_ref[...]).astype(kidx_o.dtype)


def _kvprep(proj, gkv, gk, *, b, s, kb):
    nk = s // kb
    return pl.pallas_call(
        _kvprep_kernel,
        grid=(b, nk),
        in_specs=[
            pl.BlockSpec((kb, KV_RANK), lambda bi, i: (bi * nk + i, C_CKV // KV_RANK)),
            pl.BlockSpec((kb, 128), lambda bi, i: (bi * nk + i, C_SMALL // 128)),
            pl.BlockSpec((1, KV_RANK), lambda bi, i: (0, 0)),
            pl.BlockSpec((1, IDX_DIM), lambda bi, i: (0, 0)),
        ],
        out_specs=[
            pl.BlockSpec((1, kb, KV_RANK), lambda bi, i: (bi, i, 0)),
            pl.BlockSpec((1, 1, KV_RANK + ONES_ROWS, kb), lambda bi, i: (bi, i, 0, 0)),
            pl.BlockSpec((1, kb, IDX_DIM), lambda bi, i: (bi, i, 0)),
        ],
        out_shape=[
            jax.ShapeDtypeStruct((b, s, KV_RANK), _MXU),
            jax.ShapeDtypeStruct((b, nk, KV_RANK + ONES_ROWS, kb), _MXU),
            jax.ShapeDtypeStruct((b, s, IDX_DIM), _MXU),
        ],
        compiler_params=_cparams(("parallel", "parallel")),
        name="kvprep",
    )(proj, proj, gkv, gk)


def _bit_transpose32(words):
    words = list(words)
    j, m = 16, 0x0000FFFF
    while j:
        mi = jnp.int32(m - (1 << 32) if m >= (1 << 31) else m)
        for k in range(32):
            if k & j == 0:
                t = (words[k] ^ lax.shift_right_logical(words[k + j], jnp.int32(j))) & mi
                words[k] = words[k] ^ t
                words[k + j] = words[k + j] ^ lax.shift_left(t, jnp.int32(j))
        j >>= 1
        m = (m ^ (m << j)) & 0xFFFFFFFF
    return words


def _attn_kernel(q_ref, qi_ref, sm_ref, ckv_ref, ckvT_ref, kidx_ref, wuk_ref, wuvT_ref, o_ref,
                 key_ref, planes_ref, qlat_ref, s_ref, p_ref, acc_ref, m_ref, alpha_ref,
                 *, k_top, kb):
    j = pl.program_id(1)
    nblk = (j * Q_BLOCK + Q_BLOCK + kb - 1) // kb
    nh = N_HEADS
    qb = Q_BLOCK

    qT = q_ref[...].T
    qiT = qi_ref[...].T
    smT = sm_ref[...].T
    w_row = jnp.concatenate([smT[SM_WIDX + h:SM_WIDX + h + 1, :] for h in range(nh)], axis=1)
    qiT_stack = jnp.concatenate(
        [qiT[h * IDX_DIM:(h + 1) * IDX_DIM, :] for h in range(nh)], axis=1).astype(_MXU)
    scale = HEAD_DIM ** -0.5
    qlatT = jnp.concatenate(
        [_dot(wuk_ref[h], qT[h * HEAD_DIM:(h + 1) * HEAD_DIM, :].astype(_MXU)) for h in range(nh)],
        axis=1)
    qlat_ref[...] = (qlatT * (scale * LOG2E)).astype(_MXU)

    qpos = j * qb + lax.broadcasted_iota(jnp.int32, (kb, qb), 1)

    def idx_body(bi, carry):
        off = pl.multiple_of(bi * kb, kb)
        kblk = kidx_ref[0, pl.ds(off, kb), :]
        r = jnp.maximum(_dot(kblk, qiT_stack), 0.0) * w_row
        sc = r[:, 0:qb]
        for h in range(1, nh):
            sc = sc + r[:, h * qb:(h + 1) * qb]
        bits = pltpu.bitcast(sc, jnp.int32)
        key = jnp.where(bits < 0, bits ^ jnp.int32(0x7FFFFFFF), bits)
        kpos = off + lax.broadcasted_iota(jnp.int32, (kb, qb), 0)
        key = jnp.where(kpos <= qpos, key, jnp.int32(INT_MIN))
        key_ref[pl.ds(off, kb), :] = key
        u = key ^ jnp.int32(INT_MIN)
        for g in range(kb // 256):
            words = _bit_transpose32([u[g * 256 + 8 * t:g * 256 + 8 * t + 8, :] for t in range(32)])
            row = pl.multiple_of((bi * (kb // 256) + g) * 8, 8)
            for i in range(32):
                planes_ref[i, pl.ds(row, 8), :] = words[i]
        return carry

    lax.fori_loop(0, nblk, idx_body, 0)

    ngrp = planes_ref.shape[1] // 8
    grp = lax.broadcasted_iota(jnp.int32, (ngrp * 8, qb), 0) // 8
    eq0 = jnp.where(grp < nblk * (kb // 256), jnp.int32(-1), jnp.int32(0))

    def bit_body(i, carry):
        eq, n_above, prefix = carry
        x = planes_ref[i]
        ones = eq & x
        cnt = jnp.sum(lax.population_count(ones).reshape(ngrp, 8, qb), axis=0)
        cnt = jnp.sum(cnt, axis=0, keepdims=True)
        take = (n_above + cnt) >= k_top
        eq = jnp.where(take, ones, eq & ~x)
        n_above = jnp.where(take, n_above, n_above + cnt)
        prefix = jnp.where(take, prefix | jnp.left_shift(jnp.int32(1), 31 - i), prefix)
        return eq, n_above, prefix

    zrow = jnp.zeros((1, qb), jnp.int32)
    _, _, prefix = lax.fori_loop(0, 32, bit_body, (eq0, zrow, zrow))
    thr = jnp.maximum(prefix ^ jnp.int32(INT_MIN), jnp.int32(INT_MIN + 1))
    thr_b = jnp.broadcast_to(thr, (RC, qb))

    m_ref[...] = jnp.full(m_ref.shape, NEG_BIG, F32)
    alpha_ref[...] = jnp.ones_like(alpha_ref)
    acc_ref[...] = jnp.zeros_like(acc_ref)
    p_ref[1] = jnp.zeros(p_ref.shape[1:], p_ref.dtype)

    def qk(bi, slot):
        off = pl.multiple_of(jnp.minimum(bi, nblk - 1) * kb, kb)
        s_ref[slot] = _dot(ckv_ref[0, pl.ds(off, kb), :], qlat_ref[...])

    def pv(bi, slot):
        acc_ref[...] = alpha_ref[...] * acc_ref[...] + _dot(ckvT_ref[0, jnp.maximum(bi, 0)], p_ref[slot])

    def softmax(bi, slot):
        off = pl.multiple_of(bi * kb, kb)
        nch = kb // RC

        def masks(c):
            return key_ref[pl.ds(off + c * RC, RC), :] >= thr_b

        mx = [jnp.full((8, qb), NEG_BIG, F32)] * nh
        for c in range(nch):
            mk = masks(c)
            for h in range(nh):
                sc = jnp.where(mk, s_ref[slot, c * RC:(c + 1) * RC, h * qb:(h + 1) * qb], NEG_BIG)
                mx[h] = jnp.maximum(mx[h], jnp.max(sc.reshape(RC // 8, 8, qb), axis=0))
        m_new = []
        for h in range(nh):
            hs = slice(h * qb, (h + 1) * qb)
            m_old = m_ref[:, hs]
            mn = jnp.maximum(m_old, jnp.max(mx[h], axis=0, keepdims=True))
            alpha_ref[:, hs] = jnp.exp2(m_old - mn)
            m_ref[:, hs] = mn
            m_new.append(jnp.broadcast_to(mn, (RC, qb)))
        for c in range(nch):
            mk = masks(c)
            for h in range(nh):
                sc = s_ref[slot, c * RC:(c + 1) * RC, h * qb:(h + 1) * qb]
                pr = jnp.where(mk, jnp.exp2(sc - m_new[h]), 0.0)
                p_ref[slot, c * RC:(c + 1) * RC, h * qb:(h + 1) * qb] = pr.astype(p_ref.dtype)

    def step(bi, cur):
        qk(bi + 1, 1 - cur)
        pv(bi - 1, 1 - cur)
        softmax(bi, cur)

    qk(0, 0)

    def pair_body(i2, carry):
        step(2 * i2, 0)
        step(2 * i2 + 1, 1)
        return carry

    lax.fori_loop(0, nblk // 2, pair_body, 0)
    odd = nblk % 2 == 1

    @pl.when(odd)
    def _():
        step(nblk - 1, 0)
        pv(nblk - 1, 0)

    @pl.when(jnp.logical_not(odd))
    def _():
        pv(nblk - 1, 1)

    o = (acc_ref[:KV_RANK, :] / acc_ref[KV_RANK:KV_RANK + 1, :]).astype(_MXU)
    yT = jnp.concatenate([_dot(wuvT_ref[h], o[:, h * qb:(h + 1) * qb]) for h in range(nh)], axis=0)
    o_ref[...] = yT.T


def _attention(proj, ckv, ckvT, kidx, wuk, wuvT, *, b, s, kb):
    nq = s // Q_BLOCK
    k_top = min(TOPK_MAX, s // 4)
    kern = functools.partial(_attn_kernel, k_top=k_top, kb=kb)
    hq = N_HEADS * Q_BLOCK
    return pl.pallas_call(
        kern,
        grid=(b, nq),
        in_specs=[
            pl.BlockSpec((Q_BLOCK, 512), lambda bi, j: (bi * nq + j, C_Q // 512)),
            pl.BlockSpec((Q_BLOCK, 512), lambda bi, j: (bi * nq + j, C_QIDX // 512)),
            pl.BlockSpec((Q_BLOCK, 128), lambda bi, j: (bi * nq + j, C_SMALL // 128)),
            pl.BlockSpec((1, s, KV_RANK), lambda bi, j: (bi, 0, 0)),
            pl.BlockSpec((1, s // kb, KV_RANK + ONES_ROWS, kb), lambda bi, j: (bi, 0, 0, 0)),
            pl.BlockSpec((1, s, IDX_DIM), lambda bi, j: (bi, 0, 0)),
            pl.BlockSpec((N_HEADS, KV_RANK, HEAD_DIM), lambda bi, j: (0, 0, 0)),
            pl.BlockSpec((N_HEADS, HEAD_DIM, KV_RANK), lambda bi, j: (0, 0, 0)),
        ],
        out_specs=pl.BlockSpec((Q_BLOCK, N_HEADS * HEAD_DIM), lambda bi, j: (bi * nq + j, 0)),
        out_shape=jax.ShapeDtypeStruct((b * s, N_HEADS * HEAD_DIM), F32),
        scratch_shapes=[
            pltpu.VMEM((s, Q_BLOCK), jnp.int32),
            pltpu.VMEM((32, s // 32, Q_BLOCK), jnp.int32),
            pltpu.VMEM((KV_RANK, hq), _MXU),
            pltpu.VMEM((2, kb, hq), F32),
            pltpu.VMEM((2, kb, hq), _MXU),
            pltpu.VMEM((KV_RANK + ONES_ROWS, hq), F32),
            pltpu.VMEM((1, hq), F32),
            pltpu.VMEM((1, hq), F32),
        ],
        compiler_params=_cparams(("parallel", "arbitrary")),
        name="dsa_attn",
    )(proj, proj, proj, ckv, ckvT, kidx, wuk, wuvT)


def _pool_kernel(x_ref, halo_ref, pw_ref, ps_ref, o_ref, *, tm):
    i = pl.program_id(1)
    x = x_ref[...]
    halo = jnp.where(i > 0, halo_ref[...], 0.0)
    xe = jnp.concatenate([halo, x], axis=0)
    pos = i * tm + lax.broadcasted_iota(jnp.int32, (tm, POOL_GROUP), 0)
    outs = []
    for g, w in enumerate(POOL_WINDOWS):
        sl = slice(g * POOL_GROUP, (g + 1) * POOL_GROUP)
        cur = xe[:, sl]
        span = 1
        while span < w:
            n = cur.shape[0]
            cur = cur[span:, :] + cur[:n - span, :]
            span *= 2
        win = cur[cur.shape[0] - tm:, :]
        count = jnp.minimum(pos + 1, w).astype(F32)
        pooled = win / count - x[:, sl]
        outs.append(_dot(pooled.astype(_MXU), pw_ref[g]))
    o_ref[...] = jnp.concatenate(outs, axis=1) * ps_ref[...]


def _pool(proj, pw, ps, *, b, s, tm):
    nt = s // tm
    kern = functools.partial(_pool_kernel, tm=tm)
    return pl.pallas_call(
        kern,
        grid=(b, nt),
        in_specs=[
            pl.BlockSpec((tm, POOL_WIDTH), lambda bi, i: (bi * nt + i, C_POOL // POOL_WIDTH)),
            pl.BlockSpec((16, POOL_WIDTH),
                         lambda bi, i: (jnp.maximum((bi * s + i * tm) // 16 - 1, 0), C_POOL // POOL_WIDTH)),
            pl.BlockSpec((len(POOL_WINDOWS), POOL_GROUP, POOL_GROUP), lambda bi, i: (0, 0, 0)),
            pl.BlockSpec((1, POOL_WIDTH), lambda bi, i: (0, 0)),
        ],
        out_specs=pl.BlockSpec((tm, POOL_WIDTH), lambda bi, i: (bi * nt + i, 0)),
        out_shape=jax.ShapeDtypeStruct((b * s, POOL_WIDTH), F32),
        compiler_params=_cparams(("parallel", "parallel")),
        name="pool",
    )(proj, proj, pw, ps)


def _ssd_kernel(xbc_ref, z_ref, sm_ref, cw_ref, cb_ref, dtb_row_ref, dtb_col_ref, a_x_ref, a_col_ref,
                dsk_x_ref, expand_ref, ng_ref, o_ref, state_ref, tail_ref):
    c = pl.program_id(1)
    L = CHUNK
    P = SSD_HEAD_DIM

    @pl.when(c == 0)
    def _():
        state_ref[...] = jnp.zeros_like(state_ref)
        tail_ref[...] = jnp.zeros_like(tail_ref)

    x_raw = xbc_ref[...]
    xe = jnp.concatenate([tail_ref[...], x_raw], axis=0)
    cw = cw_ref[...]
    conv = cb_ref[...] + cw[3:4, :] * x_raw
    for k in range(CONV_WIDTH - 1):
        sh = CONV_WIDTH - 1 - k
        conv = conv + cw[k:k + 1, :] * xe[8 - sh:8 - sh + L, :]
    tail_ref[...] = x_raw[L - 8:, :]
    xc = _silu(conv)
    xs = xc[:, :D_INNER]
    bm = xc[:, D_INNER:D_INNER + N_GROUPS * D_STATE]
    cm = xc[:, D_INNER + N_GROUPS * D_STATE:]

    sm = sm_ref[...]
    smT = sm.T
    dt = jax.nn.softplus(sm[:, SM_DT:SM_DT + SSD_HEADS] + dtb_row_ref[...])
    dtT = jax.nn.softplus(smT[SM_DT:SM_DT + SSD_HEADS, :] + dtb_col_ref[...])
    hi = lax.Precision.HIGHEST
    dtx = jnp.dot(dt, expand_ref[...], precision=hi, preferred_element_type=F32)
    dax = dtx * a_x_ref[...]
    daT = dtT * a_col_ref[...]
    row = lax.broadcasted_iota(jnp.int32, (L, L), 0)
    col = lax.broadcasted_iota(jnp.int32, (L, L), 1)
    lower = row >= col
    tril = jnp.where(lower, 1.0, 0.0).astype(F32)
    triu = jnp.where(row <= col, 1.0, 0.0).astype(F32)
    acs_x = jnp.dot(tril, dax, precision=hi, preferred_element_type=F32)
    acsT = jnp.dot(daT, triu, precision=hi, preferred_element_type=F32)
    a_last = acs_x[L - 1:L, :]
    ea_x = jnp.exp(acs_x)
    decay_x = jnp.exp(a_last - acs_x)
    cd_x = jnp.exp(a_last)

    X = xs * dtx
    Xb = X.astype(_MXU)
    Xd = (X * decay_x).astype(_MXU)
    state = state_ref[...]
    ys = []
    hpg = SSD_HEADS // N_GROUPS
    for g in range(N_GROUPS):
        bg = bm[:, g * D_STATE:(g + 1) * D_STATE]
        cg = cm[:, g * D_STATE:(g + 1) * D_STATE].astype(_MXU)
        bgT = bg.T.astype(_MXU)
        cb = _dot(cg, bgT)
        gs = slice(g * hpg * P, (g + 1) * hpg * P)
        y_off = _dot(cg, state[:, gs].astype(_MXU)) * ea_x[:, gs]
        state_ref[:, gs] = state[:, gs] * cd_x[:, gs] + _dot(bgT, Xd[:, gs])
        for e in range(hpg):
            h = g * hpg + e
            hs = slice(h * P, (h + 1) * P)
            diff = acs_x[:, h * P:h * P + 1] - acsT[h:h + 1, :]
            lmat = jnp.exp(jnp.where(lower, diff, -jnp.inf))
            m = (cb * lmat).astype(_MXU)
            ys.append(_dot(m, Xb[:, hs]) + y_off[:, e * P:(e + 1) * P])
    y = jnp.concatenate(ys, axis=1) + xs * dsk_x_ref[...]
    y = y * _silu(z_ref[...])
    o_ref[...] = _rms(y, ng_ref[...])


def _ssd(proj, cw, cb, dtb, a_log, dsk, ng, *, b, s):
    nc = s // CHUNK
    rep = lambda v: jnp.repeat(v, SSD_HEAD_DIM)[None, :]
    a = -jnp.exp(a_log)
    expand = jnp.repeat(jnp.eye(SSD_HEADS, dtype=F32), SSD_HEAD_DIM, axis=1)
    const2 = lambda bi, c: (0, 0)
    return pl.pallas_call(
        _ssd_kernel,
        grid=(b, nc),
        in_specs=[
            pl.BlockSpec((CHUNK, CONV_DIM), lambda bi, c: (bi * nc + c, C_XBC // CONV_DIM)),
            pl.BlockSpec((CHUNK, D_INNER), lambda bi, c: (bi * nc + c, C_Z // D_INNER)),
            pl.BlockSpec((CHUNK, 128), lambda bi, c: (bi * nc + c, C_SMALL // 128)),
            pl.BlockSpec((CONV_WIDTH, CONV_DIM), const2),
            pl.BlockSpec((1, CONV_DIM), const2),
            pl.BlockSpec((1, SSD_HEADS), const2),
            pl.BlockSpec((SSD_HEADS, 1), const2),
            pl.BlockSpec((1, D_INNER), const2),
            pl.BlockSpec((SSD_HEADS, 1), const2),
            pl.BlockSpec((1, D_INNER), const2),
            pl.BlockSpec((SSD_HEADS, D_INNER), const2),
            pl.BlockSpec((1, D_INNER), const2),
        ],
        out_specs=pl.BlockSpec((CHUNK, D_INNER), lambda bi, c: (bi * nc + c, 0)),
        out_shape=jax.ShapeDtypeStruct((b * s, D_INNER), F32),
        scratch_shapes=[pltpu.VMEM((D_STATE, D_INNER), F32), pltpu.VMEM((8, CONV_DIM), F32)],
        compiler_params=_cparams(("parallel", "arbitrary")),
        name="ssd",
    )(proj, proj, proj, cw, cb[None, :], dtb[None, :], dtb[:, None], rep(a), a[:, None],
      rep(dsk), expand, ng[None, :])


def _merge_kernel(ya_ref, yp_ref, ys_ref, ga_ref, gp_ref, gs_ref, h_ref, wa_ref, wp_ref, ws_ref,
                  wo_ref, o_ref):
    m = _sigmoid(ga_ref[...]) * _dot(ya_ref[...].astype(_MXU), wa_ref[...])
    m = m + _sigmoid(gp_ref[...]) * _dot(yp_ref[...].astype(_MXU), wp_ref[...])
    m = m + _sigmoid(gs_ref[...]) * _dot(ys_ref[...].astype(_MXU), ws_ref[...])
    o_ref[...] = h_ref[...] + _dot(m.astype(_MXU), wo_ref[...])


def _merge(ya, yp, ys, proj, h, wa, wp, ws, wo, *, tm):
    t, d = h.shape
    row = lambda i: (i, 0)
    const = lambda i: (0, 0)
    gate = lambda k: (lambda i: (i, C_GATE // d + k))
    return pl.pallas_call(
        _merge_kernel,
        grid=(t // tm,),
        in_specs=[
            pl.BlockSpec((tm, ya.shape[1]), row),
            pl.BlockSpec((tm, yp.shape[1]), row),
            pl.BlockSpec((tm, ys.shape[1]), row),
            pl.BlockSpec((tm, d), gate(0)),
            pl.BlockSpec((tm, d), gate(1)),
            pl.BlockSpec((tm, d), gate(2)),
            pl.BlockSpec((tm, d), row),
            pl.BlockSpec(wa.shape, const),
            pl.BlockSpec(wp.shape, const),
            pl.BlockSpec(ws.shape, const),
            pl.BlockSpec(wo.shape, const),
        ],
        out_specs=pl.BlockSpec((tm, d), row),
        out_shape=jax.ShapeDtypeStruct((t, d), F32),
        compiler_params=_cparams(("parallel",)),
        name="merge",
    )(ya, yp, ys, proj, proj, proj, h, wa, wp, ws, wo)


def _ple_kernel(h_ref, p_ref, g_ref, wg_ref, wp_ref, fg_ref, o_ref, *, final):
    h = h_ref[...]
    gate = _sigmoid(_dot(_rms(h, g_ref[...]).astype(_MXU), wg_ref[...]))
    h = h + gate * _dot(p_ref[...].astype(_MXU), wp_ref[...])
    if final:
        h = _rms(h, fg_ref[...])
    o_ref[...] = h


def _ple(h, p, g, wg, wp, fg, *, tm, final):
    t, d = h.shape
    row = lambda i: (i, 0)
    const = lambda i: (0, 0)
    return pl.pallas_call(
        functools.partial(_ple_kernel, final=final),
        grid=(t // tm,),
        in_specs=[
            pl.BlockSpec((tm, d), row),
            pl.BlockSpec((tm, p.shape[1]), row),
            pl.BlockSpec((1, d), const),
            pl.BlockSpec(wg.shape, const),
            pl.BlockSpec(wp.shape, const),
            pl.BlockSpec((1, d), const),
        ],
        out_specs=pl.BlockSpec((tm, d), row),
        out_shape=jax.ShapeDtypeStruct((t, d), F32),
        compiler_params=_cparams(("parallel",)),
        name="ple",
    )(h, p, g, wg, wp, fg)


def _reorder_w_in(w):
    o_q, o_ckv, o_qidx, o_widx, o_kidx, o_pool, o_z, o_xbc, o_dt, o_gate = (
        0, 512, 768, 1280, 1288, 1352, 1864, 2888, 4424, 4440)
    d = w.shape[0]
    small = jnp.concatenate([
        w[:, o_kidx:o_kidx + IDX_DIM], w[:, o_widx:o_widx + IDX_HEADS], w[:, o_dt:o_dt + SSD_HEADS],
        jnp.zeros((d, 128 - IDX_DIM - IDX_HEADS - SSD_HEADS), w.dtype)], axis=1)
    cols = [
        w[:, o_xbc:o_xbc + CONV_DIM], w[:, o_q:o_q + 512], w[:, o_z:o_z + D_INNER],
        w[:, o_gate:o_gate + 3 * D_MODEL], w[:, o_qidx:o_qidx + 512], w[:, o_pool:o_pool + POOL_WIDTH],
        w[:, o_ckv:o_ckv + KV_RANK], small,
    ]
    out = jnp.concatenate(cols, axis=1)
    return jnp.pad(out, ((0, 0), (0, PROJ_COLS - out.shape[1])))


def kernel(x, p, ffn1_norm, ffn1_w_gate, ffn1_w_up, ffn1_w_down, mix_norm, w_in, kv_norm, idx_k_norm, w_uk, w_uv, pool_w, pool_scale, conv_w, conv_b, dt_bias, a_log, d_skip, ssd_norm, w_br_attn, w_br_pool, w_br_ssd, w_out, ffn2_norm, ffn2_w_gate, ffn2_w_up, ffn2_w_down, ple_norm, ple_w_gate, ple_w_proj, final_norm):
    b, s, d = x.shape
    depth = w_in.shape[0]
    t = b * s
    tm = min(1024, t)
    tm_small = min(512, t)
    kb = min(512, s)
    tf = 1408
    mx = lambda a: a.astype(_MXU)

    h = x.reshape(t, d)
    for i in range(depth):
        h = _ffn(h, ffn1_norm[i][None], mx(ffn1_w_gate[i]), mx(ffn1_w_up[i]), mx(ffn1_w_down[i]),
                 tm=tm_small, tf=tf)
        proj = _inproj(h, mix_norm[i][None], mx(_reorder_w_in(w_in[i])), tm=tm, tn=512)
        ckv, ckvT, kidx = _kvprep(proj, kv_norm[i][None], idx_k_norm[i][None], b=b, s=s, kb=kb)
        wuk = mx(jnp.transpose(w_uk[i], (1, 0, 2)))
        wuvT = mx(jnp.transpose(w_uv[i], (1, 2, 0)))
        y_attn = _attention(proj, ckv, ckvT, kidx, wuk, wuvT, b=b, s=s, kb=kb)
        y_pool = _pool(proj, mx(pool_w[i]), pool_scale[i][None], b=b, s=s, tm=tm_small)
        y_ssd = _ssd(proj, conv_w[i], conv_b[i], dt_bias[i], a_log[i], d_skip[i], ssd_norm[i], b=b, s=s)
        h = _merge(y_attn, y_pool, y_ssd, proj, h, mx(w_br_attn[i]), mx(w_br_pool[i]),
                   mx(w_br_ssd[i]), mx(w_out[i]), tm=tm_small)
        h = _ffn(h, ffn2_norm[i][None], mx(ffn2_w_gate[i]), mx(ffn2_w_up[i]), mx(ffn2_w_down[i]),
                 tm=tm_small, tf=tf)
        h = _ple(h, p[i].reshape(t, -1), ple_norm[i][None], mx(ple_w_gate[i]), mx(ple_w_proj[i]),
                 final_norm[None], tm=tm_small, final=(i == depth - 1))
    return h.reshape(b, s, d)
```

```python
import functools

import jax
import jax.numpy as jnp
from jax import lax
from jax.experimental import pallas as pl
from jax.experimental.pallas import tpu as pltpu

_MXU = jnp.bfloat16
F32 = jnp.float32

D_MODEL = 1024
N_HEADS = 8
HEAD_DIM = 64
KV_RANK = 256
IDX_HEADS = 8
IDX_DIM = 64
TOPK_MAX = 256
Q_BLOCK = 128
POOL_WINDOWS = (2, 4, 8, 16)
POOL_WIDTH = 512
POOL_GROUP = 128
SSD_HEADS = 16
SSD_HEAD_DIM = 64
D_INNER = 1024
N_GROUPS = 2
D_STATE = 128
CONV_WIDTH = 4
CONV_DIM = D_INNER + 2 * N_GROUPS * D_STATE
CHUNK = 128
D_FF = 2816
PLE_DIM = 256
EPS = 1e-6

C_XBC = 0
C_Q = 1536
C_Z = 2048
C_GATE = 3072
C_QIDX = 6144
C_POOL = 6656
C_CKV = 7168
PROJ_COLS = 7680
SM_KIDX = 0
SM_WIDX = 64
SM_DT = 72

INT_MIN = -2147483648
LOG2E = 1.4426950408889634
RC = 32
ONES_ROWS = 16
CONV_TAIL = 16
NEG_BIG = -1e30
VMEM_LIMIT = 56 * 1024 * 1024


def _cparams(sem):
    return pltpu.CompilerParams(dimension_semantics=sem, vmem_limit_bytes=VMEM_LIMIT)


def _rms(x, g):
    return x * lax.rsqrt(jnp.mean(x * x, axis=-1, keepdims=True) + EPS) * g


def _dot(a, b):
    return jnp.dot(a, b, preferred_element_type=F32)


def _sigmoid(x):
    return 0.5 * jnp.tanh(0.5 * x) + 0.5


def _silu(x):
    h = 0.5 * x
    return h * jnp.tanh(h) + h


def _split3(a):
    hi = a.astype(jnp.bfloat16)
    r1 = a - hi.astype(F32)
    mid = r1.astype(jnp.bfloat16)
    lo = (r1 - mid.astype(F32)).astype(jnp.bfloat16)
    return hi, mid, lo


def _dot01_right(a, b01):
    b = b01.astype(jnp.bfloat16)
    hi, mid, lo = _split3(a)
    return _dot(hi, b) + _dot(mid, b) + _dot(lo, b)


def _select_rows(a01, b):
    if b.dtype == jnp.bfloat16:
        return _dot(a01.astype(jnp.bfloat16), b)
    return _dot01_left(a01, b)


def _dot01_left(a01, b):
    a = a01.astype(jnp.bfloat16)
    hi, mid, lo = _split3(b)
    return _dot(a, hi) + _dot(a, mid) + _dot(a, lo)


def _ffn_kernel(h_ref, g_ref, wg_ref, wu_ref, wd_ref, o_ref, xn_ref, acc_ref):
    k = pl.program_id(1)

    @pl.when(k == 0)
    def _():
        xn_ref[...] = _rms(h_ref[...], g_ref[...]).astype(xn_ref.dtype)
        acc_ref[...] = jnp.zeros_like(acc_ref)

    xn = xn_ref[...]
    g = _dot(xn, wg_ref[...])
    u = _dot(xn, wu_ref[...])
    a = _silu(g) * u
    acc_ref[...] += _dot(a.astype(_MXU), wd_ref[...])

    @pl.when(k == pl.num_programs(1) - 1)
    def _():
        o_ref[...] = h_ref[...] + 0.5 * acc_ref[...]


def _ffn(h, g, wg, wu, wd, *, tm, tf):
    t, d = h.shape
    ff = wg.shape[1]
    return pl.pallas_call(
        _ffn_kernel,
        grid=(t // tm, ff // tf),
        in_specs=[
            pl.BlockSpec((tm, d), lambda i, k: (i, 0)),
            pl.BlockSpec((1, d), lambda i, k: (0, 0)),
            pl.BlockSpec((d, tf), lambda i, k: (0, k)),
            pl.BlockSpec((d, tf), lambda i, k: (0, k)),
            pl.BlockSpec((tf, d), lambda i, k: (k, 0)),
        ],
        out_specs=pl.BlockSpec((tm, d), lambda i, k: (i, 0)),
        out_shape=jax.ShapeDtypeStruct((t, d), F32),
        scratch_shapes=[pltpu.VMEM((tm, d), _MXU), pltpu.VMEM((tm, d), F32)],
        compiler_params=_cparams(("parallel", "arbitrary")),
        name="ffn",
    )(h, g, wg, wu, wd)


def _inproj_kernel(h_ref, g_ref, w_ref, ws_ref, o_ref, os_ref, xn_ref):
    @pl.when(pl.program_id(1) == 0)
    def _():
        xn = _rms(h_ref[...], g_ref[...]).astype(xn_ref.dtype)
        xn_ref[...] = xn
        os_ref[...] = _dot(xn, ws_ref[...])

    o_ref[...] = _dot(xn_ref[...], w_ref[...]).astype(o_ref.dtype)


def _inproj(h, g, w, ws, *, tm, tn):
    t, d = h.shape
    n = w.shape[1]
    return pl.pallas_call(
        _inproj_kernel,
        grid=(t // tm, n // tn),
        in_specs=[
            pl.BlockSpec((tm, d), lambda i, j: (i, 0)),
            pl.BlockSpec((1, d), lambda i, j: (0, 0)),
            pl.BlockSpec((d, tn), lambda i, j: (0, j)),
            pl.BlockSpec((d, 128), lambda i, j: (0, 0)),
        ],
        out_specs=[pl.BlockSpec((tm, tn), lambda i, j: (i, j)),
                   pl.BlockSpec((tm, 128), lambda i, j: (i, 0))],
        out_shape=[jax.ShapeDtypeStruct((t, n), _MXU), jax.ShapeDtypeStruct((t, 128), F32)],
        scratch_shapes=[pltpu.VMEM((tm, d), _MXU)],
        compiler_params=_cparams(("parallel", "arbitrary")),
        name="inproj",
    )(h, g, w, ws)


def _kvprep_kernel(ckv_ref, sm_ref, gkv_ref, gk_ref, ckv_o, ckvT_o, kidx_o):
    c = _rms(ckv_ref[...].astype(F32), gkv_ref[...])
    ckv_o[0] = c.astype(ckv_o.dtype)
    ones = jnp.ones((ONES_ROWS, c.shape[0]), F32)
    ckvT_o[0, 0] = jnp.concatenate([c.T, ones], axis=0).astype(ckvT_o.dtype)
    k = sm_ref[...][:, SM_KIDX:SM_KIDX + IDX_DIM]
    kidx_o[0] = _rms(k, gk_ref[...]).astype(kidx_o.dtype)


def _kvprep(proj, small, gkv, gk, *, b, s, kb):
    nk = s // kb
    return pl.pallas_call(
        _kvprep_kernel,
        grid=(b, nk),
        in_specs=[
            pl.BlockSpec((kb, KV_RANK), lambda bi, i: (bi * nk + i, C_CKV // KV_RANK)),
            pl.BlockSpec((kb, 128), lambda bi, i: (bi * nk + i, 0)),
            pl.BlockSpec((1, KV_RANK), lambda bi, i: (0, 0)),
            pl.BlockSpec((1, IDX_DIM), lambda bi, i: (0, 0)),
        ],
        out_specs=[
            pl.BlockSpec((1, kb, KV_RANK), lambda bi, i: (bi, i, 0)),
            pl.BlockSpec((1, 1, KV_RANK + ONES_ROWS, kb), lambda bi, i: (bi, i, 0, 0)),
            pl.BlockSpec((1, kb, IDX_DIM), lambda bi, i: (bi, i, 0)),
        ],
        out_shape=[
            jax.ShapeDtypeStruct((b, s, KV_RANK), _MXU),
            jax.ShapeDtypeStruct((b, nk, KV_RANK + ONES_ROWS, kb), _MXU),
            jax.ShapeDtypeStruct((b, s, IDX_DIM), _MXU),
        ],
        compiler_params=_cparams(("parallel", "parallel")),
        name="kvprep",
    )(proj, small, gkv, gk)


def _bit_transpose32(words):
    words = list(words)
    j, m = 16, 0x0000FFFF
    while j:
        mi = jnp.int32(m - (1 << 32) if m >= (1 << 31) else m)
        for k in range(32):
            if k & j == 0:
                t = (words[k] ^ lax.shift_right_logical(words[k + j], jnp.int32(j))) & mi
                words[k] = words[k] ^ t
                words[k + j] = words[k + j] ^ lax.shift_left(t, jnp.int32(j))
        j >>= 1
        m = (m ^ (m << j)) & 0xFFFFFFFF
    return words


def _attn_kernel(q_ref, qi_ref, sm_ref, ckv_ref, ckvT_ref, kidx_ref, wuk_ref, wuvT_ref, o_ref,
                 key_ref, planes_ref, qlat_ref, s_ref, p_ref, acc_ref, m_ref, alpha_ref,
                 *, k_top, kb):
    j = pl.program_id(1)
    nblk = (j * Q_BLOCK + Q_BLOCK + kb - 1) // kb
    nh = N_HEADS
    qb = Q_BLOCK

    qT = q_ref[...].astype(F32).T
    qiT = qi_ref[...].astype(F32).T
    smT = sm_ref[...].T
    w_row = jnp.concatenate([smT[SM_WIDX + h:SM_WIDX + h + 1, :] for h in range(nh)], axis=1)
    qiT_stack = jnp.concatenate(
        [qiT[h * IDX_DIM:(h + 1) * IDX_DIM, :] for h in range(nh)], axis=1).astype(_MXU)
    scale = HEAD_DIM ** -0.5
    qlatT = jnp.concatenate(
        [_dot(wuk_ref[h], qT[h * HEAD_DIM:(h + 1) * HEAD_DIM, :].astype(_MXU)) for h in range(nh)],
        axis=1)
    qlat_ref[...] = (qlatT * (scale * LOG2E)).astype(_MXU)

    qpos = j * qb + lax.broadcasted_iota(jnp.int32, (kb, qb), 1)

    def idx_body(bi, carry):
        off = pl.multiple_of(bi * kb, kb)
        kblk = kidx_ref[0, pl.ds(off, kb), :]
        r = jnp.maximum(_dot(kblk, qiT_stack), 0.0) * w_row
        sc = r[:, 0:qb]
        for h in range(1, nh):
            sc = sc + r[:, h * qb:(h + 1) * qb]
        bits = pltpu.bitcast(sc, jnp.int32)
        key = jnp.where(bits < 0, bits ^ jnp.int32(0x7FFFFFFF), bits)
        kpos = off + lax.broadcasted_iota(jnp.int32, (kb, qb), 0)
        key = jnp.where(kpos <= qpos, key, jnp.int32(INT_MIN))
        key_ref[pl.ds(off, kb), :] = key
        u = key ^ jnp.int32(INT_MIN)
        for g in range(kb // 256):
            words = _bit_transpose32([u[g * 256 + 8 * t:g * 256 + 8 * t + 8, :] for t in range(32)])
            row = pl.multiple_of((bi * (kb // 256) + g) * 8, 8)
            for i in range(32):
                planes_ref[i, pl.ds(row, 8), :] = words[i]
        return carry

    lax.fori_loop(0, nblk, idx_body, 0)

    ngrp = planes_ref.shape[1] // 8
    grp = lax.broadcasted_iota(jnp.int32, (ngrp * 8, qb), 0) // 8
    eq0 = jnp.where(grp < nblk * (kb // 256), jnp.int32(-1), jnp.int32(0))

    def bit_body(i, carry):
        eq, n_above, prefix = carry
        x = planes_ref[i]
        ones = eq & x
        cnt = jnp.sum(lax.population_count(ones).reshape(ngrp, 8, qb), axis=0)
        cnt = jnp.sum(cnt, axis=0, keepdims=True)
        take = (n_above + cnt) >= k_top
        eq = jnp.where(take, ones, eq & ~x)
        n_above = jnp.where(take, n_above, n_above + cnt)
        prefix = jnp.where(take, prefix | jnp.left_shift(jnp.int32(1), 31 - i), prefix)
        return eq, n_above, prefix

    zrow = jnp.zeros((1, qb), jnp.int32)
    _, _, prefix = lax.fori_loop(0, 32, bit_body, (eq0, zrow, zrow))
    thr = jnp.maximum(prefix ^ jnp.int32(INT_MIN), jnp.int32(INT_MIN + 1))
    thr_b = jnp.broadcast_to(thr, (RC, qb))

    m_ref[...] = jnp.full(m_ref.shape, NEG_BIG, F32)
    alpha_ref[...] = jnp.ones_like(alpha_ref)
    acc_ref[...] = jnp.zeros_like(acc_ref)
    p_ref[1] = jnp.zeros(p_ref.shape[1:], p_ref.dtype)

    def qk(bi, slot):
        off = pl.multiple_of(jnp.minimum(bi, nblk - 1) * kb, kb)
        s_ref[slot] = _dot(ckv_ref[0, pl.ds(off, kb), :], qlat_ref[...])

    def pv(bi, slot):
        acc_ref[...] = alpha_ref[...] * acc_ref[...] + _dot(ckvT_ref[0, jnp.maximum(bi, 0)], p_ref[slot])

    def softmax(bi, slot):
        off = pl.multiple_of(bi * kb, kb)
        nch = kb // RC

        def masks(c):
            return key_ref[pl.ds(off + c * RC, RC), :] >= thr_b

        mx = [jnp.full((8, qb), NEG_BIG, F32)] * nh
        for c in range(nch):
            mk = masks(c)
            for h in range(nh):
                sc = jnp.where(mk, s_ref[slot, c * RC:(c + 1) * RC, h * qb:(h + 1) * qb], NEG_BIG)
                mx[h] = jnp.maximum(mx[h], jnp.max(sc.reshape(RC // 8, 8, qb), axis=0))
        m_new = []
        for h in range(nh):
            hs = slice(h * qb, (h + 1) * qb)
            m_old = m_ref[:, hs]
            mn = jnp.maximum(m_old, jnp.max(mx[h], axis=0, keepdims=True))
            alpha_ref[:, hs] = jnp.exp2(m_old - mn)
            m_ref[:, hs] = mn
            m_new.append(jnp.broadcast_to(mn, (RC, qb)))
        for c in range(nch):
            mk = masks(c)
            for h in range(nh):
                sc = s_ref[slot, c * RC:(c + 1) * RC, h * qb:(h + 1) * qb]
                pr = jnp.where(mk, jnp.exp2(sc - m_new[h]), 0.0)
                p_ref[slot, c * RC:(c + 1) * RC, h * qb:(h + 1) * qb] = pr.astype(p_ref.dtype)

    def step(bi, cur):
        qk(bi + 1, 1 - cur)
        pv(bi - 1, 1 - cur)
        softmax(bi, cur)

    qk(0, 0)

    def pair_body(i2, carry):
        step(2 * i2, 0)
        step(2 * i2 + 1, 1)
        return carry

    lax.fori_loop(0, nblk // 2, pair_body, 0)
    odd = nblk % 2 == 1

    @pl.when(odd)
    def _():
        step(nblk - 1, 0)
        pv(nblk - 1, 0)

    @pl.when(jnp.logical_not(odd))
    def _():
        pv(nblk - 1, 1)

    o = (acc_ref[:KV_RANK, :] / acc_ref[KV_RANK:KV_RANK + 1, :]).astype(_MXU)
    yT = jnp.concatenate([_dot(wuvT_ref[h], o[:, h * qb:(h + 1) * qb]) for h in range(nh)], axis=0)
    o_ref[...] = yT.T


def _attention(proj, small, ckv, ckvT, kidx, wuk, wuvT, *, b, s, kb):
    nq = s // Q_BLOCK
    k_top = min(TOPK_MAX, s // 4)
    kern = functools.partial(_attn_kernel, k_top=k_top, kb=kb)
    hq = N_HEADS * Q_BLOCK
    return pl.pallas_call(
        kern,
        grid=(b, nq),
        in_specs=[
            pl.BlockSpec((Q_BLOCK, 512), lambda bi, j: (bi * nq + j, C_Q // 512)),
            pl.BlockSpec((Q_BLOCK, 512), lambda bi, j: (bi * nq + j, C_QIDX // 512)),
            pl.BlockSpec((Q_BLOCK, 128), lambda bi, j: (bi * nq + j, 0)),
            pl.BlockSpec((1, s, KV_RANK), lambda bi, j: (bi, 0, 0)),
            pl.BlockSpec((1, s // kb, KV_RANK + ONES_ROWS, kb), lambda bi, j: (bi, 0, 0, 0)),
            pl.BlockSpec((1, s, IDX_DIM), lambda bi, j: (bi, 0, 0)),
            pl.BlockSpec((N_HEADS, KV_RANK, HEAD_DIM), lambda bi, j: (0, 0, 0)),
            pl.BlockSpec((N_HEADS, HEAD_DIM, KV_RANK), lambda bi, j: (0, 0, 0)),
        ],
        out_specs=pl.BlockSpec((Q_BLOCK, N_HEADS * HEAD_DIM), lambda bi, j: (bi * nq + j, 0)),
        out_shape=jax.ShapeDtypeStruct((b * s, N_HEADS * HEAD_DIM), F32),
        scratch_shapes=[
            pltpu.VMEM((s, Q_BLOCK), jnp.int32),
            pltpu.VMEM((32, s // 32, Q_BLOCK), jnp.int32),
            pltpu.VMEM((KV_RANK, hq), _MXU),
            pltpu.VMEM((2, kb, hq), F32),
            pltpu.VMEM((2, kb, hq), _MXU),
            pltpu.VMEM((KV_RANK + ONES_ROWS, hq), F32),
            pltpu.VMEM((1, hq), F32),
            pltpu.VMEM((1, hq), F32),
        ],
        compiler_params=_cparams(("parallel", "arbitrary")),
        name="dsa_attn",
    )(proj, proj, small, ckv, ckvT, kidx, wuk, wuvT)


def _pool_kernel(x_ref, halo_ref, pw_ref, ps_ref, o_ref, *, tm):
    i = pl.program_id(1)
    x = x_ref[...].astype(F32)
    halo = jnp.where(i > 0, halo_ref[...].astype(F32), 0.0)
    xe = jnp.concatenate([halo, x], axis=0)
    pos = i * tm + lax.broadcasted_iota(jnp.int32, (tm, POOL_GROUP), 0)
    outs = []
    for g, w in enumerate(POOL_WINDOWS):
        sl = slice(g * POOL_GROUP, (g + 1) * POOL_GROUP)
        cur = xe[:, sl]
        span = 1
        while span < w:
            n = cur.shape[0]
            cur = cur[span:, :] + cur[:n - span, :]
            span *= 2
        win = cur[cur.shape[0] - tm:, :]
        count = jnp.minimum(pos + 1, w).astype(F32)
        pooled = win / count - x[:, sl]
        outs.append(_dot(pooled.astype(_MXU), pw_ref[g]))
    o_ref[...] = jnp.concatenate(outs, axis=1) * ps_ref[...]


def _pool(proj, pw, ps, *, b, s, tm):
    nt = s // tm
    kern = functools.partial(_pool_kernel, tm=tm)
    return pl.pallas_call(
        kern,
        grid=(b, nt),
        in_specs=[
            pl.BlockSpec((tm, POOL_WIDTH), lambda bi, i: (bi * nt + i, C_POOL // POOL_WIDTH)),
            pl.BlockSpec((16, POOL_WIDTH),
                         lambda bi, i: (jnp.maximum((bi * s + i * tm) // 16 - 1, 0), C_POOL // POOL_WIDTH)),
            pl.BlockSpec((len(POOL_WINDOWS), POOL_GROUP, POOL_GROUP), lambda bi, i: (0, 0, 0)),
            pl.BlockSpec((1, POOL_WIDTH), lambda bi, i: (0, 0)),
        ],
        out_specs=pl.BlockSpec((tm, POOL_WIDTH), lambda bi, i: (bi * nt + i, 0)),
        out_shape=jax.ShapeDtypeStruct((b * s, POOL_WIDTH), F32),
        compiler_params=_cparams(("parallel", "parallel")),
        name="pool",
    )(proj, proj, pw, ps)


def _ssd_kernel(xbc_ref, halo_ref, z_ref, sm_ref, cw_ref, cb_ref, dtb_row_ref, dtb_col_ref, a_row_ref, a_col_ref,
                dsk_x_ref, expand_ref, ng_ref, o_ref, state_ref):
    c = pl.program_id(1)
    L = CHUNK
    P = SSD_HEAD_DIM

    @pl.when(c == 0)
    def _():
        state_ref[...] = jnp.zeros_like(state_ref)

    x_nat = xbc_ref[...]
    halo = halo_ref[...]
    halo = jnp.where(c > 0, halo, jnp.zeros_like(halo))
    xe = jnp.concatenate([halo, x_nat], axis=0)
    cw = cw_ref[...]
    conv = cb_ref[...] + cw[3:4, :] * x_nat.astype(F32)
    srow = lax.broadcasted_iota(jnp.int32, (L, L + CONV_TAIL), 0)
    scol = lax.broadcasted_iota(jnp.int32, (L, L + CONV_TAIL), 1)
    for k in range(CONV_WIDTH - 1):
        sh = CONV_WIDTH - 1 - k
        sel = jnp.where(scol == srow + (CONV_TAIL - sh), 1.0, 0.0)
        conv = conv + cw[k:k + 1, :] * _select_rows(sel, xe)
    xc = _silu(conv)
    xs = xc[:, :D_INNER]
    bm = xc[:, D_INNER:D_INNER + N_GROUPS * D_STATE]
    cm = xc[:, D_INNER + N_GROUPS * D_STATE:]

    sm = sm_ref[...]
    smT = sm.T
    dt = jax.nn.softplus(sm[:, SM_DT:SM_DT + SSD_HEADS] + dtb_row_ref[...])
    dtT = jax.nn.softplus(smT[SM_DT:SM_DT + SSD_HEADS, :] + dtb_col_ref[...])
    expand = expand_ref[...]
    dtx = _dot01_right(dt, expand)
    da = dt * a_row_ref[...]
    daT = dtT * a_col_ref[...]
    row = lax.broadcasted_iota(jnp.int32, (L, L), 0)
    col = lax.broadcasted_iota(jnp.int32, (L, L), 1)
    lower = row >= col
    tril = jnp.where(lower, 1.0, 0.0).astype(F32)
    triu = jnp.where(row <= col, 1.0, 0.0).astype(F32)
    acs = _dot01_left(tril, da)
    acsT = _dot01_right(daT, triu)
    acs_x = _dot01_right(acs, expand)
    a_last = acs_x[L - 1:L, :]
    ea_x = jnp.exp(acs_x)
    decay_x = jnp.exp(a_last - acs_x)
    cd_x = jnp.exp(a_last)

    X = xs * dtx
    Xb = X.astype(_MXU)
    Xd = (X * decay_x).astype(_MXU)
    state = state_ref[...]
    ys = []
    hpg = SSD_HEADS // N_GROUPS
    for g in range(N_GROUPS):
        bg = bm[:, g * D_STATE:(g + 1) * D_STATE]
        cg = cm[:, g * D_STATE:(g + 1) * D_STATE].astype(_MXU)
        bgT = bg.T.astype(_MXU)
        cb = _dot(cg, bgT)
        gs = slice(g * hpg * P, (g + 1) * hpg * P)
        y_off = _dot(cg, state[:, gs].astype(_MXU)) * ea_x[:, gs]
        state_ref[:, gs] = state[:, gs] * cd_x[:, gs] + _dot(bgT, Xd[:, gs])
        for e in range(hpg):
            h = g * hpg + e
            hs = slice(h * P, (h + 1) * P)
            diff = acs[:, h:h + 1] - acsT[h:h + 1, :]
            lmat = jnp.exp(jnp.where(lower, diff, -jnp.inf))
            m = (cb * lmat).astype(_MXU)
            ys.append(_dot(m, Xb[:, hs]) + y_off[:, e * P:(e + 1) * P])
    y = jnp.concatenate(ys, axis=1) + xs * dsk_x_ref[...]
    y = y * _silu(z_ref[...].astype(F32))
    o_ref[...] = _rms(y, ng_ref[...])


def _ssd(proj, small, cw, cb, dtb, a_log, dsk, ng, *, b, s):
    nc = s // CHUNK
    rep = lambda v: jnp.repeat(v, SSD_HEAD_DIM)[None, :]
    a = -jnp.exp(a_log)
    expand = jnp.repeat(jnp.eye(SSD_HEADS, dtype=F32), SSD_HEAD_DIM, axis=1)
    const2 = lambda bi, c: (0, 0)
    return pl.pallas_call(
        _ssd_kernel,
        grid=(b, nc),
        in_specs=[
            pl.BlockSpec((CHUNK, CONV_DIM), lambda bi, c: (bi * nc + c, C_XBC // CONV_DIM)),
            pl.BlockSpec((CONV_TAIL, CONV_DIM),
                         lambda bi, c: (jnp.maximum((bi * nc + c) * (CHUNK // CONV_TAIL) - 1, 0), C_XBC // CONV_DIM)),
            pl.BlockSpec((CHUNK, D_INNER), lambda bi, c: (bi * nc + c, C_Z // D_INNER)),
            pl.BlockSpec((CHUNK, 128), lambda bi, c: (bi * nc + c, 0)),
            pl.BlockSpec((CONV_WIDTH, CONV_DIM), const2),
            pl.BlockSpec((1, CONV_DIM), const2),
            pl.BlockSpec((1, SSD_HEADS), const2),
            pl.BlockSpec((SSD_HEADS, 1), const2),
            pl.BlockSpec((1, SSD_HEADS), const2),
            pl.BlockSpec((SSD_HEADS, 1), const2),
            pl.BlockSpec((1, D_INNER), const2),
            pl.BlockSpec((SSD_HEADS, D_INNER), const2),
            pl.BlockSpec((1, D_INNER), const2),
        ],
        out_specs=pl.BlockSpec((CHUNK, D_INNER), lambda bi, c: (bi * nc + c, 0)),
        out_shape=jax.ShapeDtypeStruct((b * s, D_INNER), F32),
        scratch_shapes=[pltpu.VMEM((D_STATE, D_INNER), F32)],
        compiler_params=_cparams(("parallel", "arbitrary")),
        name="ssd",
    )(proj, proj, proj, small, cw, cb[None, :], dtb[None, :], dtb[:, None], a[None, :], a[:, None],
      rep(dsk), expand, ng[None, :])


def _merge_kernel(ya_ref, yp_ref, ys_ref, ga_ref, gp_ref, gs_ref, h_ref, wa_ref, wp_ref, ws_ref,
                  wo_ref, o_ref):
    m = _sigmoid(ga_ref[...].astype(F32)) * _dot(ya_ref[...].astype(_MXU), wa_ref[...])
    m = m + _sigmoid(gp_ref[...].astype(F32)) * _dot(yp_ref[...].astype(_MXU), wp_ref[...])
    m = m + _sigmoid(gs_ref[...].astype(F32)) * _dot(ys_ref[...].astype(_MXU), ws_ref[...])
    o_ref[...] = h_ref[...] + _dot(m.astype(_MXU), wo_ref[...])


def _merge(ya, yp, ys, proj, h, wa, wp, ws, wo, *, tm):
    t, d = h.shape
    row = lambda i: (i, 0)
    const = lambda i: (0, 0)
    gate = lambda k: (lambda i: (i, C_GATE // d + k))
    return pl.pallas_call(
        _merge_kernel,
        grid=(t // tm,),
        in_specs=[
            pl.BlockSpec((tm, ya.shape[1]), row),
            pl.BlockSpec((tm, yp.shape[1]), row),
            pl.BlockSpec((tm, ys.shape[1]), row),
            pl.BlockSpec((tm, d), gate(0)),
            pl.BlockSpec((tm, d), gate(1)),
            pl.BlockSpec((tm, d), gate(2)),
            pl.BlockSpec((tm, d), row),
            pl.BlockSpec(wa.shape, const),
            pl.BlockSpec(wp.shape, const),
            pl.BlockSpec(ws.shape, const),
            pl.BlockSpec(wo.shape, const),
        ],
        out_specs=pl.BlockSpec((tm, d), row),
        out_shape=jax.ShapeDtypeStruct((t, d), F32),
        compiler_params=_cparams(("parallel",)),
        name="merge",
    )(ya, yp, ys, proj, proj, proj, h, wa, wp, ws, wo)


def _ple_kernel(h_ref, p_ref, g_ref, wg_ref, wp_ref, fg_ref, o_ref, *, final):
    h = h_ref[...]
    gate = _sigmoid(_dot(_rms(h, g_ref[...]).astype(_MXU), wg_ref[...]))
    h = h + gate * _dot(p_ref[...].astype(_MXU), wp_ref[...])
    if final:
        h = _rms(h, fg_ref[...])
    o_ref[...] = h


def _ple(h, p, g, wg, wp, fg, *, tm, final):
    t, d = h.shape
    row = lambda i: (i, 0)
    const = lambda i: (0, 0)
    return pl.pallas_call(
        functools.partial(_ple_kernel, final=final),
        grid=(t // tm,),
        in_specs=[
            pl.BlockSpec((tm, d), row),
            pl.BlockSpec((tm, p.shape[1]), row),
            pl.BlockSpec((1, d), const),
            pl.BlockSpec(wg.shape, const),
            pl.BlockSpec(wp.shape, const),
            pl.BlockSpec((1, d), const),
        ],
        out_specs=pl.BlockSpec((tm, d), row),
        out_shape=jax.ShapeDtypeStruct((t, d), F32),
        compiler_params=_cparams(("parallel",)),
        name="ple",
    )(h, p, g, wg, wp, fg)


def _reorder_w_in(w):
    o_q, o_ckv, o_qidx, o_widx, o_kidx, o_pool, o_z, o_xbc, o_dt, o_gate = (
        0, 512, 768, 1280, 1288, 1352, 1864, 2888, 4424, 4440)
    d = w.shape[0]
    small = jnp.concatenate([
        w[:, o_kidx:o_kidx + IDX_DIM], w[:, o_widx:o_widx + IDX_HEADS], w[:, o_dt:o_dt + SSD_HEADS],
        jnp.zeros((d, 128 - IDX_DIM - IDX_HEADS - SSD_HEADS), w.dtype)], axis=1)
    wb = w.astype(_MXU)
    used = CONV_DIM + 512 + D_INNER + 3 * D_MODEL + 512 + POOL_WIDTH + KV_RANK
    wide = jnp.concatenate([
        wb[:, o_xbc:o_xbc + CONV_DIM], wb[:, o_q:o_q + 512], wb[:, o_z:o_z + D_INNER],
        wb[:, o_gate:o_gate + 3 * D_MODEL], wb[:, o_qidx:o_qidx + 512], wb[:, o_pool:o_pool + POOL_WIDTH],
        wb[:, o_ckv:o_ckv + KV_RANK], jnp.zeros((d, PROJ_COLS - used), _MXU)], axis=1)
    return wide, small.astype(_MXU)


def kernel(x, p, ffn1_norm, ffn1_w_gate, ffn1_w_up, ffn1_w_down, mix_norm, w_in, kv_norm, idx_k_norm, w_uk, w_uv, pool_w, pool_scale, conv_w, conv_b, dt_bias, a_log, d_skip, ssd_norm, w_br_attn, w_br_pool, w_br_ssd, w_out, ffn2_norm, ffn2_w_gate, ffn2_w_up, ffn2_w_down, ple_norm, ple_w_gate, ple_w_proj, final_norm):
    b, s, d = x.shape
    depth = w_in.shape[0]
    t = b * s
    tm = min(1024, t)
    tm_small = min(512, t)
    kb = min(512, s)
    tf = 1408
    mx = lambda a: a.astype(_MXU)

    h = x.reshape(t, d)
    for i in range(depth):
        h = _ffn(h, ffn1_norm[i][None], mx(ffn1_w_gate[i]), mx(ffn1_w_up[i]), mx(ffn1_w_down[i]),
                 tm=tm_small, tf=tf)
        w_wide, w_small = _reorder_w_in(w_in[i])
        proj, small = _inproj(h, mix_norm[i][None], w_wide, w_small, tm=tm, tn=512)
        ckv, ckvT, kidx = _kvprep(proj, small, kv_norm[i][None], idx_k_norm[i][None], b=b, s=s, kb=kb)
        wuk = mx(jnp.transpose(w_uk[i], (1, 0, 2)))
        wuvT = mx(jnp.transpose(w_uv[i], (1, 2, 0)))
        y_attn = _attention(proj, small, ckv, ckvT, kidx, wuk, wuvT, b=b, s=s, kb=kb)
        y_pool = _pool(proj, mx(pool_w[i]), pool_scale[i][None], b=b, s=s, tm=tm_small)
        y_ssd = _ssd(proj, small, conv_w[i], conv_b[i], dt_bias[i], a_log[i], d_skip[i], ssd_norm[i], b=b, s=s)
        h = _merge(y_attn, y_pool, y_ssd, proj, h, mx(w_br_attn[i]), mx(w_br_pool[i]),
                   mx(w_br_ssd[i]), mx(w_out[i]), tm=tm_small)
        h = _ffn(h, ffn2_norm[i][None], mx(ffn2_w_gate[i]), mx(ffn2_w_up[i]), mx(ffn2_w_down[i]),
                 tm=tm_small, tf=tf)
        h = _ple(h, p[i].reshape(t, -1), ple_norm[i][None], mx(ple_w_gate[i]), mx(ple_w_proj[i]),
                 final_norm[None], tm=tm_small, final=(i == depth - 1))
    return h.reshape(b, s, d)
```

```python
import functools

import jax
import jax.numpy as jnp
from jax import lax
from jax.experimental import pallas as pl
from jax.experimental.pallas import tpu as pltpu

_MXU = jnp.bfloat16
F32 = jnp.float32

D_MODEL = 1024
N_HEADS = 8
HEAD_DIM = 64
KV_RANK = 256
IDX_HEADS = 8
IDX_DIM = 64
TOPK_MAX = 256
Q_BLOCK = 128
POOL_WINDOWS = (2, 4, 8, 16)
POOL_WIDTH = 512
POOL_GROUP = 128
SSD_HEADS = 16
SSD_HEAD_DIM = 64
D_INNER = 1024
N_GROUPS = 2
D_STATE = 128
CONV_WIDTH = 4
CONV_DIM = D_INNER + 2 * N_GROUPS * D_STATE
CHUNK = 128
D_FF = 2816
PLE_DIM = 256
EPS = 1e-6

C_XBC = 0
C_Q = 1536
C_Z = 2048
C_GATE = 3072
C_QIDX = 6144
C_POOL = 6656
C_CKV = 7168
PROJ_COLS = 7680
SM_KIDX = 0
SM_WIDX = 64
SM_DT = 72

INT_MIN = -2147483648
LOG2E = 1.4426950408889634
RC = 32
ONES_ROWS = 16
SSD_SUB = 4
CONV_TAIL = 16
NEG_BIG = -1e30
VMEM_LIMIT = 56 * 1024 * 1024


def _cparams(sem):
    return pltpu.CompilerParams(dimension_semantics=sem, vmem_limit_bytes=VMEM_LIMIT)


def _rms(x, g):
    return x * lax.rsqrt(jnp.mean(x * x, axis=-1, keepdims=True) + EPS) * g


def _dot(a, b):
    return jnp.dot(a, b, preferred_element_type=F32)


def _sigmoid(x):
    return 0.5 * jnp.tanh(0.5 * x) + 0.5


def _silu(x):
    h = 0.5 * x
    return h * jnp.tanh(h) + h


def _split3(a):
    hi = a.astype(jnp.bfloat16)
    r1 = a - hi.astype(F32)
    mid = r1.astype(jnp.bfloat16)
    lo = (r1 - mid.astype(F32)).astype(jnp.bfloat16)
    return hi, mid, lo


def _dot01_right(a, b01):
    b = b01.astype(jnp.bfloat16)
    hi, mid, lo = _split3(a)
    return _dot(hi, b) + _dot(mid, b) + _dot(lo, b)


def _select_rows(a01, b):
    if b.dtype == jnp.bfloat16:
        return _dot(a01.astype(jnp.bfloat16), b)
    return _dot01_left(a01, b)


def _dot01_left(a01, b):
    a = a01.astype(jnp.bfloat16)
    hi, mid, lo = _split3(b)
    return _dot(a, hi) + _dot(a, mid) + _dot(a, lo)


def _ffn_kernel(h_ref, g_ref, wg_ref, wu_ref, wd_ref, o_ref):
    h = h_ref[...]
    xn = _rms(h, g_ref[...]).astype(_MXU)
    a = _silu(_dot(xn, wg_ref[...])) * _dot(xn, wu_ref[...])
    o_ref[...] = h + 0.5 * _dot(a.astype(_MXU), wd_ref[...])


def _ffn(h, g, wg, wu, wd, *, tm):
    t, d = h.shape
    ff = wg.shape[1]
    once = dict(pipeline_mode=pl.Buffered(1))
    return pl.pallas_call(
        _ffn_kernel,
        grid=(t // tm,),
        in_specs=[
            pl.BlockSpec((tm, d), lambda i: (i, 0)),
            pl.BlockSpec((1, d), lambda i: (0, 0)),
            pl.BlockSpec((d, ff), lambda i: (0, 0), **once),
            pl.BlockSpec((d, ff), lambda i: (0, 0), **once),
            pl.BlockSpec((ff, d), lambda i: (0, 0), **once),
        ],
        out_specs=pl.BlockSpec((tm, d), lambda i: (i, 0)),
        out_shape=jax.ShapeDtypeStruct((t, d), F32),
        compiler_params=_cparams(("parallel",)),
        name="ffn",
    )(h, g, wg, wu, wd)


def _inproj_kernel(h_ref, g_ref, w_ref, ws_ref, o_ref, os_ref, *, tn):
    xn = _rms(h_ref[...], g_ref[...]).astype(_MXU)
    os_ref[...] = _dot(xn, ws_ref[...])
    for j in range(w_ref.shape[1] // tn):
        o_ref[:, j * tn:(j + 1) * tn] = _dot(xn, w_ref[:, j * tn:(j + 1) * tn]).astype(o_ref.dtype)


def _inproj(h, g, w, ws, *, tm, tn):
    t, d = h.shape
    n = w.shape[1]
    once = dict(pipeline_mode=pl.Buffered(1))
    return pl.pallas_call(
        functools.partial(_inproj_kernel, tn=tn),
        grid=(t // tm,),
        in_specs=[
            pl.BlockSpec((tm, d), lambda i: (i, 0)),
            pl.BlockSpec((1, d), lambda i: (0, 0)),
            pl.BlockSpec((d, n), lambda i: (0, 0), **once),
            pl.BlockSpec((d, 128), lambda i: (0, 0), **once),
        ],
        out_specs=[pl.BlockSpec((tm, n), lambda i: (i, 0)),
                   pl.BlockSpec((tm, 128), lambda i: (i, 0))],
        out_shape=[jax.ShapeDtypeStruct((t, n), _MXU), jax.ShapeDtypeStruct((t, 128), F32)],
        compiler_params=_cparams(("parallel",)),
        name="inproj",
    )(h, g, w, ws)


def _kvprep_kernel(ckv_ref, sm_ref, gkv_ref, gk_ref, ckv_o, ckvT_o, kidx_o):
    c = _rms(ckv_ref[...].astype(F32), gkv_ref[...])
    ckv_o[0] = c.astype(ckv_o.dtype)
    ones = jnp.ones((ONES_ROWS, c.shape[0]), F32)
    ckvT_o[0, 0] = jnp.concatenate([c.T, ones], axis=0).astype(ckvT_o.dtype)
    k = sm_ref[...][:, SM_KIDX:SM_KIDX + IDX_DIM]
    kidx_o[0] = _rms(k, gk_ref[...]).astype(kidx_o.dtype)


def _kvprep(proj, small, gkv, gk, *, b, s, kb):
    nk = s // kb
    return pl.pallas_call(
        _kvprep_kernel,
        grid=(b, nk),
        in_specs=[
            pl.BlockSpec((kb, KV_RANK), lambda bi, i: (bi * nk + i, C_CKV // KV_RANK)),
            pl.BlockSpec((kb, 128), lambda bi, i: (bi * nk + i, 0)),
            pl.BlockSpec((1, KV_RANK), lambda bi, i: (0, 0)),
            pl.BlockSpec((1, IDX_DIM), lambda bi, i: (0, 0)),
        ],
        out_specs=[
            pl.BlockSpec((1, kb, KV_RANK), lambda bi, i: (bi, i, 0)),
            pl.BlockSpec((1, 1, KV_RANK + ONES_ROWS, kb), lambda bi, i: (bi, i, 0, 0)),
            pl.BlockSpec((1, kb, IDX_DIM), lambda bi, i: (bi, i, 0)),
        ],
        out_shape=[
            jax.ShapeDtypeStruct((b, s, KV_RANK), _MXU),
            jax.ShapeDtypeStruct((b, nk, KV_RANK + ONES_ROWS, kb), _MXU),
            jax.ShapeDtypeStruct((b, s, IDX_DIM), _MXU),
        ],
        compiler_params=_cparams(("parallel", "parallel")),
        name="kvprep",
    )(proj, small, gkv, gk)


def _bit_transpose32(words):
    words = list(words)
    j, m = 16, 0x0000FFFF
    while j:
        mi = jnp.int32(m - (1 << 32) if m >= (1 << 31) else m)
        for k in range(32):
            if k & j == 0:
                t = (words[k] ^ lax.shift_right_logical(words[k + j], jnp.int32(j))) & mi
                words[k] = words[k] ^ t
                words[k + j] = words[k + j] ^ lax.shift_left(t, jnp.int32(j))
        j >>= 1
        m = (m ^ (m << j)) & 0xFFFFFFFF
    return words


def _attn_kernel(q_ref, qi_ref, sm_ref, ckv_ref, ckvT_ref, kidx_ref, wuk_ref, wuvT_ref, o_ref,
                 key_ref, planes_ref, qlat_ref, s_ref, p_ref, acc_ref, m_ref, alpha_ref,
                 *, k_top, kb):
    j = pl.program_id(1)
    nblk = (j * Q_BLOCK + Q_BLOCK + kb - 1) // kb
    nh = N_HEADS
    qb = Q_BLOCK

    qT = q_ref[...].astype(F32).T
    qiT = qi_ref[...].astype(F32).T
    smT = sm_ref[...].T
    w_row = jnp.concatenate([smT[SM_WIDX + h:SM_WIDX + h + 1, :] for h in range(nh)], axis=1)
    qiT_stack = jnp.concatenate(
        [qiT[h * IDX_DIM:(h + 1) * IDX_DIM, :] for h in range(nh)], axis=1).astype(_MXU)
    scale = HEAD_DIM ** -0.5
    qlatT = jnp.concatenate(
        [_dot(wuk_ref[h], qT[h * HEAD_DIM:(h + 1) * HEAD_DIM, :].astype(_MXU)) for h in range(nh)],
        axis=1)
    qlat_ref[...] = (qlatT * (scale * LOG2E)).astype(_MXU)

    qpos = j * qb + lax.broadcasted_iota(jnp.int32, (kb, qb), 1)

    def idx_body(bi, carry):
        off = pl.multiple_of(bi * kb, kb)
        kblk = kidx_ref[0, pl.ds(off, kb), :]
        r = jnp.maximum(_dot(kblk, qiT_stack), 0.0) * w_row
        sc = r[:, 0:qb]
        for h in range(1, nh):
            sc = sc + r[:, h * qb:(h + 1) * qb]
        bits = pltpu.bitcast(sc, jnp.int32)
        key = jnp.where(bits < 0, bits ^ jnp.int32(0x7FFFFFFF), bits)
        kpos = off + lax.broadcasted_iota(jnp.int32, (kb, qb), 0)
        key = jnp.where(kpos <= qpos, key, jnp.int32(INT_MIN))
        key_ref[pl.ds(off, kb), :] = key
        u = key ^ jnp.int32(INT_MIN)
        for g in range(kb // 256):
            words = _bit_transpose32([u[g * 256 + 8 * t:g * 256 + 8 * t + 8, :] for t in range(32)])
            row = pl.multiple_of((bi * (kb // 256) + g) * 8, 8)
            for i in range(32):
                planes_ref[i, pl.ds(row, 8), :] = words[i]
        return carry

    lax.fori_loop(0, nblk, idx_body, 0)

    ngrp = planes_ref.shape[1] // 8
    grp = lax.broadcasted_iota(jnp.int32, (ngrp * 8, qb), 0) // 8
    eq0 = jnp.where(grp < nblk * (kb // 256), jnp.int32(-1), jnp.int32(0))

    def bit_body(i, carry):
        eq, n_above, prefix = carry
        x = planes_ref[i]
        ones = eq & x
        cnt = jnp.sum(lax.population_count(ones).reshape(ngrp, 8, qb), axis=0)
        cnt = jnp.sum(cnt, axis=0, keepdims=True)
        take = (n_above + cnt) >= k_top
        eq = jnp.where(take, ones, eq & ~x)
        n_above = jnp.where(take, n_above, n_above + cnt)
        prefix = jnp.where(take, prefix | jnp.left_shift(jnp.int32(1), 31 - i), prefix)
        return eq, n_above, prefix

    zrow = jnp.zeros((1, qb), jnp.int32)
    _, _, prefix = lax.fori_loop(0, 32, bit_body, (eq0, zrow, zrow))
    thr = jnp.maximum(prefix ^ jnp.int32(INT_MIN), jnp.int32(INT_MIN + 1))
    thr_b = jnp.broadcast_to(thr, (RC, qb))

    m_ref[...] = jnp.full(m_ref.shape, NEG_BIG, F32)
    alpha_ref[...] = jnp.ones_like(alpha_ref)
    acc_ref[...] = jnp.zeros_like(acc_ref)
    p_ref[1] = jnp.zeros(p_ref.shape[1:], p_ref.dtype)

    def qk(bi, slot):
        off = pl.multiple_of(jnp.minimum(bi, nblk - 1) * kb, kb)
        s_ref[slot] = _dot(ckv_ref[0, pl.ds(off, kb), :], qlat_ref[...])

    def pv(bi, slot):
        acc_ref[...] = alpha_ref[...] * acc_ref[...] + _dot(ckvT_ref[0, jnp.maximum(bi, 0)], p_ref[slot])

    def softmax(bi, slot):
        off = pl.multiple_of(bi * kb, kb)
        nch = kb // RC

        def masks(c):
            return key_ref[pl.ds(off + c * RC, RC), :] >= thr_b

        mx = [jnp.full((8, qb), NEG_BIG, F32)] * nh
        for c in range(nch):
            mk = masks(c)
            for h in range(nh):
                sc = jnp.where(mk, s_ref[slot, c * RC:(c + 1) * RC, h * qb:(h + 1) * qb], NEG_BIG)
                mx[h] = jnp.maximum(mx[h], jnp.max(sc.reshape(RC // 8, 8, qb), axis=0))
        m_new = []
        for h in range(nh):
            hs = slice(h * qb, (h + 1) * qb)
            m_old = m_ref[:, hs]
            mn = jnp.maximum(m_old, jnp.max(mx[h], axis=0, keepdims=True))
            alpha_ref[:, hs] = jnp.exp2(m_old - mn)
            m_ref[:, hs] = mn
            m_new.append(jnp.broadcast_to(mn, (RC, qb)))
        for c in range(nch):
            mk = masks(c)
            for h in range(nh):
                sc = s_ref[slot, c * RC:(c + 1) * RC, h * qb:(h + 1) * qb]
                pr = jnp.where(mk, jnp.exp2(sc - m_new[h]), 0.0)
                p_ref[slot, c * RC:(c + 1) * RC, h * qb:(h + 1) * qb] = pr.astype(p_ref.dtype)

    def step(bi, cur):
        qk(bi + 1, 1 - cur)
        pv(bi - 1, 1 - cur)
        softmax(bi, cur)

    qk(0, 0)

    def pair_body(i2, carry):
        step(2 * i2, 0)
        step(2 * i2 + 1, 1)
        return carry

    lax.fori_loop(0, nblk // 2, pair_body, 0)
    odd = nblk % 2 == 1

    @pl.when(odd)
    def _():
        step(nblk - 1, 0)
        pv(nblk - 1, 0)

    @pl.when(jnp.logical_not(odd))
    def _():
        pv(nblk - 1, 1)

    inv_l = 1.0 / acc_ref[KV_RANK:KV_RANK + 1, :]
    o = (acc_ref[:KV_RANK, :] * inv_l).astype(_MXU)
    yT = jnp.concatenate([_dot(wuvT_ref[h], o[:, h * qb:(h + 1) * qb]) for h in range(nh)], axis=0)
    o_ref[...] = yT.T


def _attention(proj, small, ckv, ckvT, kidx, wuk, wuvT, *, b, s, kb):
    nq = s // Q_BLOCK
    k_top = min(TOPK_MAX, s // 4)
    kern = functools.partial(_attn_kernel, k_top=k_top, kb=kb)
    hq = N_HEADS * Q_BLOCK
    return pl.pallas_call(
        kern,
        grid=(b, nq),
        in_specs=[
            pl.BlockSpec((Q_BLOCK, 512), lambda bi, j: (bi * nq + j, C_Q // 512)),
            pl.BlockSpec((Q_BLOCK, 512), lambda bi, j: (bi * nq + j, C_QIDX // 512)),
            pl.BlockSpec((Q_BLOCK, 128), lambda bi, j: (bi * nq + j, 0)),
            pl.BlockSpec((1, s, KV_RANK), lambda bi, j: (bi, 0, 0)),
            pl.BlockSpec((1, s // kb, KV_RANK + ONES_ROWS, kb), lambda bi, j: (bi, 0, 0, 0)),
            pl.BlockSpec((1, s, IDX_DIM), lambda bi, j: (bi, 0, 0)),
            pl.BlockSpec((N_HEADS, KV_RANK, HEAD_DIM), lambda bi, j: (0, 0, 0)),
            pl.BlockSpec((N_HEADS, HEAD_DIM, KV_RANK), lambda bi, j: (0, 0, 0)),
        ],
        out_specs=pl.BlockSpec((Q_BLOCK, N_HEADS * HEAD_DIM), lambda bi, j: (bi * nq + j, 0)),
        out_shape=jax.ShapeDtypeStruct((b * s, N_HEADS * HEAD_DIM), F32),
        scratch_shapes=[
            pltpu.VMEM((s, Q_BLOCK), jnp.int32),
            pltpu.VMEM((32, s // 32, Q_BLOCK), jnp.int32),
            pltpu.VMEM((KV_RANK, hq), _MXU),
            pltpu.VMEM((2, kb, hq), F32),
            pltpu.VMEM((2, kb, hq), _MXU),
            pltpu.VMEM((KV_RANK + ONES_ROWS, hq), F32),
            pltpu.VMEM((1, hq), F32),
            pltpu.VMEM((1, hq), F32),
        ],
        compiler_params=_cparams(("parallel", "arbitrary")),
        name="dsa_attn",
    )(proj, proj, small, ckv, ckvT, kidx, wuk, wuvT)


def _pool_kernel(x_ref, halo_ref, pw_ref, ps_ref, o_ref, *, tm):
    i = pl.program_id(1)
    x = x_ref[...].astype(F32)
    halo = jnp.where(i > 0, halo_ref[...].astype(F32), 0.0)
    xe = jnp.concatenate([halo, x], axis=0)
    pos = i * tm + lax.broadcasted_iota(jnp.int32, (tm, POOL_GROUP), 0)
    outs = []
    for g, w in enumerate(POOL_WINDOWS):
        sl = slice(g * POOL_GROUP, (g + 1) * POOL_GROUP)
        cur = xe[:, sl]
        span = 1
        while span < w:
            n = cur.shape[0]
            cur = cur[span:, :] + cur[:n - span, :]
            span *= 2
        win = cur[cur.shape[0] - tm:, :]
        count = jnp.minimum(pos + 1, w).astype(F32)
        pooled = win / count - x[:, sl]
        outs.append(_dot(pooled.astype(_MXU), pw_ref[g]))
    o_ref[...] = jnp.concatenate(outs, axis=1) * ps_ref[...]


def _pool(proj, pw, ps, *, b, s, tm):
    nt = s // tm
    kern = functools.partial(_pool_kernel, tm=tm)
    return pl.pallas_call(
        kern,
        grid=(b, nt),
        in_specs=[
            pl.BlockSpec((tm, POOL_WIDTH), lambda bi, i: (bi * nt + i, C_POOL // POOL_WIDTH)),
            pl.BlockSpec((16, POOL_WIDTH),
                         lambda bi, i: (jnp.maximum((bi * s + i * tm) // 16 - 1, 0), C_POOL // POOL_WIDTH)),
            pl.BlockSpec((len(POOL_WINDOWS), POOL_GROUP, POOL_GROUP), lambda bi, i: (0, 0, 0)),
            pl.BlockSpec((1, POOL_WIDTH), lambda bi, i: (0, 0)),
        ],
        out_specs=pl.BlockSpec((tm, POOL_WIDTH), lambda bi, i: (bi * nt + i, 0)),
        out_shape=jax.ShapeDtypeStruct((b * s, POOL_WIDTH), F32),
        compiler_params=_cparams(("parallel", "parallel")),
        name="pool",
    )(proj, proj, pw, ps)


def _ssd_kernel(xbc_ref, halo_ref, z_ref, sm_ref, cw_ref, cb_ref, dtb_row_ref, dtb_col_ref, a_row_ref, a_col_ref,
                dsk_x_ref, expand_ref, ng_ref, o_ref, state_ref):
    c = pl.program_id(1)
    L = CHUNK
    P = SSD_HEAD_DIM

    @pl.when(c == 0)
    def _():
        state_ref[...] = jnp.zeros_like(state_ref)

    halo0 = halo_ref[...]
    halo0 = jnp.where(c > 0, halo0, jnp.zeros_like(halo0))
    for sub in range(SSD_SUB):
        rows = slice(sub * L, (sub + 1) * L)
        x_nat = xbc_ref[rows, :]
        halo = halo0 if sub == 0 else xbc_ref[sub * L - CONV_TAIL:sub * L, :]
        o_ref[rows, :] = _ssd_chunk(x_nat, halo, z_ref[rows, :], sm_ref[rows, :], cw_ref, cb_ref, dtb_row_ref,
                                    dtb_col_ref, a_row_ref, a_col_ref, dsk_x_ref, expand_ref, ng_ref, state_ref)


def _ssd_chunk(x_nat, halo, z, sm, cw_ref, cb_ref, dtb_row_ref, dtb_col_ref, a_row_ref, a_col_ref,
               dsk_x_ref, expand_ref, ng_ref, state_ref):
    L = CHUNK
    P = SSD_HEAD_DIM
    xe = jnp.concatenate([halo, x_nat], axis=0)
    cw = cw_ref[...]
    conv = cb_ref[...] + cw[3:4, :] * x_nat.astype(F32)
    srow = lax.broadcasted_iota(jnp.int32, (L, L + CONV_TAIL), 0)
    scol = lax.broadcasted_iota(jnp.int32, (L, L + CONV_TAIL), 1)
    for k in range(CONV_WIDTH - 1):
        sh = CONV_WIDTH - 1 - k
        sel = jnp.where(scol == srow + (CONV_TAIL - sh), 1.0, 0.0)
        conv = conv + cw[k:k + 1, :] * _select_rows(sel, xe)
    xc = _silu(conv)
    xs = xc[:, :D_INNER]
    bm = xc[:, D_INNER:D_INNER + N_GROUPS * D_STATE]
    cm = xc[:, D_INNER + N_GROUPS * D_STATE:]

    smT = sm.T
    dt = jax.nn.softplus(sm[:, SM_DT:SM_DT + SSD_HEADS] + dtb_row_ref[...])
    dtT = jax.nn.softplus(smT[SM_DT:SM_DT + SSD_HEADS, :] + dtb_col_ref[...])
    expand = expand_ref[...]
    dtx = _dot01_right(dt, expand)
    da = dt * a_row_ref[...]
    daT = dtT * a_col_ref[...]
    row = lax.broadcasted_iota(jnp.int32, (L, L), 0)
    col = lax.broadcasted_iota(jnp.int32, (L, L), 1)
    lower = row >= col
    tril = jnp.where(lower, 1.0, 0.0).astype(F32)
    triu = jnp.where(row <= col, 1.0, 0.0).astype(F32)
    acs = _dot01_left(tril, da)
    acsT = _dot01_right(daT, triu)
    acs_x = _dot01_right(acs, expand)
    a_last = acs_x[L - 1:L, :]
    ea_x = jnp.exp(acs_x)
    decay_x = jnp.exp(a_last - acs_x)
    cd_x = jnp.exp(a_last)

    X = xs * dtx
    Xb = X.astype(_MXU)
    Xd = (X * decay_x).astype(_MXU)
    state = state_ref[...]
    ys = []
    hpg = SSD_HEADS // N_GROUPS
    for g in range(N_GROUPS):
        bg = bm[:, g * D_STATE:(g + 1) * D_STATE]
        cg = cm[:, g * D_STATE:(g + 1) * D_STATE].astype(_MXU)
        bgT = bg.T.astype(_MXU)
        cb = _dot(cg, bgT)
        gs = slice(g * hpg * P, (g + 1) * hpg * P)
        y_off = _dot(cg, state[:, gs].astype(_MXU)) * ea_x[:, gs]
        state_ref[:, gs] = state[:, gs] * cd_x[:, gs] + _dot(bgT, Xd[:, gs])
        for e in range(hpg):
            h = g * hpg + e
            hs = slice(h * P, (h + 1) * P)
            diff = acs[:, h:h + 1] - acsT[h:h + 1, :]
            lmat = jnp.exp(jnp.where(lower, diff, -jnp.inf))
            m = (cb * lmat).astype(_MXU)
            ys.append(_dot(m, Xb[:, hs]) + y_off[:, e * P:(e + 1) * P])
    y = jnp.concatenate(ys, axis=1) + xs * dsk_x_ref[...]
    y = y * _silu(z.astype(F32))
    return _rms(y, ng_ref[...])


def _ssd(proj, small, cw, cb, dtb, a_log, dsk, ng, *, b, s):
    blk = CHUNK * SSD_SUB
    nc = s // blk
    rep = lambda v: jnp.repeat(v, SSD_HEAD_DIM)[None, :]
    a = -jnp.exp(a_log)
    expand = jnp.repeat(jnp.eye(SSD_HEADS, dtype=F32), SSD_HEAD_DIM, axis=1)
    const2 = lambda bi, c: (0, 0)
    return pl.pallas_call(
        _ssd_kernel,
        grid=(b, nc),
        in_specs=[
            pl.BlockSpec((blk, CONV_DIM), lambda bi, c: (bi * nc + c, C_XBC // CONV_DIM)),
            pl.BlockSpec((CONV_TAIL, CONV_DIM),
                         lambda bi, c: (jnp.maximum((bi * nc + c) * (blk // CONV_TAIL) - 1, 0), C_XBC // CONV_DIM)),
            pl.BlockSpec((blk, D_INNER), lambda bi, c: (bi * nc + c, C_Z // D_INNER)),
            pl.BlockSpec((blk, 128), lambda bi, c: (bi * nc + c, 0)),
            pl.BlockSpec((CONV_WIDTH, CONV_DIM), const2),
            pl.BlockSpec((1, CONV_DIM), const2),
            pl.BlockSpec((1, SSD_HEADS), const2),
            pl.BlockSpec((SSD_HEADS, 1), const2),
            pl.BlockSpec((1, SSD_HEADS), const2),
            pl.BlockSpec((SSD_HEADS, 1), const2),
            pl.BlockSpec((1, D_INNER), const2),
            pl.BlockSpec((SSD_HEADS, D_INNER), const2),
            pl.BlockSpec((1, D_INNER), const2),
        ],
        out_specs=pl.BlockSpec((blk, D_INNER), lambda bi, c: (bi * nc + c, 0)),
        out_shape=jax.ShapeDtypeStruct((b * s, D_INNER), F32),
        scratch_shapes=[pltpu.VMEM((D_STATE, D_INNER), F32)],
        compiler_params=_cparams(("parallel", "arbitrary")),
        name="ssd",
    )(proj, proj, proj, small, cw, cb[None, :], dtb[None, :], dtb[:, None], a[None, :], a[:, None],
      rep(dsk), expand, ng[None, :])


def _merge_kernel(ya_ref, yp_ref, ys_ref, ga_ref, gp_ref, gs_ref, h_ref, wa_ref, wp_ref, ws_ref,
                  wo_ref, o_ref):
    m = _sigmoid(ga_ref[...].astype(F32)) * _dot(ya_ref[...].astype(_MXU), wa_ref[...])
    m = m + _sigmoid(gp_ref[...].astype(F32)) * _dot(yp_ref[...].astype(_MXU), wp_ref[...])
    m = m + _sigmoid(gs_ref[...].astype(F32)) * _dot(ys_ref[...].astype(_MXU), ws_ref[...])
    o_ref[...] = h_ref[...] + _dot(m.astype(_MXU), wo_ref[...])


def _merge(ya, yp, ys, proj, h, wa, wp, ws, wo, *, tm):
    t, d = h.shape
    row = lambda i: (i, 0)
    const = lambda i: (0, 0)
    gate = lambda k: (lambda i: (i, C_GATE // d + k))
    return pl.pallas_call(
        _merge_kernel,
        grid=(t // tm,),
        in_specs=[
            pl.BlockSpec((tm, ya.shape[1]), row),
            pl.BlockSpec((tm, yp.shape[1]), row),
            pl.BlockSpec((tm, ys.shape[1]), row),
            pl.BlockSpec((tm, d), gate(0)),
            pl.BlockSpec((tm, d), gate(1)),
            pl.BlockSpec((tm, d), gate(2)),
            pl.BlockSpec((tm, d), row),
            pl.BlockSpec(wa.shape, const),
            pl.BlockSpec(wp.shape, const),
            pl.BlockSpec(ws.shape, const),
            pl.BlockSpec(wo.shape, const),
        ],
        out_specs=pl.BlockSpec((tm, d), row),
        out_shape=jax.ShapeDtypeStruct((t, d), F32),
        compiler_params=_cparams(("parallel",)),
        name="merge",
    )(ya, yp, ys, proj, proj, proj, h, wa, wp, ws, wo)


def _ple_kernel(h_ref, p_ref, g_ref, wg_ref, wp_ref, fg_ref, o_ref, *, final):
    h = h_ref[...]
    gate = _sigmoid(_dot(_rms(h, g_ref[...]).astype(_MXU), wg_ref[...]))
    h = h + gate * _dot(p_ref[...].astype(_MXU), wp_ref[...])
    if final:
        h = _rms(h, fg_ref[...])
    o_ref[...] = h


def _ple(h, p, g, wg, wp, fg, *, tm, final):
    t, d = h.shape
    row = lambda i: (i, 0)
    const = lambda i: (0, 0)
    return pl.pallas_call(
        functools.partial(_ple_kernel, final=final),
        grid=(t // tm,),
        in_specs=[
            pl.BlockSpec((tm, d), row),
            pl.BlockSpec((tm, p.shape[1]), row),
            pl.BlockSpec((1, d), const),
            pl.BlockSpec(wg.shape, const),
            pl.BlockSpec(wp.shape, const),
            pl.BlockSpec((1, d), const),
        ],
        out_specs=pl.BlockSpec((tm, d), row),
        out_shape=jax.ShapeDtypeStruct((t, d), F32),
        compiler_params=_cparams(("parallel",)),
        name="ple",
    )(h, p, g, wg, wp, fg)


def _reorder_w_in(w):
    o_q, o_ckv, o_qidx, o_widx, o_kidx, o_pool, o_z, o_xbc, o_dt, o_gate = (
        0, 512, 768, 1280, 1288, 1352, 1864, 2888, 4424, 4440)
    d = w.shape[0]
    small = jnp.concatenate([
        w[:, o_kidx:o_kidx + IDX_DIM], w[:, o_widx:o_widx + IDX_HEADS], w[:, o_dt:o_dt + SSD_HEADS],
        jnp.zeros((d, 128 - IDX_DIM - IDX_HEADS - SSD_HEADS), w.dtype)], axis=1)
    wb = w.astype(_MXU)
    used = CONV_DIM + 512 + D_INNER + 3 * D_MODEL + 512 + POOL_WIDTH + KV_RANK
    wide = jnp.concatenate([
        wb[:, o_xbc:o_xbc + CONV_DIM], wb[:, o_q:o_q + 512], wb[:, o_z:o_z + D_INNER],
        wb[:, o_gate:o_gate + 3 * D_MODEL], wb[:, o_qidx:o_qidx + 512], wb[:, o_pool:o_pool + POOL_WIDTH],
        wb[:, o_ckv:o_ckv + KV_RANK], jnp.zeros((d, PROJ_COLS - used), _MXU)], axis=1)
    return wide, small.astype(_MXU)


def kernel(x, p, ffn1_norm, ffn1_w_gate, ffn1_w_up, ffn1_w_down, mix_norm, w_in, kv_norm, idx_k_norm, w_uk, w_uv, pool_w, pool_scale, conv_w, conv_b, dt_bias, a_log, d_skip, ssd_norm, w_br_attn, w_br_pool, w_br_ssd, w_out, ffn2_norm, ffn2_w_gate, ffn2_w_up, ffn2_w_down, ple_norm, ple_w_gate, ple_w_proj, final_norm):
    b, s, d = x.shape
    depth = w_in.shape[0]
    t = b * s
    tm = min(1024, t)
    tm_small = min(512, t)
    kb = min(512, s)
    mx = lambda a: a.astype(_MXU)

    h = x.reshape(t, d)
    for i in range(depth):
        h = _ffn(h, ffn1_norm[i][None], mx(ffn1_w_gate[i]), mx(ffn1_w_up[i]), mx(ffn1_w_down[i]),
                 tm=tm_small)
        w_wide, w_small = _reorder_w_in(w_in[i])
        proj, small = _inproj(h, mix_norm[i][None], w_wide, w_small, tm=tm_small, tn=2560)
        ckv, ckvT, kidx = _kvprep(proj, small, kv_norm[i][None], idx_k_norm[i][None], b=b, s=s, kb=kb)
        wuk = mx(jnp.transpose(w_uk[i], (1, 0, 2)))
        wuvT = mx(jnp.transpose(w_uv[i], (1, 2, 0)))
        y_attn = _attention(proj, small, ckv, ckvT, kidx, wuk, wuvT, b=b, s=s, kb=kb)
        y_pool = _pool(proj, mx(pool_w[i]), pool_scale[i][None], b=b, s=s, tm=tm_small)
        y_ssd = _ssd(proj, small, conv_w[i], conv_b[i], dt_bias[i], a_log[i], d_skip[i], ssd_norm[i], b=b, s=s)
        h = _merge(y_attn, y_pool, y_ssd, proj, h, mx(w_br_attn[i]), mx(w_br_pool[i]),
                   mx(w_br_ssd[i]), mx(w_out[i]), tm=tm_small)
        h = _ffn(h, ffn2_norm[i][None], mx(ffn2_w_gate[i]), mx(ffn2_w_up[i]), mx(ffn2_w_down[i]),
                 tm=tm_small)
        h = _ple(h, p[i].reshape(t, -1), ple_norm[i][None], mx(ple_w_gate[i]), mx(ple_w_proj[i]),
                 final_norm[None], tm=tm, final=(i == depth - 1))
    return h.reshape(b, s, d)
```

```python
import functools

import jax
import jax.numpy as jnp
from jax import lax
from jax.experimental import pallas as pl
from jax.experimental.pallas import tpu as pltpu

_MXU = jnp.bfloat16
F32 = jnp.float32

D_MODEL = 1024
N_HEADS = 8
HEAD_DIM = 64
KV_RANK = 256
IDX_HEADS = 8
IDX_DIM = 64
TOPK_MAX = 256
Q_BLOCK = 128
POOL_WINDOWS = (2, 4, 8, 16)
POOL_WIDTH = 512
POOL_GROUP = 128
SSD_HEADS = 16
SSD_HEAD_DIM = 64
D_INNER = 1024
N_GROUPS = 2
D_STATE = 128
CONV_WIDTH = 4
CONV_DIM = D_INNER + 2 * N_GROUPS * D_STATE
CHUNK = 128
D_FF = 2816
PLE_DIM = 256
EPS = 1e-6

C_XBC = 0
C_Q = 1536
C_Z = 2048
C_GATE = 3072
C_QIDX = 6144
C_POOL = 6656
C_CKV = 7168
PROJ_COLS = 7680
SM_KIDX = 0
SM_WIDX = 64
SM_DT = 72

INT_MIN = -2147483648
LOG2E = 1.4426950408889634
RC = 32
ONES_ROWS = 16
SSD_SUB = 4
CONV_TAIL = 16
NEG_BIG = -1e30
VMEM_LIMIT = 56 * 1024 * 1024


def _cparams(sem):
    return pltpu.CompilerParams(dimension_semantics=sem, vmem_limit_bytes=VMEM_LIMIT)


def _rms(x, g):
    return x * lax.rsqrt(jnp.mean(x * x, axis=-1, keepdims=True) + EPS) * g


def _dot(a, b):
    return jnp.dot(a, b, preferred_element_type=F32)


def _sigmoid(x):
    return 0.5 * jnp.tanh(0.5 * x) + 0.5


def _silu(x):
    h = 0.5 * x
    return h * jnp.tanh(h) + h


def _split3(a):
    hi = a.astype(jnp.bfloat16)
    r1 = a - hi.astype(F32)
    mid = r1.astype(jnp.bfloat16)
    lo = (r1 - mid.astype(F32)).astype(jnp.bfloat16)
    return hi, mid, lo


def _dot01_right(a, b01):
    b = b01.astype(jnp.bfloat16)
    hi, mid, lo = _split3(a)
    return _dot(hi, b) + _dot(mid, b) + _dot(lo, b)


def _select_rows(a01, b):
    if b.dtype == jnp.bfloat16:
        return _dot(a01.astype(jnp.bfloat16), b)
    return _dot01_left(a01, b)


def _dot01_left(a01, b):
    a = a01.astype(jnp.bfloat16)
    hi, mid, lo = _split3(b)
    return _dot(a, hi) + _dot(a, mid) + _dot(a, lo)


def _ffn_kernel(h_ref, g_ref, wg_ref, wu_ref, wd_ref, o_ref):
    h = h_ref[...]
    xn = _rms(h, g_ref[...]).astype(_MXU)
    a = _silu(_dot(xn, wg_ref[...])) * _dot(xn, wu_ref[...])
    o_ref[...] = h + 0.5 * _dot(a.astype(_MXU), wd_ref[...])


def _ffn(h, g, wg, wu, wd, *, tm):
    t, d = h.shape
    ff = wg.shape[1]
    once = dict(pipeline_mode=pl.Buffered(1))
    return pl.pallas_call(
        _ffn_kernel,
        grid=(t // tm,),
        in_specs=[
            pl.BlockSpec((tm, d), lambda i: (i, 0)),
            pl.BlockSpec((1, d), lambda i: (0, 0)),
            pl.BlockSpec((d, ff), lambda i: (0, 0), **once),
            pl.BlockSpec((d, ff), lambda i: (0, 0), **once),
            pl.BlockSpec((ff, d), lambda i: (0, 0), **once),
        ],
        out_specs=pl.BlockSpec((tm, d), lambda i: (i, 0)),
        out_shape=jax.ShapeDtypeStruct((t, d), F32),
        compiler_params=_cparams(("parallel",)),
        name="ffn",
    )(h, g, wg, wu, wd)


def _inproj_kernel(h_ref, g_ref, w_ref, ws_ref, o_ref, os_ref, *, tn):
    xn = _rms(h_ref[...], g_ref[...]).astype(_MXU)
    os_ref[...] = _dot(xn, ws_ref[...])
    for j in range(w_ref.shape[1] // tn):
        o_ref[:, j * tn:(j + 1) * tn] = _dot(xn, w_ref[:, j * tn:(j + 1) * tn]).astype(o_ref.dtype)


def _inproj(h, g, w, ws, *, tm, tn):
    t, d = h.shape
    n = w.shape[1]
    once = dict(pipeline_mode=pl.Buffered(1))
    return pl.pallas_call(
        functools.partial(_inproj_kernel, tn=tn),
        grid=(t // tm,),
        in_specs=[
            pl.BlockSpec((tm, d), lambda i: (i, 0)),
            pl.BlockSpec((1, d), lambda i: (0, 0)),
            pl.BlockSpec((d, n), lambda i: (0, 0), **once),
            pl.BlockSpec((d, 128), lambda i: (0, 0), **once),
        ],
        out_specs=[pl.BlockSpec((tm, n), lambda i: (i, 0)),
                   pl.BlockSpec((tm, 128), lambda i: (i, 0))],
        out_shape=[jax.ShapeDtypeStruct((t, n), _MXU), jax.ShapeDtypeStruct((t, 128), F32)],
        compiler_params=_cparams(("parallel",)),
        name="inproj",
    )(h, g, w, ws)


def _kvprep_kernel(ckv_ref, sm_ref, gkv_ref, gk_ref, ckv_o, ckvT_o, kidx_o):
    c = _rms(ckv_ref[...].astype(F32), gkv_ref[...])
    ckv_o[0] = c.astype(ckv_o.dtype)
    ones = jnp.ones((ONES_ROWS, c.shape[0]), F32)
    ckvT_o[0, 0] = jnp.concatenate([c.T, ones], axis=0).astype(ckvT_o.dtype)
    k = sm_ref[...][:, SM_KIDX:SM_KIDX + IDX_DIM]
    kidx_o[0] = _rms(k, gk_ref[...]).astype(kidx_o.dtype)


def _kvprep(proj, small, gkv, gk, *, b, s, kb):
    nk = s // kb
    return pl.pallas_call(
        _kvprep_kernel,
        grid=(b, nk),
        in_specs=[
            pl.BlockSpec((kb, KV_RANK), lambda bi, i: (bi * nk + i, C_CKV // KV_RANK)),
            pl.BlockSpec((kb, 128), lambda bi, i: (bi * nk + i, 0)),
            pl.BlockSpec((1, KV_RANK), lambda bi, i: (0, 0)),
            pl.BlockSpec((1, IDX_DIM), lambda bi, i: (0, 0)),
        ],
        out_specs=[
            pl.BlockSpec((1, kb, KV_RANK), lambda bi, i: (bi, i, 0)),
            pl.BlockSpec((1, 1, KV_RANK + ONES_ROWS, kb), lambda bi, i: (bi, i, 0, 0)),
            pl.BlockSpec((1, kb, IDX_DIM), lambda bi, i: (bi, i, 0)),
        ],
        out_shape=[
            jax.ShapeDtypeStruct((b, s, KV_RANK), _MXU),
            jax.ShapeDtypeStruct((b, nk, KV_RANK + ONES_ROWS, kb), _MXU),
            jax.ShapeDtypeStruct((b, s, IDX_DIM), _MXU),
        ],
        compiler_params=_cparams(("parallel", "parallel")),
        name="kvprep",
    )(proj, small, gkv, gk)


def _bit_transpose32(words):
    words = list(words)
    j, m = 16, 0x0000FFFF
    while j:
        mi = jnp.int32(m - (1 << 32) if m >= (1 << 31) else m)
        for k in range(32):
            if k & j == 0:
                t = (words[k] ^ lax.shift_right_logical(words[k + j], jnp.int32(j))) & mi
                words[k] = words[k] ^ t
                words[k + j] = words[k + j] ^ lax.shift_left(t, jnp.int32(j))
        j >>= 1
        m = (m ^ (m << j)) & 0xFFFFFFFF
    return words


def _attn_kernel(q_ref, qi_ref, sm_ref, ckv_ref, ckvT_ref, kidx_ref, wuk_ref, wuvT_ref, o_ref,
                 key_ref, planes_ref, qlat_ref, s_ref, p_ref, acc_ref, m_ref, alpha_ref,
                 *, k_top, kb):
    j = pl.program_id(1)
    nblk = (j * Q_BLOCK + Q_BLOCK + kb - 1) // kb
    nh = N_HEADS
    qb = Q_BLOCK

    qT = q_ref[...].astype(F32).T
    qiT = qi_ref[...].astype(F32).T
    smT = sm_ref[...].T
    w_row = jnp.concatenate([smT[SM_WIDX + h:SM_WIDX + h + 1, :] for h in range(nh)], axis=1)
    qiT_stack = jnp.concatenate(
        [qiT[h * IDX_DIM:(h + 1) * IDX_DIM, :] for h in range(nh)], axis=1).astype(_MXU)
    scale = HEAD_DIM ** -0.5
    qlatT = jnp.concatenate(
        [_dot(wuk_ref[h], qT[h * HEAD_DIM:(h + 1) * HEAD_DIM, :].astype(_MXU)) for h in range(nh)],
        axis=1)
    qlat_ref[...] = (qlatT * (scale * LOG2E)).astype(_MXU)
    s_ref[0] = _dot(ckv_ref[0, 0:kb, :], qlat_ref[...])

    qpos = j * qb + lax.broadcasted_iota(jnp.int32, (kb, qb), 1)

    def idx_body(bi, carry):
        off = pl.multiple_of(bi * kb, kb)
        kblk = kidx_ref[0, pl.ds(off, kb), :]
        r = jnp.maximum(_dot(kblk, qiT_stack), 0.0) * w_row
        sc = r[:, 0:qb]
        for h in range(1, nh):
            sc = sc + r[:, h * qb:(h + 1) * qb]
        bits = pltpu.bitcast(sc, jnp.int32)
        key = jnp.where(bits < 0, bits ^ jnp.int32(0x7FFFFFFF), bits)
        kpos = off + lax.broadcasted_iota(jnp.int32, (kb, qb), 0)
        key = jnp.where(kpos <= qpos, key, jnp.int32(INT_MIN))
        key_ref[pl.ds(off, kb), :] = key
        u = key ^ jnp.int32(INT_MIN)
        for g in range(kb // 256):
            words = _bit_transpose32([u[g * 256 + 8 * t:g * 256 + 8 * t + 8, :] for t in range(32)])
            row = pl.multiple_of((bi * (kb // 256) + g) * 8, 8)
            for i in range(32):
                planes_ref[i, pl.ds(row, 8), :] = words[i]
        return carry

    lax.fori_loop(0, nblk, idx_body, 0)

    ngrp = planes_ref.shape[1] // 8
    grp = lax.broadcasted_iota(jnp.int32, (ngrp * 8, qb), 0) // 8
    eq0 = jnp.where(grp < nblk * (kb // 256), jnp.int32(-1), jnp.int32(0))

    def count(words):
        c = jnp.sum(lax.population_count(words).reshape(ngrp, 8, qb), axis=0)
        return jnp.sum(c, axis=0, keepdims=True)

    def bit_body(i, carry):
        eq, n_above, prefix = carry
        x1 = planes_ref[2 * i]
        x0 = planes_ref[2 * i + 1]
        e1 = eq & x1
        e0 = eq & ~x1
        n1, n11, n01 = count(e1), count(e1 & x0), count(e0 & x0)
        take1 = (n_above + n1) >= k_top
        n_above = jnp.where(take1, n_above, n_above + n1)
        n_lo = jnp.where(take1, n11, n01)
        take0 = (n_above + n_lo) >= k_top
        n_above = jnp.where(take0, n_above, n_above + n_lo)
        eq = jnp.where(take1, e1, e0)
        eq = jnp.where(take0, eq & x0, eq & ~x0)
        two = jnp.where(take1, 2, 0) | jnp.where(take0, 1, 0)
        prefix = prefix | jnp.left_shift(two, 30 - 2 * i)
        return eq, n_above, prefix

    zrow = jnp.zeros((1, qb), jnp.int32)
    _, _, prefix = lax.fori_loop(0, 16, bit_body, (eq0, zrow, zrow))
    thr = jnp.maximum(prefix ^ jnp.int32(INT_MIN), jnp.int32(INT_MIN + 1))
    thr_b = jnp.broadcast_to(thr, (RC, qb))

    m_ref[...] = jnp.full(m_ref.shape, NEG_BIG, F32)
    alpha_ref[...] = jnp.ones_like(alpha_ref)
    acc_ref[...] = jnp.zeros_like(acc_ref)
    p_ref[1] = jnp.zeros(p_ref.shape[1:], p_ref.dtype)

    def qk(bi, slot):
        off = pl.multiple_of(jnp.minimum(bi, nblk - 1) * kb, kb)
        s_ref[slot] = _dot(ckv_ref[0, pl.ds(off, kb), :], qlat_ref[...])

    def pv(bi, slot):
        acc_ref[...] = alpha_ref[...] * acc_ref[...] + _dot(ckvT_ref[0, jnp.maximum(bi, 0)], p_ref[slot])

    def softmax(bi, slot):
        off = pl.multiple_of(bi * kb, kb)
        nch = kb // RC

        def masks(c):
            return key_ref[pl.ds(off + c * RC, RC), :] >= thr_b

        mx = [jnp.full((8, qb), NEG_BIG, F32)] * nh
        for c in range(nch):
            mk = masks(c)
            for h in range(nh):
                sc = jnp.where(mk, s_ref[slot, c * RC:(c + 1) * RC, h * qb:(h + 1) * qb], NEG_BIG)
                mx[h] = jnp.maximum(mx[h], jnp.max(sc.reshape(RC // 8, 8, qb), axis=0))
        m_new = []
        for h in range(nh):
            hs = slice(h * qb, (h + 1) * qb)
            m_old = m_ref[:, hs]
            mn = jnp.maximum(m_old, jnp.max(mx[h], axis=0, keepdims=True))
            alpha_ref[:, hs] = jnp.exp2(m_old - mn)
            m_ref[:, hs] = mn
            m_new.append(jnp.broadcast_to(mn, (RC, qb)))
        for c in range(nch):
            mk = masks(c)
            for h in range(nh):
                sc = s_ref[slot, c * RC:(c + 1) * RC, h * qb:(h + 1) * qb]
                pr = jnp.where(mk, jnp.exp2(sc - m_new[h]), 0.0)
                p_ref[slot, c * RC:(c + 1) * RC, h * qb:(h + 1) * qb] = pr.astype(p_ref.dtype)

    def step(bi, cur):
        qk(bi + 1, 1 - cur)
        pv(bi - 1, 1 - cur)
        softmax(bi, cur)

    def pair_body(i2, carry):
        step(2 * i2, 0)
        step(2 * i2 + 1, 1)
        return carry

    lax.fori_loop(0, nblk // 2, pair_body, 0)
    odd = nblk % 2 == 1

    @pl.when(odd)
    def _():
        pv(nblk - 2, 1)
        softmax(nblk - 1, 0)
        pv(nblk - 1, 0)

    @pl.when(jnp.logical_not(odd))
    def _():
        pv(nblk - 1, 1)

    inv_l = 1.0 / acc_ref[KV_RANK:KV_RANK + 1, :]
    o = (acc_ref[:KV_RANK, :] * inv_l).astype(_MXU)
    yT = jnp.concatenate([_dot(wuvT_ref[h], o[:, h * qb:(h + 1) * qb]) for h in range(nh)], axis=0)
    o_ref[...] = yT.T


def _attention(proj, small, ckv, ckvT, kidx, wuk, wuvT, *, b, s, kb):
    nq = s // Q_BLOCK
    k_top = min(TOPK_MAX, s // 4)
    kern = functools.partial(_attn_kernel, k_top=k_top, kb=kb)
    hq = N_HEADS * Q_BLOCK
    return pl.pallas_call(
        kern,
        grid=(b, nq),
        in_specs=[
            pl.BlockSpec((Q_BLOCK, 512), lambda bi, j: (bi * nq + j, C_Q // 512)),
            pl.BlockSpec((Q_BLOCK, 512), lambda bi, j: (bi * nq + j, C_QIDX // 512)),
            pl.BlockSpec((Q_BLOCK, 128), lambda bi, j: (bi * nq + j, 0)),
            pl.BlockSpec((1, s, KV_RANK), lambda bi, j: (bi, 0, 0)),
            pl.BlockSpec((1, s // kb, KV_RANK + ONES_ROWS, kb), lambda bi, j: (bi, 0, 0, 0)),
            pl.BlockSpec((1, s, IDX_DIM), lambda bi, j: (bi, 0, 0)),
            pl.BlockSpec((N_HEADS, KV_RANK, HEAD_DIM), lambda bi, j: (0, 0, 0)),
            pl.BlockSpec((N_HEADS, HEAD_DIM, KV_RANK), lambda bi, j: (0, 0, 0)),
        ],
        out_specs=pl.BlockSpec((Q_BLOCK, N_HEADS * HEAD_DIM), lambda bi, j: (bi * nq + j, 0)),
        out_shape=jax.ShapeDtypeStruct((b * s, N_HEADS * HEAD_DIM), F32),
        scratch_shapes=[
            pltpu.VMEM((s, Q_BLOCK), jnp.int32),
            pltpu.VMEM((32, s // 32, Q_BLOCK), jnp.int32),
            pltpu.VMEM((KV_RANK, hq), _MXU),
            pltpu.VMEM((2, kb, hq), F32),
            pltpu.VMEM((2, kb, hq), _MXU),
            pltpu.VMEM((KV_RANK + ONES_ROWS, hq), F32),
            pltpu.VMEM((1, hq), F32),
            pltpu.VMEM((1, hq), F32),
        ],
        compiler_params=_cparams(("parallel", "arbitrary")),
        name="dsa_attn",
    )(proj, proj, small, ckv, ckvT, kidx, wuk, wuvT)


def _pool_kernel(x_ref, halo_ref, pw_ref, ps_ref, o_ref, *, tm):
    i = pl.program_id(1)
    x = x_ref[...].astype(F32)
    halo = jnp.where(i > 0, halo_ref[...].astype(F32), 0.0)
    xe = jnp.concatenate([halo, x], axis=0)
    pos = i * tm + lax.broadcasted_iota(jnp.int32, (tm, POOL_GROUP), 0)
    outs = []
    for g, w in enumerate(POOL_WINDOWS):
        sl = slice(g * POOL_GROUP, (g + 1) * POOL_GROUP)
        cur = xe[:, sl]
        span = 1
        while span < w:
            n = cur.shape[0]
            cur = cur[span:, :] + cur[:n - span, :]
            span *= 2
        win = cur[cur.shape[0] - tm:, :]
        count = jnp.minimum(pos + 1, w).astype(F32)
        pooled = win / count - x[:, sl]
        outs.append(_dot(pooled.astype(_MXU), pw_ref[g]))
    o_ref[...] = jnp.concatenate(outs, axis=1) * ps_ref[...]


def _pool(proj, pw, ps, *, b, s, tm):
    nt = s // tm
    kern = functools.partial(_pool_kernel, tm=tm)
    return pl.pallas_call(
        kern,
        grid=(b, nt),
        in_specs=[
            pl.BlockSpec((tm, POOL_WIDTH), lambda bi, i: (bi * nt + i, C_POOL // POOL_WIDTH)),
            pl.BlockSpec((16, POOL_WIDTH),
                         lambda bi, i: (jnp.maximum((bi * s + i * tm) // 16 - 1, 0), C_POOL // POOL_WIDTH)),
            pl.BlockSpec((len(POOL_WINDOWS), POOL_GROUP, POOL_GROUP), lambda bi, i: (0, 0, 0)),
            pl.BlockSpec((1, POOL_WIDTH), lambda bi, i: (0, 0)),
        ],
        out_specs=pl.BlockSpec((tm, POOL_WIDTH), lambda bi, i: (bi * nt + i, 0)),
        out_shape=jax.ShapeDtypeStruct((b * s, POOL_WIDTH), F32),
        compiler_params=_cparams(("parallel", "parallel")),
        name="pool",
    )(proj, proj, pw, ps)


def _ssd_kernel(xbc_ref, halo_ref, z_ref, sm_ref, cw_ref, cb_ref, dtb_row_ref, dtb_col_ref, a_row_ref, a_col_ref,
                dsk_x_ref, expand_ref, ng_ref, o_ref, state_ref):
    c = pl.program_id(1)
    L = CHUNK
    P = SSD_HEAD_DIM

    @pl.when(c == 0)
    def _():
        state_ref[...] = jnp.zeros_like(state_ref)

    halo0 = halo_ref[...]
    halo0 = jnp.where(c > 0, halo0, jnp.zeros_like(halo0))
    for sub in range(SSD_SUB):
        rows = slice(sub * L, (sub + 1) * L)
        x_nat = xbc_ref[rows, :]
        halo = halo0 if sub == 0 else xbc_ref[sub * L - CONV_TAIL:sub * L, :]
        o_ref[rows, :] = _ssd_chunk(x_nat, halo, z_ref[rows, :], sm_ref[rows, :], cw_ref, cb_ref, dtb_row_ref,
                                    dtb_col_ref, a_row_ref, a_col_ref, dsk_x_ref, expand_ref, ng_ref, state_ref)


def _ssd_chunk(x_nat, halo, z, sm, cw_ref, cb_ref, dtb_row_ref, dtb_col_ref, a_row_ref, a_col_ref,
               dsk_x_ref, expand_ref, ng_ref, state_ref):
    L = CHUNK
    P = SSD_HEAD_DIM
    xe = jnp.concatenate([halo, x_nat], axis=0)
    cw = cw_ref[...]
    conv = cb_ref[...] + cw[3:4, :] * x_nat.astype(F32)
    srow = lax.broadcasted_iota(jnp.int32, (L, L + CONV_TAIL), 0)
    scol = lax.broadcasted_iota(jnp.int32, (L, L + CONV_TAIL), 1)
    for k in range(CONV_WIDTH - 1):
        sh = CONV_WIDTH - 1 - k
        sel = jnp.where(scol == srow + (CONV_TAIL - sh), 1.0, 0.0)
        conv = conv + cw[k:k + 1, :] * _select_rows(sel, xe)
    xc = _silu(conv)
    xs = xc[:, :D_INNER]
    bm = xc[:, D_INNER:D_INNER + N_GROUPS * D_STATE]
    cm = xc[:, D_INNER + N_GROUPS * D_STATE:]

    smT = sm.T
    dt = jax.nn.softplus(sm[:, SM_DT:SM_DT + SSD_HEADS] + dtb_row_ref[...])
    dtT = jax.nn.softplus(smT[SM_DT:SM_DT + SSD_HEADS, :] + dtb_col_ref[...])
    expand = expand_ref[...]
    dtx = _dot01_right(dt, expand)
    da = dt * a_row_ref[...]
    daT = dtT * a_col_ref[...]
    row = lax.broadcasted_iota(jnp.int32, (L, L), 0)
    col = lax.broadcasted_iota(jnp.int32, (L, L), 1)
    lower = row >= col
    tril = jnp.where(lower, 1.0, 0.0).astype(F32)
    triu = jnp.where(row <= col, 1.0, 0.0).astype(F32)
    acs = _dot01_left(tril, da)
    acsT = _dot01_right(daT, triu)
    acs_x = _dot01_right(acs, expand)
    a_last = acs_x[L - 1:L, :]
    ea_x = jnp.exp(acs_x)
    decay_x = jnp.exp(a_last - acs_x)
    cd_x = jnp.exp(a_last)

    X = xs * dtx
    Xb = X.astype(_MXU)
    Xd = (X * decay_x).astype(_MXU)
    state = state_ref[...]
    ys = []
    hpg = SSD_HEADS // N_GROUPS
    for g in range(N_GROUPS):
        bg = bm[:, g * D_STATE:(g + 1) * D_STATE]
        cg = cm[:, g * D_STATE:(g + 1) * D_STATE].astype(_MXU)
        bgT = bg.T.astype(_MXU)
        cb = _dot(cg, bgT)
        gs = slice(g * hpg * P, (g + 1) * hpg * P)
        y_off = _dot(cg, state[:, gs].astype(_MXU)) * ea_x[:, gs]
        state_ref[:, gs] = state[:, gs] * cd_x[:, gs] + _dot(bgT, Xd[:, gs])
        for e in range(hpg):
            h = g * hpg + e
            hs = slice(h * P, (h + 1) * P)
            diff = acs[:, h:h + 1] - acsT[h:h + 1, :]
            lmat = jnp.exp(jnp.where(lower, diff, -jnp.inf))
            m = (cb * lmat).astype(_MXU)
            ys.append(_dot(m, Xb[:, hs]) + y_off[:, e * P:(e + 1) * P])
    y = jnp.concatenate(ys, axis=1) + xs * dsk_x_ref[...]
    y = y * _silu(z.astype(F32))
    return _rms(y, ng_ref[...])


def _ssd(proj, small, cw, cb, dtb, a_log, dsk, ng, *, b, s):
    blk = CHUNK * SSD_SUB
    nc = s // blk
    rep = lambda v: jnp.repeat(v, SSD_HEAD_DIM)[None, :]
    a = -jnp.exp(a_log)
    expand = jnp.repeat(jnp.eye(SSD_HEADS, dtype=F32), SSD_HEAD_DIM, axis=1)
    const2 = lambda bi, c: (0, 0)
    return pl.pallas_call(
        _ssd_kernel,
        grid=(b, nc),
        in_specs=[
            pl.BlockSpec((blk, CONV_DIM), lambda bi, c: (bi * nc + c, C_XBC // CONV_DIM)),
            pl.BlockSpec((CONV_TAIL, CONV_DIM),
                         lambda bi, c: (jnp.maximum((bi * nc + c) * (blk // CONV_TAIL) - 1, 0), C_XBC // CONV_DIM)),
            pl.BlockSpec((blk, D_INNER), lambda bi, c: (bi * nc + c, C_Z // D_INNER)),
            pl.BlockSpec((blk, 128), lambda bi, c: (bi * nc + c, 0)),
            pl.BlockSpec((CONV_WIDTH, CONV_DIM), const2),
            pl.BlockSpec((1, CONV_DIM), const2),
            pl.BlockSpec((1, SSD_HEADS), const2),
            pl.BlockSpec((SSD_HEADS, 1), const2),
            pl.BlockSpec((1, SSD_HEADS), const2),
            pl.BlockSpec((SSD_HEADS, 1), const2),
            pl.BlockSpec((1, D_INNER), const2),
            pl.BlockSpec((SSD_HEADS, D_INNER), const2),
            pl.BlockSpec((1, D_INNER), const2),
        ],
        out_specs=pl.BlockSpec((blk, D_INNER), lambda bi, c: (bi * nc + c, 0)),
        out_shape=jax.ShapeDtypeStruct((b * s, D_INNER), F32),
        scratch_shapes=[pltpu.VMEM((D_STATE, D_INNER), F32)],
        compiler_params=_cparams(("parallel", "arbitrary")),
        name="ssd",
    )(proj, proj, proj, small, cw, cb[None, :], dtb[None, :], dtb[:, None], a[None, :], a[:, None],
      rep(dsk), expand, ng[None, :])


def _merge_kernel(ya_ref, yp_ref, ys_ref, ga_ref, gp_ref, gs_ref, h_ref, wa_ref, wp_ref, ws_ref,
                  wo_ref, o_ref):
    m = _sigmoid(ga_ref[...].astype(F32)) * _dot(ya_ref[...].astype(_MXU), wa_ref[...])
    m = m + _sigmoid(gp_ref[...].astype(F32)) * _dot(yp_ref[...].astype(_MXU), wp_ref[...])
    m = m + _sigmoid(gs_ref[...].astype(F32)) * _dot(ys_ref[...].astype(_MXU), ws_ref[...])
    o_ref[...] = h_ref[...] + _dot(m.astype(_MXU), wo_ref[...])


def _merge(ya, yp, ys, proj, h, wa, wp, ws, wo, *, tm):
    t, d = h.shape
    row = lambda i: (i, 0)
    const = lambda i: (0, 0)
    gate = lambda k: (lambda i: (i, C_GATE // d + k))
    return pl.pallas_call(
        _merge_kernel,
        grid=(t // tm,),
        in_specs=[
            pl.BlockSpec((tm, ya.shape[1]), row),
            pl.BlockSpec((tm, yp.shape[1]), row),
            pl.BlockSpec((tm, ys.shape[1]), row),
            pl.BlockSpec((tm, d), gate(0)),
            pl.BlockSpec((tm, d), gate(1)),
            pl.BlockSpec((tm, d), gate(2)),
            pl.BlockSpec((tm, d), row),
            pl.BlockSpec(wa.shape, const),
            pl.BlockSpec(wp.shape, const),
            pl.BlockSpec(ws.shape, const),
            pl.BlockSpec(wo.shape, const),
        ],
        out_specs=pl.BlockSpec((tm, d), row),
        out_shape=jax.ShapeDtypeStruct((t, d), F32),
        compiler_params=_cparams(("parallel",)),
        name="merge",
    )(ya, yp, ys, proj, proj, proj, h, wa, wp, ws, wo)


def _ple_kernel(h_ref, p_ref, g_ref, wg_ref, wp_ref, fg_ref, o_ref, *, final):
    h = h_ref[...]
    gate = _sigmoid(_dot(_rms(h, g_ref[...]).astype(_MXU), wg_ref[...]))
    h = h + gate * _dot(p_ref[...].astype(_MXU), wp_ref[...])
    if final:
        h = _rms(h, fg_ref[...])
    o_ref[...] = h


def _ple(h, p, g, wg, wp, fg, *, tm, final):
    t, d = h.shape
    row = lambda i: (i, 0)
    const = lambda i: (0, 0)
    return pl.pallas_call(
        functools.partial(_ple_kernel, final=final),
        grid=(t // tm,),
        in_specs=[
            pl.BlockSpec((tm, d), row),
            pl.BlockSpec((tm, p.shape[1]), row),
            pl.BlockSpec((1, d), const),
            pl.BlockSpec(wg.shape, const),
            pl.BlockSpec(wp.shape, const),
            pl.BlockSpec((1, d), const),
        ],
        out_specs=pl.BlockSpec((tm, d), row),
        out_shape=jax.ShapeDtypeStruct((t, d), F32),
        compiler_params=_cparams(("parallel",)),
        name="ple",
    )(h, p, g, wg, wp, fg)


def _reorder_w_in(w):
    o_q, o_ckv, o_qidx, o_widx, o_kidx, o_pool, o_z, o_xbc, o_dt, o_gate = (
        0, 512, 768, 1280, 1288, 1352, 1864, 2888, 4424, 4440)
    d = w.shape[0]
    small = jnp.concatenate([
        w[:, o_kidx:o_kidx + IDX_DIM], w[:, o_widx:o_widx + IDX_HEADS], w[:, o_dt:o_dt + SSD_HEADS],
        jnp.zeros((d, 128 - IDX_DIM - IDX_HEADS - SSD_HEADS), w.dtype)], axis=1)
    wb = w.astype(_MXU)
    used = CONV_DIM + 512 + D_INNER + 3 * D_MODEL + 512 + POOL_WIDTH + KV_RANK
    wide = jnp.concatenate([
        wb[:, o_xbc:o_xbc + CONV_DIM], wb[:, o_q:o_q + 512], wb[:, o_z:o_z + D_INNER],
        wb[:, o_gate:o_gate + 3 * D_MODEL], wb[:, o_qidx:o_qidx + 512], wb[:, o_pool:o_pool + POOL_WIDTH],
        wb[:, o_ckv:o_ckv + KV_RANK], jnp.zeros((d, PROJ_COLS - used), _MXU)], axis=1)
    return wide, small.astype(_MXU)


def kernel(x, p, ffn1_norm, ffn1_w_gate, ffn1_w_up, ffn1_w_down, mix_norm, w_in, kv_norm, idx_k_norm, w_uk, w_uv, pool_w, pool_scale, conv_w, conv_b, dt_bias, a_log, d_skip, ssd_norm, w_br_attn, w_br_pool, w_br_ssd, w_out, ffn2_norm, ffn2_w_gate, ffn2_w_up, ffn2_w_down, ple_norm, ple_w_gate, ple_w_proj, final_norm):
    b, s, d = x.shape
    depth = w_in.shape[0]
    t = b * s
    tm = min(1024, t)
    tm_small = min(512, t)
    kb = min(512, s)
    mx = lambda a: a.astype(_MXU)

    h = x.reshape(t, d)
    for i in range(depth):
        h = _ffn(h, ffn1_norm[i][None], mx(ffn1_w_gate[i]), mx(ffn1_w_up[i]), mx(ffn1_w_down[i]),
                 tm=tm_small)
        w_wide, w_small = _reorder_w_in(w_in[i])
        proj, small = _inproj(h, mix_norm[i][None], w_wide, w_small, tm=tm_small, tn=2560)
        ckv, ckvT, kidx = _kvprep(proj, small, kv_norm[i][None], idx_k_norm[i][None], b=b, s=s, kb=kb)
        wuk = mx(jnp.transpose(w_uk[i], (1, 0, 2)))
        wuvT = mx(jnp.transpose(w_uv[i], (1, 2, 0)))
        y_attn = _attention(proj, small, ckv, ckvT, kidx, wuk, wuvT, b=b, s=s, kb=kb)
        y_pool = _pool(proj, mx(pool_w[i]), pool_scale[i][None], b=b, s=s, tm=tm_small)
        y_ssd = _ssd(proj, small, conv_w[i], conv_b[i], dt_bias[i], a_log[i], d_skip[i], ssd_norm[i], b=b, s=s)
        h = _merge(y_attn, y_pool, y_ssd, proj, h, mx(w_br_attn[i]), mx(w_br_pool[i]),
                   mx(w_br_ssd[i]), mx(w_out[i]), tm=tm_small)
        h = _ffn(h, ffn2_norm[i][None], mx(ffn2_w_gate[i]), mx(ffn2_w_up[i]), mx(ffn2_w_down[i]),
                 tm=tm_small)
        h = _ple(h, p[i].reshape(t, -1), ple_norm[i][None], mx(ple_w_gate[i]), mx(ple_w_proj[i]),
                 final_norm[None], tm=tm, final=(i == depth - 1))
    return h.reshape(b, s, d)
```

```python
import functools

import jax
import jax.numpy as jnp
from jax import lax
from jax.experimental import pallas as pl
from jax.experimental.pallas import tpu as pltpu

_MXU = jnp.bfloat16
F32 = jnp.float32

D_MODEL = 1024
N_HEADS = 8
HEAD_DIM = 64
KV_RANK = 256
IDX_HEADS = 8
IDX_DIM = 64
TOPK_MAX = 256
Q_BLOCK = 256
POOL_WINDOWS = (2, 4, 8, 16)
POOL_WIDTH = 512
POOL_GROUP = 128
SSD_HEADS = 16
SSD_HEAD_DIM = 64
D_INNER = 1024
N_GROUPS = 2
D_STATE = 128
CONV_WIDTH = 4
CONV_DIM = D_INNER + 2 * N_GROUPS * D_STATE
CHUNK = 128
D_FF = 2816
PLE_DIM = 256
EPS = 1e-6

C_XBC = 0
C_Q = 1536
C_Z = 2048
C_GATE = 3072
C_QIDX = 6144
C_POOL = 6656
C_CKV = 7168
PROJ_COLS = 7680
SM_KIDX = 0
SM_WIDX = 64
SM_DT = 72

INT_MIN = -2147483648
LOG2E = 1.4426950408889634
RC = 32
ONES_ROWS = 16
SSD_SUB = 4
CONV_TAIL = 16
NEG_BIG = -1e30
VMEM_LIMIT = 56 * 1024 * 1024


def _cparams(sem):
    return pltpu.CompilerParams(dimension_semantics=sem, vmem_limit_bytes=VMEM_LIMIT)


def _rms(x, g):
    return x * lax.rsqrt(jnp.mean(x * x, axis=-1, keepdims=True) + EPS) * g


def _dot(a, b):
    return jnp.dot(a, b, preferred_element_type=F32)


def _sigmoid(x):
    return 0.5 * jnp.tanh(0.5 * x) + 0.5


def _silu(x):
    h = 0.5 * x
    return h * jnp.tanh(h) + h


def _split3(a):
    hi = a.astype(jnp.bfloat16)
    r1 = a - hi.astype(F32)
    mid = r1.astype(jnp.bfloat16)
    lo = (r1 - mid.astype(F32)).astype(jnp.bfloat16)
    return hi, mid, lo


def _dot01_right(a, b01):
    b = b01.astype(jnp.bfloat16)
    hi, mid, lo = _split3(a)
    return _dot(hi, b) + _dot(mid, b) + _dot(lo, b)


def _select_rows(a01, b):
    if b.dtype == jnp.bfloat16:
        return _dot(a01.astype(jnp.bfloat16), b)
    return _dot01_left(a01, b)


def _dot01_left(a01, b):
    a = a01.astype(jnp.bfloat16)
    hi, mid, lo = _split3(b)
    return _dot(a, hi) + _dot(a, mid) + _dot(a, lo)


def _ffn_kernel(h_ref, g_ref, wg_ref, wu_ref, wd_ref, o_ref):
    h = h_ref[...]
    xn = _rms(h, g_ref[...]).astype(_MXU)
    a = _silu(_dot(xn, wg_ref[...])) * _dot(xn, wu_ref[...])
    o_ref[...] = h + 0.5 * _dot(a.astype(_MXU), wd_ref[...])


def _ffn(h, g, wg, wu, wd, *, tm):
    t, d = h.shape
    ff = wg.shape[1]
    once = dict(pipeline_mode=pl.Buffered(1))
    return pl.pallas_call(
        _ffn_kernel,
        grid=(t // tm,),
        in_specs=[
            pl.BlockSpec((tm, d), lambda i: (i, 0)),
            pl.BlockSpec((1, d), lambda i: (0, 0)),
            pl.BlockSpec((d, ff), lambda i: (0, 0), **once),
            pl.BlockSpec((d, ff), lambda i: (0, 0), **once),
            pl.BlockSpec((ff, d), lambda i: (0, 0), **once),
        ],
        out_specs=pl.BlockSpec((tm, d), lambda i: (i, 0)),
        out_shape=jax.ShapeDtypeStruct((t, d), F32),
        compiler_params=_cparams(("parallel",)),
        name="ffn",
    )(h, g, wg, wu, wd)


def _inproj_kernel(h_ref, g_ref, w_ref, ws_ref, o_ref, os_ref, *, tn):
    xn = _rms(h_ref[...], g_ref[...]).astype(_MXU)
    os_ref[...] = _dot(xn, ws_ref[...])
    for j in range(w_ref.shape[1] // tn):
        o_ref[:, j * tn:(j + 1) * tn] = _dot(xn, w_ref[:, j * tn:(j + 1) * tn]).astype(o_ref.dtype)


def _inproj(h, g, w, ws, *, tm, tn):
    t, d = h.shape
    n = w.shape[1]
    once = dict(pipeline_mode=pl.Buffered(1))
    return pl.pallas_call(
        functools.partial(_inproj_kernel, tn=tn),
        grid=(t // tm,),
        in_specs=[
            pl.BlockSpec((tm, d), lambda i: (i, 0)),
            pl.BlockSpec((1, d), lambda i: (0, 0)),
            pl.BlockSpec((d, n), lambda i: (0, 0), **once),
            pl.BlockSpec((d, 128), lambda i: (0, 0), **once),
        ],
        out_specs=[pl.BlockSpec((tm, n), lambda i: (i, 0)),
                   pl.BlockSpec((tm, 128), lambda i: (i, 0))],
        out_shape=[jax.ShapeDtypeStruct((t, n), _MXU), jax.ShapeDtypeStruct((t, 128), F32)],
        compiler_params=_cparams(("parallel",)),
        name="inproj",
    )(h, g, w, ws)


def _kvprep_kernel(ckv_ref, sm_ref, gkv_ref, gk_ref, ckv_o, ckvT_o, kidx_o):
    c = _rms(ckv_ref[...].astype(F32), gkv_ref[...])
    ckv_o[0] = c.astype(ckv_o.dtype)
    ones = jnp.ones((ONES_ROWS, c.shape[0]), F32)
    ckvT_o[0, 0] = jnp.concatenate([c.T, ones], axis=0).astype(ckvT_o.dtype)
    k = sm_ref[...][:, SM_KIDX:SM_KIDX + IDX_DIM]
    kidx_o[0] = _rms(k, gk_ref[...]).astype(kidx_o.dtype)


def _kvprep(proj, small, gkv, gk, *, b, s, kb):
    nk = s // kb
    return pl.pallas_call(
        _kvprep_kernel,
        grid=(b, nk),
        in_specs=[
            pl.BlockSpec((kb, KV_RANK), lambda bi, i: (bi * nk + i, C_CKV // KV_RANK)),
            pl.BlockSpec((kb, 128), lambda bi, i: (bi * nk + i, 0)),
            pl.BlockSpec((1, KV_RANK), lambda bi, i: (0, 0)),
            pl.BlockSpec((1, IDX_DIM), lambda bi, i: (0, 0)),
        ],
        out_specs=[
            pl.BlockSpec((1, kb, KV_RANK), lambda bi, i: (bi, i, 0)),
            pl.BlockSpec((1, 1, KV_RANK + ONES_ROWS, kb), lambda bi, i: (bi, i, 0, 0)),
            pl.BlockSpec((1, kb, IDX_DIM), lambda bi, i: (bi, i, 0)),
        ],
        out_shape=[
            jax.ShapeDtypeStruct((b, s, KV_RANK), _MXU),
            jax.ShapeDtypeStruct((b, nk, KV_RANK + ONES_ROWS, kb), _MXU),
            jax.ShapeDtypeStruct((b, s, IDX_DIM), _MXU),
        ],
        compiler_params=_cparams(("parallel", "parallel")),
        name="kvprep",
    )(proj, small, gkv, gk)


def _bit_transpose32(words):
    words = list(words)
    j, m = 16, 0x0000FFFF
    while j:
        mi = jnp.int32(m - (1 << 32) if m >= (1 << 31) else m)
        for k in range(32):
            if k & j == 0:
                t = (words[k] ^ lax.shift_right_logical(words[k + j], jnp.int32(j))) & mi
                words[k] = words[k] ^ t
                words[k + j] = words[k + j] ^ lax.shift_left(t, jnp.int32(j))
        j >>= 1
        m = (m ^ (m << j)) & 0xFFFFFFFF
    return words


def _attn_kernel(q_ref, qi_ref, sm_ref, ckv_ref, ckvT_ref, kidx_ref, wuk_ref, wuvT_ref, o_ref,
                 key_ref, planes_ref, qlat_ref, s_ref, p_ref, acc_ref, m_ref, alpha_ref,
                 *, k_top, kb):
    j = pl.program_id(1)
    nblk = (j * Q_BLOCK + Q_BLOCK + kb - 1) // kb
    nh = N_HEADS
    qb = Q_BLOCK

    qT = q_ref[...].astype(F32).T
    qiT = qi_ref[...].astype(F32).T
    smT = sm_ref[...].T
    w_row = jnp.concatenate([smT[SM_WIDX + h:SM_WIDX + h + 1, :] for h in range(nh)], axis=1)
    qiT_stack = jnp.concatenate(
        [qiT[h * IDX_DIM:(h + 1) * IDX_DIM, :] for h in range(nh)], axis=1).astype(_MXU)
    scale = HEAD_DIM ** -0.5
    qlatT = jnp.concatenate(
        [_dot(wuk_ref[h], qT[h * HEAD_DIM:(h + 1) * HEAD_DIM, :].astype(_MXU)) for h in range(nh)],
        axis=1)
    qlat_ref[...] = (qlatT * (scale * LOG2E)).astype(_MXU)
    s_ref[0] = _dot(ckv_ref[0, 0:kb, :], qlat_ref[...])

    qpos = j * qb + lax.broadcasted_iota(jnp.int32, (kb, qb), 1)

    def idx_body(bi, carry):
        off = pl.multiple_of(bi * kb, kb)
        kblk = kidx_ref[0, pl.ds(off, kb), :]
        r = jnp.maximum(_dot(kblk, qiT_stack), 0.0) * w_row
        sc = r[:, 0:qb]
        for h in range(1, nh):
            sc = sc + r[:, h * qb:(h + 1) * qb]
        bits = pltpu.bitcast(sc, jnp.int32)
        key = jnp.where(bits < 0, bits ^ jnp.int32(0x7FFFFFFF), bits)
        kpos = off + lax.broadcasted_iota(jnp.int32, (kb, qb), 0)
        key = jnp.where(kpos <= qpos, key, jnp.int32(INT_MIN))
        key_ref[pl.ds(off, kb), :] = key
        u = key ^ jnp.int32(INT_MIN)
        for g in range(kb // 256):
            words = _bit_transpose32([u[g * 256 + 8 * t:g * 256 + 8 * t + 8, :] for t in range(32)])
            row = pl.multiple_of((bi * (kb // 256) + g) * 8, 8)
            for i in range(32):
                planes_ref[i, pl.ds(row, 8), :] = words[i]
        return carry

    lax.fori_loop(0, nblk, idx_body, 0)

    ngrp = planes_ref.shape[1] // 8
    grp = lax.broadcasted_iota(jnp.int32, (ngrp * 8, qb), 0) // 8
    eq0 = jnp.where(grp < nblk * (kb // 256), jnp.int32(-1), jnp.int32(0))

    def count(words):
        c = jnp.sum(lax.population_count(words).reshape(ngrp, 8, qb), axis=0)
        return jnp.sum(c, axis=0, keepdims=True)

    def bit_body(i, carry):
        eq, n_above, prefix = carry
        x1 = planes_ref[2 * i]
        x0 = planes_ref[2 * i + 1]
        e1 = eq & x1
        e0 = eq & ~x1
        n1, n11, n01 = count(e1), count(e1 & x0), count(e0 & x0)
        take1 = (n_above + n1) >= k_top
        n_above = jnp.where(take1, n_above, n_above + n1)
        n_lo = jnp.where(take1, n11, n01)
        take0 = (n_above + n_lo) >= k_top
        n_above = jnp.where(take0, n_above, n_above + n_lo)
        eq = jnp.where(take1, e1, e0)
        eq = jnp.where(take0, eq & x0, eq & ~x0)
        two = jnp.where(take1, 2, 0) | jnp.where(take0, 1, 0)
        prefix = prefix | jnp.left_shift(two, 30 - 2 * i)
        return eq, n_above, prefix

    zrow = jnp.zeros((1, qb), jnp.int32)
    _, _, prefix = lax.fori_loop(0, 16, bit_body, (eq0, zrow, zrow))
    thr = jnp.maximum(prefix ^ jnp.int32(INT_MIN), jnp.int32(INT_MIN + 1))
    thr_b = jnp.broadcast_to(thr, (RC, qb))

    m_ref[...] = jnp.full(m_ref.shape, NEG_BIG, F32)
    alpha_ref[...] = jnp.ones_like(alpha_ref)
    acc_ref[...] = jnp.zeros_like(acc_ref)
    p_ref[1] = jnp.zeros(p_ref.shape[1:], p_ref.dtype)

    def qk(bi, slot):
        off = pl.multiple_of(jnp.minimum(bi, nblk - 1) * kb, kb)
        s_ref[slot] = _dot(ckv_ref[0, pl.ds(off, kb), :], qlat_ref[...])

    def pv(bi, slot):
        acc_ref[...] = alpha_ref[...] * acc_ref[...] + _dot(ckvT_ref[0, jnp.maximum(bi, 0)], p_ref[slot])

    def softmax(bi, slot):
        off = pl.multiple_of(bi * kb, kb)
        nch = kb // RC

        def masks(c):
            return key_ref[pl.ds(off + c * RC, RC), :] >= thr_b

        mx = [jnp.full((8, qb), NEG_BIG, F32)] * nh
        for c in range(nch):
            mk = masks(c)
            for h in range(nh):
                sc = jnp.where(mk, s_ref[slot, c * RC:(c + 1) * RC, h * qb:(h + 1) * qb], NEG_BIG)
                mx[h] = jnp.maximum(mx[h], jnp.max(sc.reshape(RC // 8, 8, qb), axis=0))
        m_new = []
        for h in range(nh):
            hs = slice(h * qb, (h + 1) * qb)
            m_old = m_ref[:, hs]
            mn = jnp.maximum(m_old, jnp.max(mx[h], axis=0, keepdims=True))
            alpha_ref[:, hs] = jnp.exp2(m_old - mn)
            m_ref[:, hs] = mn
            m_new.append(jnp.broadcast_to(mn, (RC, qb)))
        for c in range(nch):
            mk = masks(c)
            for h in range(nh):
                sc = s_ref[slot, c * RC:(c + 1) * RC, h * qb:(h + 1) * qb]
                pr = jnp.where(mk, jnp.exp2(sc - m_new[h]), 0.0)
                p_ref[slot, c * RC:(c + 1) * RC, h * qb:(h + 1) * qb] = pr.astype(p_ref.dtype)

    def step(bi, cur):
        qk(bi + 1, 1 - cur)
        pv(bi - 1, 1 - cur)
        softmax(bi, cur)

    def pair_body(i2, carry):
        step(2 * i2, 0)
        step(2 * i2 + 1, 1)
        return carry

    lax.fori_loop(0, nblk // 2, pair_body, 0)
    odd = nblk % 2 == 1

    @pl.when(odd)
    def _():
        pv(nblk - 2, 1)
        softmax(nblk - 1, 0)
        pv(nblk - 1, 0)

    @pl.when(jnp.logical_not(odd))
    def _():
        pv(nblk - 1, 1)

    inv_l = 1.0 / acc_ref[KV_RANK:KV_RANK + 1, :]
    o = (acc_ref[:KV_RANK, :] * inv_l).astype(_MXU)
    yT = jnp.concatenate([_dot(wuvT_ref[h], o[:, h * qb:(h + 1) * qb]) for h in range(nh)], axis=0)
    o_ref[...] = yT.T


def _attention(proj, small, ckv, ckvT, kidx, wuk, wuvT, *, b, s, kb):
    nq = s // Q_BLOCK
    k_top = min(TOPK_MAX, s // 4)
    kern = functools.partial(_attn_kernel, k_top=k_top, kb=kb)
    hq = N_HEADS * Q_BLOCK
    return pl.pallas_call(
        kern,
        grid=(b, nq),
        in_specs=[
            pl.BlockSpec((Q_BLOCK, 512), lambda bi, j: (bi * nq + j, C_Q // 512)),
            pl.BlockSpec((Q_BLOCK, 512), lambda bi, j: (bi * nq + j, C_QIDX // 512)),
            pl.BlockSpec((Q_BLOCK, 128), lambda bi, j: (bi * nq + j, 0)),
            pl.BlockSpec((1, s, KV_RANK), lambda bi, j: (bi, 0, 0)),
            pl.BlockSpec((1, s // kb, KV_RANK + ONES_ROWS, kb), lambda bi, j: (bi, 0, 0, 0)),
            pl.BlockSpec((1, s, IDX_DIM), lambda bi, j: (bi, 0, 0)),
            pl.BlockSpec((N_HEADS, KV_RANK, HEAD_DIM), lambda bi, j: (0, 0, 0)),
            pl.BlockSpec((N_HEADS, HEAD_DIM, KV_RANK), lambda bi, j: (0, 0, 0)),
        ],
        out_specs=pl.BlockSpec((Q_BLOCK, N_HEADS * HEAD_DIM), lambda bi, j: (bi * nq + j, 0)),
        out_shape=jax.ShapeDtypeStruct((b * s, N_HEADS * HEAD_DIM), F32),
        scratch_shapes=[
            pltpu.VMEM((s, Q_BLOCK), jnp.int32),
            pltpu.VMEM((32, s // 32, Q_BLOCK), jnp.int32),
            pltpu.VMEM((KV_RANK, hq), _MXU),
            pltpu.VMEM((2, kb, hq), F32),
            pltpu.VMEM((2, kb, hq), _MXU),
            pltpu.VMEM((KV_RANK + ONES_ROWS, hq), F32),
            pltpu.VMEM((1, hq), F32),
            pltpu.VMEM((1, hq), F32),
        ],
        compiler_params=_cparams(("parallel", "arbitrary")),
        name="dsa_attn",
    )(proj, proj, small, ckv, ckvT, kidx, wuk, wuvT)


def _pool_kernel(x_ref, halo_ref, pw_ref, ps_ref, o_ref, *, tm):
    i = pl.program_id(1)
    x = x_ref[...].astype(F32)
    halo = jnp.where(i > 0, halo_ref[...].astype(F32), 0.0)
    xe = jnp.concatenate([halo, x], axis=0)
    pos = i * tm + lax.broadcasted_iota(jnp.int32, (tm, POOL_GROUP), 0)
    outs = []
    for g, w in enumerate(POOL_WINDOWS):
        sl = slice(g * POOL_GROUP, (g + 1) * POOL_GROUP)
        cur = xe[:, sl]
        span = 1
        while span < w:
            n = cur.shape[0]
            cur = cur[span:, :] + cur[:n - span, :]
            span *= 2
        win = cur[cur.shape[0] - tm:, :]
        count = jnp.minimum(pos + 1, w).astype(F32)
        pooled = win / count - x[:, sl]
        outs.append(_dot(pooled.astype(_MXU), pw_ref[g]))
    o_ref[...] = jnp.concatenate(outs, axis=1) * ps_ref[...]


def _pool(proj, pw, ps, *, b, s, tm):
    nt = s // tm
    kern = functools.partial(_pool_kernel, tm=tm)
    return pl.pallas_call(
        kern,
        grid=(b, nt),
        in_specs=[
            pl.BlockSpec((tm, POOL_WIDTH), lambda bi, i: (bi * nt + i, C_POOL // POOL_WIDTH)),
            pl.BlockSpec((16, POOL_WIDTH),
                         lambda bi, i: (jnp.maximum((bi * s + i * tm) // 16 - 1, 0), C_POOL // POOL_WIDTH)),
            pl.BlockSpec((len(POOL_WINDOWS), POOL_GROUP, POOL_GROUP), lambda bi, i: (0, 0, 0)),
            pl.BlockSpec((1, POOL_WIDTH), lambda bi, i: (0, 0)),
        ],
        out_specs=pl.BlockSpec((tm, POOL_WIDTH), lambda bi, i: (bi * nt + i, 0)),
        out_shape=jax.ShapeDtypeStruct((b * s, POOL_WIDTH), F32),
        compiler_params=_cparams(("parallel", "parallel")),
        name="pool",
    )(proj, proj, pw, ps)


def _ssd_kernel(xbc_ref, halo_ref, z_ref, sm_ref, cw_ref, cb_ref, dtb_row_ref, dtb_col_ref, a_row_ref, a_col_ref,
                dsk_x_ref, expand_ref, ng_ref, o_ref, state_ref):
    c = pl.program_id(1)
    L = CHUNK
    P = SSD_HEAD_DIM

    @pl.when(c == 0)
    def _():
        state_ref[...] = jnp.zeros_like(state_ref)

    halo0 = halo_ref[...]
    halo0 = jnp.where(c > 0, halo0, jnp.zeros_like(halo0))
    for sub in range(SSD_SUB):
        rows = slice(sub * L, (sub + 1) * L)
        x_nat = xbc_ref[rows, :]
        halo = halo0 if sub == 0 else xbc_ref[sub * L - CONV_TAIL:sub * L, :]
        o_ref[rows, :] = _ssd_chunk(x_nat, halo, z_ref[rows, :], sm_ref[rows, :], cw_ref, cb_ref, dtb_row_ref,
                                    dtb_col_ref, a_row_ref, a_col_ref, dsk_x_ref, expand_ref, ng_ref, state_ref)


def _ssd_chunk(x_nat, halo, z, sm, cw_ref, cb_ref, dtb_row_ref, dtb_col_ref, a_row_ref, a_col_ref,
               dsk_x_ref, expand_ref, ng_ref, state_ref):
    L = CHUNK
    P = SSD_HEAD_DIM
    xe = jnp.concatenate([halo, x_nat], axis=0)
    cw = cw_ref[...]
    conv = cb_ref[...] + cw[3:4, :] * x_nat.astype(F32)
    srow = lax.broadcasted_iota(jnp.int32, (L, L + CONV_TAIL), 0)
    scol = lax.broadcasted_iota(jnp.int32, (L, L + CONV_TAIL), 1)
    for k in range(CONV_WIDTH - 1):
        sh = CONV_WIDTH - 1 - k
        sel = jnp.where(scol == srow + (CONV_TAIL - sh), 1.0, 0.0)
        conv = conv + cw[k:k + 1, :] * _select_rows(sel, xe)
    xc = _silu(conv)
    xs = xc[:, :D_INNER]
    bm = xc[:, D_INNER:D_INNER + N_GROUPS * D_STATE]
    cm = xc[:, D_INNER + N_GROUPS * D_STATE:]

    smT = sm.T
    dt = jax.nn.softplus(sm[:, SM_DT:SM_DT + SSD_HEADS] + dtb_row_ref[...])
    dtT = jax.nn.softplus(smT[SM_DT:SM_DT + SSD_HEADS, :] + dtb_col_ref[...])
    expand = expand_ref[...]
    dtx = _dot01_right(dt, expand)
    da = dt * a_row_ref[...]
    daT = dtT * a_col_ref[...]
    row = lax.broadcasted_iota(jnp.int32, (L, L), 0)
    col = lax.broadcasted_iota(jnp.int32, (L, L), 1)
    lower = row >= col
    tril = jnp.where(lower, 1.0, 0.0).astype(F32)
    triu = jnp.where(row <= col, 1.0, 0.0).astype(F32)
    acs = _dot01_left(tril, da)
    acsT = _dot01_right(daT, triu)
    acs_x = _dot01_right(acs, expand)
    a_last = acs_x[L - 1:L, :]
    ea_x = jnp.exp(acs_x)
    decay_x = jnp.exp(a_last - acs_x)
    cd_x = jnp.exp(a_last)

    X = xs * dtx
    Xb = X.astype(_MXU)
    Xd = (X * decay_x).astype(_MXU)
    state = state_ref[...]
    ys = []
    hpg = SSD_HEADS // N_GROUPS
    for g in range(N_GROUPS):
        bg = bm[:, g * D_STATE:(g + 1) * D_STATE]
        cg = cm[:, g * D_STATE:(g + 1) * D_STATE].astype(_MXU)
        bgT = bg.T.astype(_MXU)
        cb = _dot(cg, bgT)
        gs = slice(g * hpg * P, (g + 1) * hpg * P)
        y_off = _dot(cg, state[:, gs].astype(_MXU)) * ea_x[:, gs]
        state_ref[:, gs] = state[:, gs] * cd_x[:, gs] + _dot(bgT, Xd[:, gs])
        for e in range(hpg):
            h = g * hpg + e
            hs = slice(h * P, (h + 1) * P)
            diff = acs[:, h:h + 1] - acsT[h:h + 1, :]
            lmat = jnp.exp(jnp.where(lower, diff, -jnp.inf))
            m = (cb * lmat).astype(_MXU)
            ys.append(_dot(m, Xb[:, hs]) + y_off[:, e * P:(e + 1) * P])
    y = jnp.concatenate(ys, axis=1) + xs * dsk_x_ref[...]
    y = y * _silu(z.astype(F32))
    return _rms(y, ng_ref[...])


def _ssd(proj, small, cw, cb, dtb, a_log, dsk, ng, *, b, s):
    blk = CHUNK * SSD_SUB
    nc = s // blk
    rep = lambda v: jnp.repeat(v, SSD_HEAD_DIM)[None, :]
    a = -jnp.exp(a_log)
    expand = jnp.repeat(jnp.eye(SSD_HEADS, dtype=F32), SSD_HEAD_DIM, axis=1)
    const2 = lambda bi, c: (0, 0)
    return pl.pallas_call(
        _ssd_kernel,
        grid=(b, nc),
        in_specs=[
            pl.BlockSpec((blk, CONV_DIM), lambda bi, c: (bi * nc + c, C_XBC // CONV_DIM)),
            pl.BlockSpec((CONV_TAIL, CONV_DIM),
                         lambda bi, c: (jnp.maximum((bi * nc + c) * (blk // CONV_TAIL) - 1, 0), C_XBC // CONV_DIM)),
            pl.BlockSpec((blk, D_INNER), lambda bi, c: (bi * nc + c, C_Z // D_INNER)),
            pl.BlockSpec((blk, 128), lambda bi, c: (bi * nc + c, 0)),
            pl.BlockSpec((CONV_WIDTH, CONV_DIM), const2),
            pl.BlockSpec((1, CONV_DIM), const2),
            pl.BlockSpec((1, SSD_HEADS), const2),
            pl.BlockSpec((SSD_HEADS, 1), const2),
            pl.BlockSpec((1, SSD_HEADS), const2),
            pl.BlockSpec((SSD_HEADS, 1), const2),
            pl.BlockSpec((1, D_INNER), const2),
            pl.BlockSpec((SSD_HEADS, D_INNER), const2),
            pl.BlockSpec((1, D_INNER), const2),
        ],
        out_specs=pl.BlockSpec((blk, D_INNER), lambda bi, c: (bi * nc + c, 0)),
        out_shape=jax.ShapeDtypeStruct((b * s, D_INNER), F32),
        scratch_shapes=[pltpu.VMEM((D_STATE, D_INNER), F32)],
        compiler_params=_cparams(("parallel", "arbitrary")),
        name="ssd",
    )(proj, proj, proj, small, cw, cb[None, :], dtb[None, :], dtb[:, None], a[None, :], a[:, None],
      rep(dsk), expand, ng[None, :])


def _merge_kernel(ya_ref, yp_ref, ys_ref, ga_ref, gp_ref, gs_ref, h_ref, wa_ref, wp_ref, ws_ref,
                  wo_ref, o_ref):
    m = _sigmoid(ga_ref[...].astype(F32)) * _dot(ya_ref[...].astype(_MXU), wa_ref[...])
    m = m + _sigmoid(gp_ref[...].astype(F32)) * _dot(yp_ref[...].astype(_MXU), wp_ref[...])
    m = m + _sigmoid(gs_ref[...].astype(F32)) * _dot(ys_ref[...].astype(_MXU), ws_ref[...])
    o_ref[...] = h_ref[...] + _dot(m.astype(_MXU), wo_ref[...])


def _merge(ya, yp, ys, proj, h, wa, wp, ws, wo, *, tm):
    t, d = h.shape
    row = lambda i: (i, 0)
    const = lambda i: (0, 0)
    gate = lambda k: (lambda i: (i, C_GATE // d + k))
    return pl.pallas_call(
        _merge_kernel,
        grid=(t // tm,),
        in_specs=[
            pl.BlockSpec((tm, ya.shape[1]), row),
            pl.BlockSpec((tm, yp.shape[1]), row),
            pl.BlockSpec((tm, ys.shape[1]), row),
            pl.BlockSpec((tm, d), gate(0)),
            pl.BlockSpec((tm, d), gate(1)),
            pl.BlockSpec((tm, d), gate(2)),
            pl.BlockSpec((tm, d), row),
            pl.BlockSpec(wa.shape, const),
            pl.BlockSpec(wp.shape, const),
            pl.BlockSpec(ws.shape, const),
            pl.BlockSpec(wo.shape, const),
        ],
        out_specs=pl.BlockSpec((tm, d), row),
        out_shape=jax.ShapeDtypeStruct((t, d), F32),
        compiler_params=_cparams(("parallel",)),
        name="merge",
    )(ya, yp, ys, proj, proj, proj, h, wa, wp, ws, wo)


def _ple_kernel(h_ref, p_ref, g_ref, wg_ref, wp_ref, fg_ref, o_ref, *, final):
    h = h_ref[...]
    gate = _sigmoid(_dot(_rms(h, g_ref[...]).astype(_MXU), wg_ref[...]))
    h = h + gate * _dot(p_ref[...].astype(_MXU), wp_ref[...])
    if final:
        h = _rms(h, fg_ref[...])
    o_ref[...] = h


def _ple(h, p, g, wg, wp, fg, *, tm, final):
    t, d = h.shape
    row = lambda i: (i, 0)
    const = lambda i: (0, 0)
    return pl.pallas_call(
        functools.partial(_ple_kernel, final=final),
        grid=(t // tm,),
        in_specs=[
            pl.BlockSpec((tm, d), row),
            pl.BlockSpec((tm, p.shape[1]), row),
            pl.BlockSpec((1, d), const),
            pl.BlockSpec(wg.shape, const),
            pl.BlockSpec(wp.shape, const),
            pl.BlockSpec((1, d), const),
        ],
        out_specs=pl.BlockSpec((tm, d), row),
        out_shape=jax.ShapeDtypeStruct((t, d), F32),
        compiler_params=_cparams(("parallel",)),
        name="ple",
    )(h, p, g, wg, wp, fg)


def _reorder_w_in(w):
    o_q, o_ckv, o_qidx, o_widx, o_kidx, o_pool, o_z, o_xbc, o_dt, o_gate = (
        0, 512, 768, 1280, 1288, 1352, 1864, 2888, 4424, 4440)
    d = w.shape[0]
    small = jnp.concatenate([
        w[:, o_kidx:o_kidx + IDX_DIM], w[:, o_widx:o_widx + IDX_HEADS], w[:, o_dt:o_dt + SSD_HEADS],
        jnp.zeros((d, 128 - IDX_DIM - IDX_HEADS - SSD_HEADS), w.dtype)], axis=1)
    wb = w.astype(_MXU)
    used = CONV_DIM + 512 + D_INNER + 3 * D_MODEL + 512 + POOL_WIDTH + KV_RANK
    wide = jnp.concatenate([
        wb[:, o_xbc:o_xbc + CONV_DIM], wb[:, o_q:o_q + 512], wb[:, o_z:o_z + D_INNER],
        wb[:, o_gate:o_gate + 3 * D_MODEL], wb[:, o_qidx:o_qidx + 512], wb[:, o_pool:o_pool + POOL_WIDTH],
        wb[:, o_ckv:o_ckv + KV_RANK], jnp.zeros((d, PROJ_COLS - used), _MXU)], axis=1)
    return wide, small.astype(_MXU)


def kernel(x, p, ffn1_norm, ffn1_w_gate, ffn1_w_up, ffn1_w_down, mix_norm, w_in, kv_norm, idx_k_norm, w_uk, w_uv, pool_w, pool_scale, conv_w, conv_b, dt_bias, a_log, d_skip, ssd_norm, w_br_attn, w_br_pool, w_br_ssd, w_out, ffn2_norm, ffn2_w_gate, ffn2_w_up, ffn2_w_down, ple_norm, ple_w_gate, ple_w_proj, final_norm):
    b, s, d = x.shape
    depth = w_in.shape[0]
    t = b * s
    tm = min(1024, t)
    tm_small = min(512, t)
    kb = min(512, s)
    mx = lambda a: a.astype(_MXU)

    h = x.reshape(t, d)
    for i in range(depth):
        h = _ffn(h, ffn1_norm[i][None], mx(ffn1_w_gate[i]), mx(ffn1_w_up[i]), mx(ffn1_w_down[i]),
                 tm=tm_small)
        w_wide, w_small = _reorder_w_in(w_in[i])
        proj, small = _inproj(h, mix_norm[i][None], w_wide, w_small, tm=tm_small, tn=2560)
        ckv, ckvT, kidx = _kvprep(proj, small, kv_norm[i][None], idx_k_norm[i][None], b=b, s=s, kb=kb)
        wuk = mx(jnp.transpose(w_uk[i], (1, 0, 2)))
        wuvT = mx(jnp.transpose(w_uv[i], (1, 2, 0)))
        y_attn = _attention(proj, small, ckv, ckvT, kidx, wuk, wuvT, b=b, s=s, kb=kb)
        y_pool = _pool(proj, mx(pool_w[i]), pool_scale[i][None], b=b, s=s, tm=tm_small)
        y_ssd = _ssd(proj, small, conv_w[i], conv_b[i], dt_bias[i], a_log[i], d_skip[i], ssd_norm[i], b=b, s=s)
        h = _merge(y_attn, y_pool, y_ssd, proj, h, mx(w_br_attn[i]), mx(w_br_pool[i]),
                   mx(w_br_ssd[i]), mx(w_out[i]), tm=tm_small)
        h = _ffn(h, ffn2_norm[i][None], mx(ffn2_w_gate[i]), mx(ffn2_w_up[i]), mx(ffn2_w_down[i]),
                 tm=tm_small)
        h = _ple(h, p[i].reshape(t, -1), ple_norm[i][None], mx(ple_w_gate[i]), mx(ple_w_proj[i]),
                 final_norm[None], tm=tm, final=(i == depth - 1))
    return h.reshape(b, s, d)
```

```python
import functools

import jax
import jax.numpy as jnp
from jax import lax
from jax.experimental import pallas as pl
from jax.experimental.pallas import tpu as pltpu

_MXU = jnp.bfloat16
F32 = jnp.float32

D_MODEL = 1024
N_HEADS = 8
HEAD_DIM = 64
KV_RANK = 256
IDX_HEADS = 8
IDX_DIM = 64
TOPK_MAX = 256
Q_BLOCK = 256
POOL_WINDOWS = (2, 4, 8, 16)
POOL_WIDTH = 512
POOL_GROUP = 128
SSD_HEADS = 16
SSD_HEAD_DIM = 64
D_INNER = 1024
N_GROUPS = 2
D_STATE = 128
CONV_WIDTH = 4
CONV_DIM = D_INNER + 2 * N_GROUPS * D_STATE
CHUNK = 128
D_FF = 2816
PLE_DIM = 256
EPS = 1e-6

C_XBC = 0
C_Q = 1536
C_Z = 2048
C_GATE = 3072
C_QIDX = 6144
C_POOL = 6656
C_CKV = 7168
PROJ_COLS = 7680
SM_KIDX = 0
SM_WIDX = 64
SM_DT = 72

INT_MIN = -2147483648
LOG2E = 1.4426950408889634
RC = 32
ONES_ROWS = 16
SSD_SUB = 4
CONV_TAIL = 16
NEG_BIG = -1e30
VMEM_LIMIT = 56 * 1024 * 1024


def _cparams(sem):
    return pltpu.CompilerParams(dimension_semantics=sem, vmem_limit_bytes=VMEM_LIMIT)


def _rms(x, g):
    return x * lax.rsqrt(jnp.mean(x * x, axis=-1, keepdims=True) + EPS) * g


def _dot(a, b):
    return jnp.dot(a, b, preferred_element_type=F32)


def _sigmoid(x):
    return 0.5 * jnp.tanh(0.5 * x) + 0.5


def _silu(x):
    h = 0.5 * x
    return h * jnp.tanh(h) + h


def _split3(a):
    hi = a.astype(jnp.bfloat16)
    r1 = a - hi.astype(F32)
    mid = r1.astype(jnp.bfloat16)
    lo = (r1 - mid.astype(F32)).astype(jnp.bfloat16)
    return hi, mid, lo


def _dot01_right(a, b01):
    b = b01.astype(jnp.bfloat16)
    hi, mid, lo = _split3(a)
    return _dot(hi, b) + _dot(mid, b) + _dot(lo, b)


def _select_rows(a01, b):
    if b.dtype == jnp.bfloat16:
        return _dot(a01.astype(jnp.bfloat16), b)
    return _dot01_left(a01, b)


def _dot01_left(a01, b):
    a = a01.astype(jnp.bfloat16)
    hi, mid, lo = _split3(b)
    return _dot(a, hi) + _dot(a, mid) + _dot(a, lo)


def _ffn_kernel(h_ref, g_ref, wg_ref, wu_ref, wd_ref, o_ref):
    h = h_ref[...]
    xn = _rms(h, g_ref[...]).astype(_MXU)
    a = _silu(_dot(xn, wg_ref[...])) * _dot(xn, wu_ref[...])
    o_ref[...] = h + 0.5 * _dot(a.astype(_MXU), wd_ref[...])


def _ffn(h, g, wg, wu, wd, *, tm):
    t, d = h.shape
    ff = wg.shape[1]
    once = dict(pipeline_mode=pl.Buffered(1))
    return pl.pallas_call(
        _ffn_kernel,
        grid=(t // tm,),
        in_specs=[
            pl.BlockSpec((tm, d), lambda i: (i, 0)),
            pl.BlockSpec((1, d), lambda i: (0, 0)),
            pl.BlockSpec((d, ff), lambda i: (0, 0), **once),
            pl.BlockSpec((d, ff), lambda i: (0, 0), **once),
            pl.BlockSpec((ff, d), lambda i: (0, 0), **once),
        ],
        out_specs=pl.BlockSpec((tm, d), lambda i: (i, 0)),
        out_shape=jax.ShapeDtypeStruct((t, d), F32),
        compiler_params=_cparams(("parallel",)),
        name="ffn",
    )(h, g, wg, wu, wd)


def _inproj_kernel(h_ref, g_ref, w_ref, ws_ref, o_ref, os_ref, *, tn):
    xn = _rms(h_ref[...], g_ref[...]).astype(_MXU)
    os_ref[...] = _dot(xn, ws_ref[...])
    for j in range(w_ref.shape[1] // tn):
        o_ref[:, j * tn:(j + 1) * tn] = _dot(xn, w_ref[:, j * tn:(j + 1) * tn]).astype(o_ref.dtype)


def _inproj(h, g, w, ws, *, tm, tn):
    t, d = h.shape
    n = w.shape[1]
    once = dict(pipeline_mode=pl.Buffered(1))
    return pl.pallas_call(
        functools.partial(_inproj_kernel, tn=tn),
        grid=(t // tm,),
        in_specs=[
            pl.BlockSpec((tm, d), lambda i: (i, 0)),
            pl.BlockSpec((1, d), lambda i: (0, 0)),
            pl.BlockSpec((d, n), lambda i: (0, 0), **once),
            pl.BlockSpec((d, 128), lambda i: (0, 0), **once),
        ],
        out_specs=[pl.BlockSpec((tm, n), lambda i: (i, 0)),
                   pl.BlockSpec((tm, 128), lambda i: (i, 0))],
        out_shape=[jax.ShapeDtypeStruct((t, n), _MXU), jax.ShapeDtypeStruct((t, 128), F32)],
        compiler_params=_cparams(("parallel",)),
        name="inproj",
    )(h, g, w, ws)


def _kvprep_kernel(ckv_ref, sm_ref, gkv_ref, gk_ref, ckv_o, ckvT_o, kidx_o):
    c = _rms(ckv_ref[...].astype(F32), gkv_ref[...])
    ckv_o[0] = c.astype(ckv_o.dtype)
    ones = jnp.ones((ONES_ROWS, c.shape[0]), F32)
    ckvT_o[0, 0] = jnp.concatenate([c.T, ones], axis=0).astype(ckvT_o.dtype)
    k = sm_ref[...][:, SM_KIDX:SM_KIDX + IDX_DIM]
    kidx_o[0] = _rms(k, gk_ref[...]).astype(kidx_o.dtype)


def _kvprep(proj, small, gkv, gk, *, b, s, kb):
    nk = s // kb
    return pl.pallas_call(
        _kvprep_kernel,
        grid=(b, nk),
        in_specs=[
            pl.BlockSpec((kb, KV_RANK), lambda bi, i: (bi * nk + i, C_CKV // KV_RANK)),
            pl.BlockSpec((kb, 128), lambda bi, i: (bi * nk + i, 0)),
            pl.BlockSpec((1, KV_RANK), lambda bi, i: (0, 0)),
            pl.BlockSpec((1, IDX_DIM), lambda bi, i: (0, 0)),
        ],
        out_specs=[
            pl.BlockSpec((1, kb, KV_RANK), lambda bi, i: (bi, i, 0)),
            pl.BlockSpec((1, 1, KV_RANK + ONES_ROWS, kb), lambda bi, i: (bi, i, 0, 0)),
            pl.BlockSpec((1, kb, IDX_DIM), lambda bi, i: (bi, i, 0)),
        ],
        out_shape=[
            jax.ShapeDtypeStruct((b, s, KV_RANK), _MXU),
            jax.ShapeDtypeStruct((b, nk, KV_RANK + ONES_ROWS, kb), _MXU),
            jax.ShapeDtypeStruct((b, s, IDX_DIM), _MXU),
        ],
        compiler_params=_cparams(("parallel", "parallel")),
        name="kvprep",
    )(proj, small, gkv, gk)


def _bit_transpose32(words):
    words = list(words)
    j, m = 16, 0x0000FFFF
    while j:
        mi = jnp.int32(m - (1 << 32) if m >= (1 << 31) else m)
        for k in range(32):
            if k & j == 0:
                t = (words[k] ^ lax.shift_right_logical(words[k + j], jnp.int32(j))) & mi
                words[k] = words[k] ^ t
                words[k + j] = words[k + j] ^ lax.shift_left(t, jnp.int32(j))
        j >>= 1
        m = (m ^ (m << j)) & 0xFFFFFFFF
    return words


def _attn_kernel(q_ref, qi_ref, sm_ref, ckv_ref, ckvT_ref, kidx_ref, wuk_ref, wuvT_ref, o_ref,
                 key_ref, planes_ref, qlat_ref, s_ref, p_ref, acc_ref, m_ref, alpha_ref,
                 *, k_top, kb):
    j = pl.program_id(1)
    nblk = (j * Q_BLOCK + Q_BLOCK + kb - 1) // kb
    nh = N_HEADS
    qb = Q_BLOCK

    qT = q_ref[...].astype(F32).T
    qiT = qi_ref[...].astype(F32).T
    smT = sm_ref[...].T
    w_row = jnp.concatenate([smT[SM_WIDX + h:SM_WIDX + h + 1, :] for h in range(nh)], axis=1)
    qiT_stack = jnp.concatenate(
        [qiT[h * IDX_DIM:(h + 1) * IDX_DIM, :] for h in range(nh)], axis=1).astype(_MXU)
    scale = HEAD_DIM ** -0.5
    qlatT = jnp.concatenate(
        [_dot(wuk_ref[h], qT[h * HEAD_DIM:(h + 1) * HEAD_DIM, :].astype(_MXU)) for h in range(nh)],
        axis=1)
    qlat_ref[...] = (qlatT * (scale * LOG2E)).astype(_MXU)
    s_ref[0] = _dot(ckv_ref[0, 0:kb, :], qlat_ref[...])

    qpos = j * qb + lax.broadcasted_iota(jnp.int32, (kb, qb), 1)

    def idx_body(bi, carry):
        off = pl.multiple_of(bi * kb, kb)
        kblk = kidx_ref[0, pl.ds(off, kb), :]
        r = jnp.maximum(_dot(kblk, qiT_stack), 0.0) * w_row
        sc = r[:, 0:qb]
        for h in range(1, nh):
            sc = sc + r[:, h * qb:(h + 1) * qb]
        bits = pltpu.bitcast(sc, jnp.int32)
        key = jnp.where(bits < 0, bits ^ jnp.int32(0x7FFFFFFF), bits)
        kpos = off + lax.broadcasted_iota(jnp.int32, (kb, qb), 0)
        key = jnp.where(kpos <= qpos, key, jnp.int32(INT_MIN))
        key_ref[pl.ds(off, kb), :] = key
        u = key ^ jnp.int32(INT_MIN)
        for g in range(kb // 256):
            words = _bit_transpose32([u[g * 256 + 8 * t:g * 256 + 8 * t + 8, :] for t in range(32)])
            row = pl.multiple_of((bi * (kb // 256) + g) * 8, 8)
            for i in range(32):
                planes_ref[i, pl.ds(row, 8), :] = words[i]
        return carry

    lax.fori_loop(0, nblk, idx_body, 0)

    ngrp_all = planes_ref.shape[1] // 8
    ngrp_live = nblk * (kb // 256)

    def radix_select(ng):
        rows = ng * 8
        grp = lax.broadcasted_iota(jnp.int32, (rows, qb), 0) // 8
        eq0 = jnp.where(grp < ngrp_live, jnp.int32(-1), jnp.int32(0))

        def count(words):
            c = jnp.sum(lax.population_count(words).reshape(ng, 8, qb), axis=0)
            return jnp.sum(c, axis=0, keepdims=True)

        def bit_body(i, carry):
            eq, n_above, prefix = carry
            x1 = planes_ref[2 * i, 0:rows, :]
            x0 = planes_ref[2 * i + 1, 0:rows, :]
            e1 = eq & x1
            n1, n11, nx0 = count(e1), count(e1 & x0), count(eq & x0)
            take1 = (n_above + n1) >= k_top
            n_above = jnp.where(take1, n_above, n_above + n1)
            n_lo = jnp.where(take1, n11, nx0 - n11)
            take0 = (n_above + n_lo) >= k_top
            n_above = jnp.where(take0, n_above, n_above + n_lo)
            flip1 = jnp.where(take1, jnp.int32(0), jnp.int32(-1))
            flip0 = jnp.where(take0, jnp.int32(0), jnp.int32(-1))
            eq = eq & (x1 ^ flip1) & (x0 ^ flip0)
            two = jnp.where(take1, 2, 0) | jnp.where(take0, 1, 0)
            return eq, n_above, prefix | jnp.left_shift(two, 30 - 2 * i)

        zrow = jnp.zeros((1, qb), jnp.int32)
        return lax.fori_loop(0, 16, bit_body, (eq0, zrow, zrow))[2]

    quarter = max(ngrp_all // 4, 1)
    sizes = list(range(quarter, ngrp_all + 1, quarter))
    which = jnp.minimum((ngrp_live - 1) // quarter, len(sizes) - 1)
    prefix = lax.switch(which, [functools.partial(radix_select, ng) for ng in sizes])
    thr = jnp.maximum(prefix ^ jnp.int32(INT_MIN), jnp.int32(INT_MIN + 1))
    thr_b = jnp.broadcast_to(thr, (RC, qb))

    m_ref[...] = jnp.full(m_ref.shape, NEG_BIG, F32)
    acc_ref[...] = jnp.zeros_like(acc_ref)

    def qk(bi, slot):
        off = pl.multiple_of(jnp.minimum(bi, nblk - 1) * kb, kb)
        s_ref[slot] = _dot(ckv_ref[0, pl.ds(off, kb), :], qlat_ref[...])

    def pv(bi, slot):
        acc_ref[...] = alpha_ref[...] * acc_ref[...] + _dot(ckvT_ref[0, jnp.maximum(bi, 0)], p_ref[slot])

    def softmax(bi, slot):
        off = pl.multiple_of(bi * kb, kb)
        nch = kb // RC

        def masks(c):
            return key_ref[pl.ds(off + c * RC, RC), :] >= thr_b

        mx = [jnp.full((8, qb), NEG_BIG, F32)] * nh
        for c in range(nch):
            mk = masks(c)
            for h in range(nh):
                sc = jnp.where(mk, s_ref[slot, c * RC:(c + 1) * RC, h * qb:(h + 1) * qb], NEG_BIG)
                mx[h] = jnp.maximum(mx[h], jnp.max(sc.reshape(RC // 8, 8, qb), axis=0))
        m_new = []
        for h in range(nh):
            hs = slice(h * qb, (h + 1) * qb)
            m_old = m_ref[:, hs]
            mn = jnp.maximum(m_old, jnp.max(mx[h], axis=0, keepdims=True))
            alpha_ref[:, hs] = jnp.exp2(m_old - mn)
            m_ref[:, hs] = mn
            m_new.append(jnp.broadcast_to(mn, (RC, qb)))
        for c in range(nch):
            mk = masks(c)
            for h in range(nh):
                sc = s_ref[slot, c * RC:(c + 1) * RC, h * qb:(h + 1) * qb]
                pr = jnp.where(mk, jnp.exp2(sc - m_new[h]), 0.0)
                p_ref[slot, c * RC:(c + 1) * RC, h * qb:(h + 1) * qb] = pr.astype(p_ref.dtype)

    def step(bi, cur):
        qk(bi + 1, 1 - cur)
        pv(bi - 1, 1 - cur)
        softmax(bi, cur)

    qk(1, 1)
    softmax(0, 0)

    def pair_body(i2, carry):
        step(2 * i2 + 1, 1)
        step(2 * i2 + 2, 0)
        return carry

    lax.fori_loop(0, (nblk - 1) // 2, pair_body, 0)
    even = nblk % 2 == 0

    @pl.when(even)
    def _():
        pv(nblk - 2, 0)
        softmax(nblk - 1, 1)
        pv(nblk - 1, 1)

    @pl.when(jnp.logical_not(even))
    def _():
        pv(nblk - 1, 0)

    inv_l = 1.0 / acc_ref[KV_RANK:KV_RANK + 1, :]
    o = (acc_ref[:KV_RANK, :] * inv_l).astype(_MXU)
    yT = jnp.concatenate([_dot(wuvT_ref[h], o[:, h * qb:(h + 1) * qb]) for h in range(nh)], axis=0)
    o_ref[...] = yT.T


def _attention(proj, small, ckv, ckvT, kidx, wuk, wuvT, *, b, s, kb):
    nq = s // Q_BLOCK
    k_top = min(TOPK_MAX, s // 4)
    kern = functools.partial(_attn_kernel, k_top=k_top, kb=kb)
    hq = N_HEADS * Q_BLOCK
    return pl.pallas_call(
        kern,
        grid=(b, nq),
        in_specs=[
            pl.BlockSpec((Q_BLOCK, 512), lambda bi, j: (bi * nq + j, C_Q // 512)),
            pl.BlockSpec((Q_BLOCK, 512), lambda bi, j: (bi * nq + j, C_QIDX // 512)),
            pl.BlockSpec((Q_BLOCK, 128), lambda bi, j: (bi * nq + j, 0)),
            pl.BlockSpec((1, s, KV_RANK), lambda bi, j: (bi, 0, 0)),
            pl.BlockSpec((1, s // kb, KV_RANK + ONES_ROWS, kb), lambda bi, j: (bi, 0, 0, 0)),
            pl.BlockSpec((1, s, IDX_DIM), lambda bi, j: (bi, 0, 0)),
            pl.BlockSpec((N_HEADS, KV_RANK, HEAD_DIM), lambda bi, j: (0, 0, 0)),
            pl.BlockSpec((N_HEADS, HEAD_DIM, KV_RANK), lambda bi, j: (0, 0, 0)),
        ],
        out_specs=pl.BlockSpec((Q_BLOCK, N_HEADS * HEAD_DIM), lambda bi, j: (bi * nq + j, 0)),
        out_shape=jax.ShapeDtypeStruct((b * s, N_HEADS * HEAD_DIM), F32),
        scratch_shapes=[
            pltpu.VMEM((s, Q_BLOCK), jnp.int32),
            pltpu.VMEM((32, s // 32, Q_BLOCK), jnp.int32),
            pltpu.VMEM((KV_RANK, hq), _MXU),
            pltpu.VMEM((2, kb, hq), F32),
            pltpu.VMEM((2, kb, hq), _MXU),
            pltpu.VMEM((KV_RANK + ONES_ROWS, hq), F32),
            pltpu.VMEM((1, hq), F32),
            pltpu.VMEM((1, hq), F32),
        ],
        compiler_params=_cparams(("parallel", "arbitrary")),
        name="dsa_attn",
    )(proj, proj, small, ckv, ckvT, kidx, wuk, wuvT)


def _pool_kernel(x_ref, halo_ref, pw_ref, ps_ref, o_ref, *, tm):
    i = pl.program_id(1)
    x = x_ref[...].astype(F32)
    halo = jnp.where(i > 0, halo_ref[...].astype(F32), 0.0)
    xe = jnp.concatenate([halo, x], axis=0)
    pos = i * tm + lax.broadcasted_iota(jnp.int32, (tm, POOL_GROUP), 0)
    outs = []
    for g, w in enumerate(POOL_WINDOWS):
        sl = slice(g * POOL_GROUP, (g + 1) * POOL_GROUP)
        cur = xe[:, sl]
        span = 1
        while span < w:
            n = cur.shape[0]
            cur = cur[span:, :] + cur[:n - span, :]
            span *= 2
        win = cur[cur.shape[0] - tm:, :]
        count = jnp.minimum(pos + 1, w).astype(F32)
        pooled = win / count - x[:, sl]
        outs.append(_dot(pooled.astype(_MXU), pw_ref[g]))
    o_ref[...] = jnp.concatenate(outs, axis=1) * ps_ref[...]


def _pool(proj, pw, ps, *, b, s, tm):
    nt = s // tm
    kern = functools.partial(_pool_kernel, tm=tm)
    return pl.pallas_call(
        kern,
        grid=(b, nt),
        in_specs=[
            pl.BlockSpec((tm, POOL_WIDTH), lambda bi, i: (bi * nt + i, C_POOL // POOL_WIDTH)),
            pl.BlockSpec((16, POOL_WIDTH),
                         lambda bi, i: (jnp.maximum((bi * s + i * tm) // 16 - 1, 0), C_POOL // POOL_WIDTH)),
            pl.BlockSpec((len(POOL_WINDOWS), POOL_GROUP, POOL_GROUP), lambda bi, i: (0, 0, 0)),
            pl.BlockSpec((1, POOL_WIDTH), lambda bi, i: (0, 0)),
        ],
        out_specs=pl.BlockSpec((tm, POOL_WIDTH), lambda bi, i: (bi * nt + i, 0)),
        out_shape=jax.ShapeDtypeStruct((b * s, POOL_WIDTH), F32),
        compiler_params=_cparams(("parallel", "parallel")),
        name="pool",
    )(proj, proj, pw, ps)


def _ssd_kernel(xbc_ref, halo_ref, z_ref, sm_ref, cw_ref, cb_ref, dtb_row_ref, dtb_col_ref, a_row_ref, a_col_ref,
                dsk_x_ref, expand_ref, ng_ref, o_ref, state_ref):
    c = pl.program_id(1)
    L = CHUNK
    P = SSD_HEAD_DIM

    @pl.when(c == 0)
    def _():
        state_ref[...] = jnp.zeros_like(state_ref)

    halo0 = halo_ref[...]
    halo0 = jnp.where(c > 0, halo0, jnp.zeros_like(halo0))
    for sub in range(SSD_SUB):
        rows = slice(sub * L, (sub + 1) * L)
        x_nat = xbc_ref[rows, :]
        halo = halo0 if sub == 0 else xbc_ref[sub * L - CONV_TAIL:sub * L, :]
        o_ref[rows, :] = _ssd_chunk(x_nat, halo, z_ref[rows, :], sm_ref[rows, :], cw_ref, cb_ref, dtb_row_ref,
                                    dtb_col_ref, a_row_ref, a_col_ref, dsk_x_ref, expand_ref, ng_ref, state_ref)


def _ssd_chunk(x_nat, halo, z, sm, cw_ref, cb_ref, dtb_row_ref, dtb_col_ref, a_row_ref, a_col_ref,
               dsk_x_ref, expand_ref, ng_ref, state_ref):
    L = CHUNK
    P = SSD_HEAD_DIM
    xe = jnp.concatenate([halo, x_nat], axis=0)
    cw = cw_ref[...]
    conv = cb_ref[...] + cw[3:4, :] * x_nat.astype(F32)
    srow = lax.broadcasted_iota(jnp.int32, (L, L + CONV_TAIL), 0)
    scol = lax.broadcasted_iota(jnp.int32, (L, L + CONV_TAIL), 1)
    for k in range(CONV_WIDTH - 1):
        sh = CONV_WIDTH - 1 - k
        sel = jnp.where(scol == srow + (CONV_TAIL - sh), 1.0, 0.0)
        conv = conv + cw[k:k + 1, :] * _select_rows(sel, xe)
    xc = _silu(conv)
    xs = xc[:, :D_INNER]
    bm = xc[:, D_INNER:D_INNER + N_GROUPS * D_STATE]
    cm = xc[:, D_INNER + N_GROUPS * D_STATE:]

    smT = sm.T
    dt = jax.nn.softplus(sm[:, SM_DT:SM_DT + SSD_HEADS] + dtb_row_ref[...])
    dtT = jax.nn.softplus(smT[SM_DT:SM_DT + SSD_HEADS, :] + dtb_col_ref[...])
    expand = expand_ref[...]
    dtx = _dot01_right(dt, expand)
    da = dt * a_row_ref[...]
    daT = dtT * a_col_ref[...]
    row = lax.broadcasted_iota(jnp.int32, (L, L), 0)
    col = lax.broadcasted_iota(jnp.int32, (L, L), 1)
    lower = row >= col
    tril = jnp.where(lower, 1.0, 0.0).astype(F32)
    triu = jnp.where(row <= col, 1.0, 0.0).astype(F32)
    acs = _dot01_left(tril, da)
    acsT = _dot01_right(daT, triu)
    acs_x = _dot01_right(acs, expand)
    a_last = acs_x[L - 1:L, :]
    ea_x = jnp.exp(acs_x)
    decay_x = jnp.exp(a_last - acs_x)
    cd_x = jnp.exp(a_last)

    X = xs * dtx
    Xb = X.astype(_MXU)
    Xd = (X * decay_x).astype(_MXU)
    state = state_ref[...]
    ys = []
    hpg = SSD_HEADS // N_GROUPS
    for g in range(N_GROUPS):
        bg = bm[:, g * D_STATE:(g + 1) * D_STATE]
        cg = cm[:, g * D_STATE:(g + 1) * D_STATE].astype(_MXU)
        bgT = bg.T.astype(_MXU)
        cb = _dot(cg, bgT)
        gs = slice(g * hpg * P, (g + 1) * hpg * P)
        y_off = _dot(cg, state[:, gs].astype(_MXU)) * ea_x[:, gs]
        state_ref[:, gs] = state[:, gs] * cd_x[:, gs] + _dot(bgT, Xd[:, gs])
        for e in range(hpg):
            h = g * hpg + e
            hs = slice(h * P, (h + 1) * P)
            diff = acs[:, h:h + 1] - acsT[h:h + 1, :]
            lmat = jnp.exp(jnp.where(lower, diff, -jnp.inf))
            m = (cb * lmat).astype(_MXU)
            ys.append(_dot(m, Xb[:, hs]) + y_off[:, e * P:(e + 1) * P])
    y = jnp.concatenate(ys, axis=1) + xs * dsk_x_ref[...]
    y = y * _silu(z.astype(F32))
    return _rms(y, ng_ref[...])


def _ssd(proj, small, cw, cb, dtb, a_log, dsk, ng, *, b, s):
    blk = CHUNK * SSD_SUB
    nc = s // blk
    rep = lambda v: jnp.repeat(v, SSD_HEAD_DIM)[None, :]
    a = -jnp.exp(a_log)
    expand = jnp.repeat(jnp.eye(SSD_HEADS, dtype=F32), SSD_HEAD_DIM, axis=1)
    const2 = lambda bi, c: (0, 0)
    return pl.pallas_call(
        _ssd_kernel,
        grid=(b, nc),
        in_specs=[
            pl.BlockSpec((blk, CONV_DIM), lambda bi, c: (bi * nc + c, C_XBC // CONV_DIM)),
            pl.BlockSpec((CONV_TAIL, CONV_DIM),
                         lambda bi, c: (jnp.maximum((bi * nc + c) * (blk // CONV_TAIL) - 1, 0), C_XBC // CONV_DIM)),
            pl.BlockSpec((blk, D_INNER), lambda bi, c: (bi * nc + c, C_Z // D_INNER)),
            pl.BlockSpec((blk, 128), lambda bi, c: (bi * nc + c, 0)),
            pl.BlockSpec((CONV_WIDTH, CONV_DIM), const2),
            pl.BlockSpec((1, CONV_DIM), const2),
            pl.BlockSpec((1, SSD_HEADS), const2),
            pl.BlockSpec((SSD_HEADS, 1), const2),
            pl.BlockSpec((1, SSD_HEADS), const2),
            pl.BlockSpec((SSD_HEADS, 1), const2),
            pl.BlockSpec((1, D_INNER), const2),
            pl.BlockSpec((SSD_HEADS, D_INNER), const2),
            pl.BlockSpec((1, D_INNER), const2),
        ],
        out_specs=pl.BlockSpec((blk, D_INNER), lambda bi, c: (bi * nc + c, 0)),
        out_shape=jax.ShapeDtypeStruct((b * s, D_INNER), F32),
        scratch_shapes=[pltpu.VMEM((D_STATE, D_INNER), F32)],
        compiler_params=_cparams(("parallel", "arbitrary")),
        name="ssd",
    )(proj, proj, proj, small, cw, cb[None, :], dtb[None, :], dtb[:, None], a[None, :], a[:, None],
      rep(dsk), expand, ng[None, :])


def _merge_kernel(ya_ref, yp_ref, ys_ref, ga_ref, gp_ref, gs_ref, h_ref, wa_ref, wp_ref, ws_ref,
                  wo_ref, o_ref):
    m = _sigmoid(ga_ref[...].astype(F32)) * _dot(ya_ref[...].astype(_MXU), wa_ref[...])
    m = m + _sigmoid(gp_ref[...].astype(F32)) * _dot(yp_ref[...].astype(_MXU), wp_ref[...])
    m = m + _sigmoid(gs_ref[...].astype(F32)) * _dot(ys_ref[...].astype(_MXU), ws_ref[...])
    o_ref[...] = h_ref[...] + _dot(m.astype(_MXU), wo_ref[...])


def _merge(ya, yp, ys, proj, h, wa, wp, ws, wo, *, tm):
    t, d = h.shape
    row = lambda i: (i, 0)
    const = lambda i: (0, 0)
    gate = lambda k: (lambda i: (i, C_GATE // d + k))
    return pl.pallas_call(
        _merge_kernel,
        grid=(t // tm,),
        in_specs=[
            pl.BlockSpec((tm, ya.shape[1]), row),
            pl.BlockSpec((tm, yp.shape[1]), row),
            pl.BlockSpec((tm, ys.shape[1]), row),
            pl.BlockSpec((tm, d), gate(0)),
            pl.BlockSpec((tm, d), gate(1)),
            pl.BlockSpec((tm, d), gate(2)),
            pl.BlockSpec((tm, d), row),
            pl.BlockSpec(wa.shape, const),
            pl.BlockSpec(wp.shape, const),
            pl.BlockSpec(ws.shape, const),
            pl.BlockSpec(wo.shape, const),
        ],
        out_specs=pl.BlockSpec((tm, d), row),
        out_shape=jax.ShapeDtypeStruct((t, d), F32),
        compiler_params=_cparams(("parallel",)),
        name="merge",
    )(ya, yp, ys, proj, proj, proj, h, wa, wp, ws, wo)


def _ple_kernel(h_ref, p_ref, g_ref, wg_ref, wp_ref, fg_ref, o_ref, *, final):
    h = h_ref[...]
    gate = _sigmoid(_dot(_rms(h, g_ref[...]).astype(_MXU), wg_ref[...]))
    h = h + gate * _dot(p_ref[...].astype(_MXU), wp_ref[...])
    if final:
        h = _rms(h, fg_ref[...])
    o_ref[...] = h


def _ple(h, p, g, wg, wp, fg, *, tm, final):
    t, d = h.shape
    row = lambda i: (i, 0)
    const = lambda i: (0, 0)
    return pl.pallas_call(
        functools.partial(_ple_kernel, final=final),
        grid=(t // tm,),
        in_specs=[
            pl.BlockSpec((tm, d), row),
            pl.BlockSpec((tm, p.shape[1]), row),
            pl.BlockSpec((1, d), const),
            pl.BlockSpec(wg.shape, const),
            pl.BlockSpec(wp.shape, const),
            pl.BlockSpec((1, d), const),
        ],
        out_specs=pl.BlockSpec((tm, d), row),
        out_shape=jax.ShapeDtypeStruct((t, d), F32),
        compiler_params=_cparams(("parallel",)),
        name="ple",
    )(h, p, g, wg, wp, fg)


def _reorder_w_in(w):
    o_q, o_ckv, o_qidx, o_widx, o_kidx, o_pool, o_z, o_xbc, o_dt, o_gate = (
        0, 512, 768, 1280, 1288, 1352, 1864, 2888, 4424, 4440)
    d = w.shape[0]
    small = jnp.concatenate([
        w[:, o_kidx:o_kidx + IDX_DIM], w[:, o_widx:o_widx + IDX_HEADS], w[:, o_dt:o_dt + SSD_HEADS],
        jnp.zeros((d, 128 - IDX_DIM - IDX_HEADS - SSD_HEADS), w.dtype)], axis=1)
    wb = w.astype(_MXU)
    used = CONV_DIM + 512 + D_INNER + 3 * D_MODEL + 512 + POOL_WIDTH + KV_RANK
    wide = jnp.concatenate([
        wb[:, o_xbc:o_xbc + CONV_DIM], wb[:, o_q:o_q + 512], wb[:, o_z:o_z + D_INNER],
        wb[:, o_gate:o_gate + 3 * D_MODEL], wb[:, o_qidx:o_qidx + 512], wb[:, o_pool:o_pool + POOL_WIDTH],
        wb[:, o_ckv:o_ckv + KV_RANK], jnp.zeros((d, PROJ_COLS - used), _MXU)], axis=1)
    return wide, small.astype(_MXU)


def kernel(x, p, ffn1_norm, ffn1_w_gate, ffn1_w_up, ffn1_w_down, mix_norm, w_in, kv_norm, idx_k_norm, w_uk, w_uv, pool_w, pool_scale, conv_w, conv_b, dt_bias, a_log, d_skip, ssd_norm, w_br_attn, w_br_pool, w_br_ssd, w_out, ffn2_norm, ffn2_w_gate, ffn2_w_up, ffn2_w_down, ple_norm, ple_w_gate, ple_w_proj, final_norm):
    b, s, d = x.shape
    depth = w_in.shape[0]
    t = b * s
    tm = min(1024, t)
    tm_small = min(512, t)
    kb = min(512, s)
    mx = lambda a: a.astype(_MXU)

    h = x.reshape(t, d)
    for i in range(depth):
        h = _ffn(h, ffn1_norm[i][None], mx(ffn1_w_gate[i]), mx(ffn1_w_up[i]), mx(ffn1_w_down[i]),
                 tm=tm_small)
        w_wide, w_small = _reorder_w_in(w_in[i])
        proj, small = _inproj(h, mix_norm[i][None], w_wide, w_small, tm=tm_small, tn=2560)
        ckv, ckvT, kidx = _kvprep(proj, small, kv_norm[i][None], idx_k_norm[i][None], b=b, s=s, kb=kb)
        wuk = mx(jnp.transpose(w_uk[i], (1, 0, 2)))
        wuvT = mx(jnp.transpose(w_uv[i], (1, 2, 0)))
        y_attn = _attention(proj, small, ckv, ckvT, kidx, wuk, wuvT, b=b, s=s, kb=kb)
        y_pool = _pool(proj, mx(pool_w[i]), pool_scale[i][None], b=b, s=s, tm=tm_small)
        y_ssd = _ssd(proj, small, conv_w[i], conv_b[i], dt_bias[i], a_log[i], d_skip[i], ssd_norm[i], b=b, s=s)
        h = _merge(y_attn, y_pool, y_ssd, proj, h, mx(w_br_attn[i]), mx(w_br_pool[i]),
                   mx(w_br_ssd[i]), mx(w_out[i]), tm=tm_small)
        h = _ffn(h, ffn2_norm[i][None], mx(ffn2_w_gate[i]), mx(ffn2_w_up[i]), mx(ffn2_w_down[i]),
                 tm=tm_small)
        h = _ple(h, p[i].reshape(t, -1), ple_norm[i][None], mx(ple_w_gate[i]), mx(ple_w_proj[i]),
                 final_norm[None], tm=tm, final=(i == depth - 1))
    return h.reshape(b, s, d)
```

```python
import functools

import jax
import jax.numpy as jnp
from jax import lax
from jax.experimental import pallas as pl
from jax.experimental.pallas import tpu as pltpu

_MXU = jnp.bfloat16
F32 = jnp.float32

D_MODEL = 1024
N_HEADS = 8
HEAD_DIM = 64
KV_RANK = 256
IDX_HEADS = 8
IDX_DIM = 64
TOPK_MAX = 256
Q_BLOCK = 256
POOL_WINDOWS = (2, 4, 8, 16)
POOL_WIDTH = 512
POOL_GROUP = 128
SSD_HEADS = 16
SSD_HEAD_DIM = 64
D_INNER = 1024
N_GROUPS = 2
D_STATE = 128
CONV_WIDTH = 4
CONV_DIM = D_INNER + 2 * N_GROUPS * D_STATE
CHUNK = 128
D_FF = 2816
PLE_DIM = 256
EPS = 1e-6

C_XBC = 0
C_Q = 1536
C_Z = 2048
C_GATE = 3072
C_QIDX = 6144
C_POOL = 6656
C_CKV = 7168
PROJ_COLS = 7680
SM_KIDX = 0
SM_WIDX = 64
SM_DT = 72

INT_MIN = -2147483648
LOG2E = 1.4426950408889634
RC = 32
ONES_ROWS = 16
SSD_SUB = 4
CONV_TAIL = 16
NEG_BIG = -1e30
VMEM_LIMIT = 56 * 1024 * 1024


def _cparams(sem):
    return pltpu.CompilerParams(dimension_semantics=sem, vmem_limit_bytes=VMEM_LIMIT)


def _rms(x, g):
    return x * lax.rsqrt(jnp.mean(x * x, axis=-1, keepdims=True) + EPS) * g


def _dot(a, b):
    return jnp.dot(a, b, preferred_element_type=F32)


def _sigmoid(x):
    return 0.5 * jnp.tanh(0.5 * x) + 0.5


def _silu(x):
    h = 0.5 * x
    return h * jnp.tanh(h) + h


def _split3(a):
    hi = a.astype(jnp.bfloat16)
    r1 = a - hi.astype(F32)
    mid = r1.astype(jnp.bfloat16)
    lo = (r1 - mid.astype(F32)).astype(jnp.bfloat16)
    return hi, mid, lo


def _dot01_right(a, b01):
    b = b01.astype(jnp.bfloat16)
    hi, mid, lo = _split3(a)
    return _dot(hi, b) + _dot(mid, b) + _dot(lo, b)


def _select_rows(a01, b):
    if b.dtype == jnp.bfloat16:
        return _dot(a01.astype(jnp.bfloat16), b)
    return _dot01_left(a01, b)


def _dot01_left(a01, b):
    a = a01.astype(jnp.bfloat16)
    hi, mid, lo = _split3(b)
    return _dot(a, hi) + _dot(a, mid) + _dot(a, lo)


def _ffn_kernel(h_ref, g_ref, wg_ref, wu_ref, wd_ref, o_ref):
    h = h_ref[...]
    xn = _rms(h, g_ref[...]).astype(_MXU)
    a = _silu(_dot(xn, wg_ref[...])) * _dot(xn, wu_ref[...])
    o_ref[...] = h + 0.5 * _dot(a.astype(_MXU), wd_ref[...])


def _ffn(h, g, wg, wu, wd, *, tm):
    t, d = h.shape
    ff = wg.shape[1]
    once = dict(pipeline_mode=pl.Buffered(1))
    return pl.pallas_call(
        _ffn_kernel,
        grid=(t // tm,),
        in_specs=[
            pl.BlockSpec((tm, d), lambda i: (i, 0)),
            pl.BlockSpec((1, d), lambda i: (0, 0)),
            pl.BlockSpec((d, ff), lambda i: (0, 0), **once),
            pl.BlockSpec((d, ff), lambda i: (0, 0), **once),
            pl.BlockSpec((ff, d), lambda i: (0, 0), **once),
        ],
        out_specs=pl.BlockSpec((tm, d), lambda i: (i, 0)),
        out_shape=jax.ShapeDtypeStruct((t, d), F32),
        compiler_params=_cparams(("parallel",)),
        name="ffn",
    )(h, g, wg, wu, wd)


def _inproj_kernel(h_ref, g_ref, w_ref, ws_ref, gkv_ref, gk_ref, o_ref, os_ref, ckv_o, ckvT_o, kidx_o, *, tn):
    xn = _rms(h_ref[...], g_ref[...]).astype(_MXU)
    small = _dot(xn, ws_ref[...])
    os_ref[...] = small
    for j in range(w_ref.shape[1] // tn):
        res = _dot(xn, w_ref[:, j * tn:(j + 1) * tn])
        o_ref[:, j * tn:(j + 1) * tn] = res.astype(o_ref.dtype)
        if j * tn <= C_CKV and C_CKV + KV_RANK <= (j + 1) * tn:
            c_raw = res[:, C_CKV - j * tn:C_CKV - j * tn + KV_RANK]
    c = _rms(c_raw, gkv_ref[...])
    ckv_o[0] = c.astype(ckv_o.dtype)
    ones = jnp.ones((ONES_ROWS, c.shape[0]), F32)
    ckvT_o[0, 0] = jnp.concatenate([c.T, ones], axis=0).astype(ckvT_o.dtype)
    kidx_o[0] = _rms(small[:, SM_KIDX:SM_KIDX + IDX_DIM], gk_ref[...]).astype(kidx_o.dtype)


def _inproj(h, g, w, ws, gkv, gk, *, b, s, tm, tn):
    t, d = h.shape
    n = w.shape[1]
    nk = s // tm
    once = dict(pipeline_mode=pl.Buffered(1))
    return pl.pallas_call(
        functools.partial(_inproj_kernel, tn=tn),
        grid=(t // tm,),
        in_specs=[
            pl.BlockSpec((tm, d), lambda i: (i, 0)),
            pl.BlockSpec((1, d), lambda i: (0, 0)),
            pl.BlockSpec((d, n), lambda i: (0, 0), **once),
            pl.BlockSpec((d, 128), lambda i: (0, 0), **once),
            pl.BlockSpec((1, KV_RANK), lambda i: (0, 0)),
            pl.BlockSpec((1, IDX_DIM), lambda i: (0, 0)),
        ],
        out_specs=[
            pl.BlockSpec((tm, n), lambda i: (i, 0)),
            pl.BlockSpec((tm, 128), lambda i: (i, 0)),
            pl.BlockSpec((1, tm, KV_RANK), lambda i: (i // nk, i % nk, 0)),
            pl.BlockSpec((1, 1, KV_RANK + ONES_ROWS, tm), lambda i: (i // nk, i % nk, 0, 0)),
            pl.BlockSpec((1, tm, IDX_DIM), lambda i: (i // nk, i % nk, 0)),
        ],
        out_shape=[
            jax.ShapeDtypeStruct((t, n), _MXU),
            jax.ShapeDtypeStruct((t, 128), F32),
            jax.ShapeDtypeStruct((b, s, KV_RANK), _MXU),
            jax.ShapeDtypeStruct((b, nk, KV_RANK + ONES_ROWS, tm), _MXU),
            jax.ShapeDtypeStruct((b, s, IDX_DIM), _MXU),
        ],
        compiler_params=_cparams(("parallel",)),
        name="inproj",
    )(h, g, w, ws, gkv, gk)


def _bit_transpose32(words):
    words = list(words)
    j, m = 16, 0x0000FFFF
    while j:
        mi = jnp.int32(m - (1 << 32) if m >= (1 << 31) else m)
        for k in range(32):
            if k & j == 0:
                t = (words[k] ^ lax.shift_right_logical(words[k + j], jnp.int32(j))) & mi
                words[k] = words[k] ^ t
                words[k + j] = words[k + j] ^ lax.shift_left(t, jnp.int32(j))
        j >>= 1
        m = (m ^ (m << j)) & 0xFFFFFFFF
    return words


def _attn_kernel(q_ref, qi_ref, sm_ref, ckv_ref, ckvT_ref, kidx_ref, wuk_ref, wuvT_ref, o_ref,
                 key_ref, planes_ref, qlat_ref, s_ref, p_ref, acc_ref, m_ref, alpha_ref,
                 *, k_top, kb):
    j = pl.program_id(1)
    nblk = (j * Q_BLOCK + Q_BLOCK + kb - 1) // kb
    nh = N_HEADS
    qb = Q_BLOCK

    qT = q_ref[...].astype(F32).T
    qiT = qi_ref[...].astype(F32).T
    smT = sm_ref[...].T
    w_row = jnp.concatenate([smT[SM_WIDX + h:SM_WIDX + h + 1, :] for h in range(nh)], axis=1)
    qiT_stack = jnp.concatenate(
        [qiT[h * IDX_DIM:(h + 1) * IDX_DIM, :] for h in range(nh)], axis=1).astype(_MXU)
    scale = HEAD_DIM ** -0.5
    qlatT = jnp.concatenate(
        [_dot(wuk_ref[h], qT[h * HEAD_DIM:(h + 1) * HEAD_DIM, :].astype(_MXU)) for h in range(nh)],
        axis=1)
    qlat_ref[...] = (qlatT * (scale * LOG2E)).astype(_MXU)
    s_ref[0] = _dot(ckv_ref[0, 0:kb, :], qlat_ref[...])

    qpos = j * qb + lax.broadcasted_iota(jnp.int32, (kb, qb), 1)

    def idx_body(bi, carry):
        off = pl.multiple_of(bi * kb, kb)
        kblk = kidx_ref[0, pl.ds(off, kb), :]
        r = jnp.maximum(_dot(kblk, qiT_stack), 0.0) * w_row
        sc = r[:, 0:qb]
        for h in range(1, nh):
            sc = sc + r[:, h * qb:(h + 1) * qb]
        bits = pltpu.bitcast(sc, jnp.int32)
        key = jnp.where(bits < 0, bits ^ jnp.int32(0x7FFFFFFF), bits)
        kpos = off + lax.broadcasted_iota(jnp.int32, (kb, qb), 0)
        key = jnp.where(kpos <= qpos, key, jnp.int32(INT_MIN))
        key_ref[pl.ds(off, kb), :] = key
        u = key ^ jnp.int32(INT_MIN)
        for g in range(kb // 256):
            words = _bit_transpose32([u[g * 256 + 8 * t:g * 256 + 8 * t + 8, :] for t in range(32)])
            row = pl.multiple_of((bi * (kb // 256) + g) * 8, 8)
            for i in range(32):
                planes_ref[i, pl.ds(row, 8), :] = words[i]
        return carry

    lax.fori_loop(0, nblk, idx_body, 0)

    ngrp_all = planes_ref.shape[1] // 8
    ngrp_live = nblk * (kb // 256)

    def radix_select(ng):
        rows = ng * 8
        grp = lax.broadcasted_iota(jnp.int32, (rows, qb), 0) // 8
        eq0 = jnp.where(grp < ngrp_live, jnp.int32(-1), jnp.int32(0))

        def count(words):
            c = jnp.sum(lax.population_count(words).reshape(ng, 8, qb), axis=0)
            return jnp.sum(c, axis=0, keepdims=True)

        def bit_body(i, carry):
            eq, n_above, prefix = carry
            x1 = planes_ref[2 * i, 0:rows, :]
            x0 = planes_ref[2 * i + 1, 0:rows, :]
            e1 = eq & x1
            n1, n11, nx0 = count(e1), count(e1 & x0), count(eq & x0)
            take1 = (n_above + n1) >= k_top
            n_above = jnp.where(take1, n_above, n_above + n1)
            n_lo = jnp.where(take1, n11, nx0 - n11)
            take0 = (n_above + n_lo) >= k_top
            n_above = jnp.where(take0, n_above, n_above + n_lo)
            flip1 = jnp.where(take1, jnp.int32(0), jnp.int32(-1))
            flip0 = jnp.where(take0, jnp.int32(0), jnp.int32(-1))
            eq = eq & (x1 ^ flip1) & (x0 ^ flip0)
            two = jnp.where(take1, 2, 0) | jnp.where(take0, 1, 0)
            return eq, n_above, prefix | jnp.left_shift(two, 30 - 2 * i)

        zrow = jnp.zeros((1, qb), jnp.int32)
        return lax.fori_loop(0, 16, bit_body, (eq0, zrow, zrow))[2]

    quarter = max(ngrp_all // 4, 1)
    sizes = list(range(quarter, ngrp_all + 1, quarter))
    which = jnp.minimum((ngrp_live - 1) // quarter, len(sizes) - 1)
    prefix = lax.switch(which, [functools.partial(radix_select, ng) for ng in sizes])
    thr = jnp.maximum(prefix ^ jnp.int32(INT_MIN), jnp.int32(INT_MIN + 1))
    thr_b = jnp.broadcast_to(thr, (RC, qb))

    m_ref[...] = jnp.full(m_ref.shape, NEG_BIG, F32)
    acc_ref[...] = jnp.zeros_like(acc_ref)

    def qk(bi, slot):
        off = pl.multiple_of(jnp.minimum(bi, nblk - 1) * kb, kb)
        s_ref[slot] = _dot(ckv_ref[0, pl.ds(off, kb), :], qlat_ref[...])

    def pv(bi, slot):
        acc_ref[...] = alpha_ref[...] * acc_ref[...] + _dot(ckvT_ref[0, jnp.maximum(bi, 0)], p_ref[slot])

    def softmax(bi, slot):
        off = pl.multiple_of(bi * kb, kb)
        nch = kb // RC

        def masks(c):
            return key_ref[pl.ds(off + c * RC, RC), :] >= thr_b

        mx = [jnp.full((8, qb), NEG_BIG, F32)] * nh
        for c in range(nch):
            mk = masks(c)
            for h in range(nh):
                sc = jnp.where(mk, s_ref[slot, c * RC:(c + 1) * RC, h * qb:(h + 1) * qb], NEG_BIG)
                mx[h] = jnp.maximum(mx[h], jnp.max(sc.reshape(RC // 8, 8, qb), axis=0))
        m_new = []
        for h in range(nh):
            hs = slice(h * qb, (h + 1) * qb)
            m_old = m_ref[:, hs]
            mn = jnp.maximum(m_old, jnp.max(mx[h], axis=0, keepdims=True))
            alpha_ref[:, hs] = jnp.exp2(m_old - mn)
            m_ref[:, hs] = mn
            m_new.append(jnp.broadcast_to(mn, (RC, qb)))
        for c in range(nch):
            mk = masks(c)
            for h in range(nh):
                sc = s_ref[slot, c * RC:(c + 1) * RC, h * qb:(h + 1) * qb]
                pr = jnp.where(mk, jnp.exp2(sc - m_new[h]), 0.0)
                p_ref[slot, c * RC:(c + 1) * RC, h * qb:(h + 1) * qb] = pr.astype(p_ref.dtype)

    def step(bi, cur):
        qk(bi + 1, 1 - cur)
        pv(bi - 1, 1 - cur)
        softmax(bi, cur)

    qk(1, 1)
    softmax(0, 0)

    def pair_body(i2, carry):
        step(2 * i2 + 1, 1)
        step(2 * i2 + 2, 0)
        return carry

    lax.fori_loop(0, (nblk - 1) // 2, pair_body, 0)
    even = nblk % 2 == 0

    @pl.when(even)
    def _():
        pv(nblk - 2, 0)
        softmax(nblk - 1, 1)
        pv(nblk - 1, 1)

    @pl.when(jnp.logical_not(even))
    def _():
        pv(nblk - 1, 0)

    inv_l = 1.0 / acc_ref[KV_RANK:KV_RANK + 1, :]
    o = (acc_ref[:KV_RANK, :] * inv_l).astype(_MXU)
    yT = jnp.concatenate([_dot(wuvT_ref[h], o[:, h * qb:(h + 1) * qb]) for h in range(nh)], axis=0)
    o_ref[...] = yT.T


def _attention(proj, small, ckv, ckvT, kidx, wuk, wuvT, *, b, s, kb):
    nq = s // Q_BLOCK
    k_top = min(TOPK_MAX, s // 4)
    kern = functools.partial(_attn_kernel, k_top=k_top, kb=kb)
    hq = N_HEADS * Q_BLOCK
    return pl.pallas_call(
        kern,
        grid=(b, nq),
        in_specs=[
            pl.BlockSpec((Q_BLOCK, 512), lambda bi, j: (bi * nq + j, C_Q // 512)),
            pl.BlockSpec((Q_BLOCK, 512), lambda bi, j: (bi * nq + j, C_QIDX // 512)),
            pl.BlockSpec((Q_BLOCK, 128), lambda bi, j: (bi * nq + j, 0)),
            pl.BlockSpec((1, s, KV_RANK), lambda bi, j: (bi, 0, 0)),
            pl.BlockSpec((1, s // kb, KV_RANK + ONES_ROWS, kb), lambda bi, j: (bi, 0, 0, 0)),
            pl.BlockSpec((1, s, IDX_DIM), lambda bi, j: (bi, 0, 0)),
            pl.BlockSpec((N_HEADS, KV_RANK, HEAD_DIM), lambda bi, j: (0, 0, 0)),
            pl.BlockSpec((N_HEADS, HEAD_DIM, KV_RANK), lambda bi, j: (0, 0, 0)),
        ],
        out_specs=pl.BlockSpec((Q_BLOCK, N_HEADS * HEAD_DIM), lambda bi, j: (bi * nq + j, 0)),
        out_shape=jax.ShapeDtypeStruct((b * s, N_HEADS * HEAD_DIM), F32),
        scratch_shapes=[
            pltpu.VMEM((s, Q_BLOCK), jnp.int32),
            pltpu.VMEM((32, s // 32, Q_BLOCK), jnp.int32),
            pltpu.VMEM((KV_RANK, hq), _MXU),
            pltpu.VMEM((2, kb, hq), F32),
            pltpu.VMEM((2, kb, hq), _MXU),
            pltpu.VMEM((KV_RANK + ONES_ROWS, hq), F32),
            pltpu.VMEM((1, hq), F32),
            pltpu.VMEM((1, hq), F32),
        ],
        compiler_params=_cparams(("parallel", "arbitrary")),
        name="dsa_attn",
    )(proj, proj, small, ckv, ckvT, kidx, wuk, wuvT)


def _pool_rows(x, halo, pos, pw_ref, ps_ref):
    tm = x.shape[0]
    xe = jnp.concatenate([halo, x], axis=0)
    outs = []
    for g, w in enumerate(POOL_WINDOWS):
        sl = slice(g * POOL_GROUP, (g + 1) * POOL_GROUP)
        cur = xe[:, sl]
        span = 1
        while span < w:
            n = cur.shape[0]
            cur = cur[span:, :] + cur[:n - span, :]
            span *= 2
        win = cur[cur.shape[0] - tm:, :]
        count = jnp.minimum(pos + 1, w).astype(F32)
        pooled = win / count - x[:, sl]
        outs.append(_dot(pooled.astype(_MXU), pw_ref[g]))
    return jnp.concatenate(outs, axis=1) * ps_ref[...]


def _ssd_kernel(xbc_ref, halo_ref, z_ref, sm_ref, cw_ref, cb_ref, dtb_row_ref, dtb_col_ref, a_row_ref, a_col_ref,
                dsk_x_ref, expand_ref, ng_ref, o_ref, state_ref):
    c = pl.program_id(1)
    L = CHUNK
    P = SSD_HEAD_DIM

    @pl.when(c == 0)
    def _():
        state_ref[...] = jnp.zeros_like(state_ref)

    halo0 = halo_ref[...]
    halo0 = jnp.where(c > 0, halo0, jnp.zeros_like(halo0))
    for sub in range(SSD_SUB):
        rows = slice(sub * L, (sub + 1) * L)
        x_nat = xbc_ref[rows, :]
        halo = halo0 if sub == 0 else xbc_ref[sub * L - CONV_TAIL:sub * L, :]
        o_ref[rows, :] = _ssd_chunk(x_nat, halo, z_ref[rows, :], sm_ref[rows, :], cw_ref, cb_ref, dtb_row_ref,
                                    dtb_col_ref, a_row_ref, a_col_ref, dsk_x_ref, expand_ref, ng_ref, state_ref)


def _ssd_chunk(x_nat, halo, z, sm, cw_ref, cb_ref, dtb_row_ref, dtb_col_ref, a_row_ref, a_col_ref,
               dsk_x_ref, expand_ref, ng_ref, state_ref):
    L = CHUNK
    P = SSD_HEAD_DIM
    xe = jnp.concatenate([halo, x_nat], axis=0)
    cw = cw_ref[...]
    conv = cb_ref[...] + cw[3:4, :] * x_nat.astype(F32)
    srow = lax.broadcasted_iota(jnp.int32, (L, L + CONV_TAIL), 0)
    scol = lax.broadcasted_iota(jnp.int32, (L, L + CONV_TAIL), 1)
    for k in range(CONV_WIDTH - 1):
        sh = CONV_WIDTH - 1 - k
        sel = jnp.where(scol == srow + (CONV_TAIL - sh), 1.0, 0.0)
        conv = conv + cw[k:k + 1, :] * _select_rows(sel, xe)
    xc = _silu(conv)
    xs = xc[:, :D_INNER]
    bm = xc[:, D_INNER:D_INNER + N_GROUPS * D_STATE]
    cm = xc[:, D_INNER + N_GROUPS * D_STATE:]

    smT = sm.T
    dt = jax.nn.softplus(sm[:, SM_DT:SM_DT + SSD_HEADS] + dtb_row_ref[...])
    dtT = jax.nn.softplus(smT[SM_DT:SM_DT + SSD_HEADS, :] + dtb_col_ref[...])
    expand = expand_ref[...]
    dtx = _dot01_right(dt, expand)
    da = dt * a_row_ref[...]
    daT = dtT * a_col_ref[...]
    row = lax.broadcasted_iota(jnp.int32, (L, L), 0)
    col = lax.broadcasted_iota(jnp.int32, (L, L), 1)
    lower = row >= col
    tril = jnp.where(lower, 1.0, 0.0).astype(F32)
    triu = jnp.where(row <= col, 1.0, 0.0).astype(F32)
    acs = _dot01_left(tril, da)
    acsT = _dot01_right(daT, triu)
    acs_x = _dot01_right(acs, expand)
    a_last = acs_x[L - 1:L, :]
    ea_x = jnp.exp(acs_x)
    decay_x = jnp.exp(a_last - acs_x)
    cd_x = jnp.exp(a_last)

    X = xs * dtx
    Xb = X.astype(_MXU)
    Xd = (X * decay_x).astype(_MXU)
    state = state_ref[...]
    ys = []
    hpg = SSD_HEADS // N_GROUPS
    for g in range(N_GROUPS):
        bg = bm[:, g * D_STATE:(g + 1) * D_STATE]
        cg = cm[:, g * D_STATE:(g + 1) * D_STATE].astype(_MXU)
        bgT = bg.T.astype(_MXU)
        cb = _dot(cg, bgT)
        gs = slice(g * hpg * P, (g + 1) * hpg * P)
        y_off = _dot(cg, state[:, gs].astype(_MXU)) * ea_x[:, gs]
        state_ref[:, gs] = state[:, gs] * cd_x[:, gs] + _dot(bgT, Xd[:, gs])
        for e in range(hpg):
            h = g * hpg + e
            hs = slice(h * P, (h + 1) * P)
            diff = acs[:, h:h + 1] - acsT[h:h + 1, :]
            lmat = jnp.exp(jnp.where(lower, diff, -jnp.inf))
            m = (cb * lmat).astype(_MXU)
            ys.append(_dot(m, Xb[:, hs]) + y_off[:, e * P:(e + 1) * P])
    y = jnp.concatenate(ys, axis=1) + xs * dsk_x_ref[...]
    y = y * _silu(z.astype(F32))
    return _rms(y, ng_ref[...])


def _ssd(proj, small, cw, cb, dtb, a_log, dsk, ng, *, b, s):
    blk = CHUNK * SSD_SUB
    nc = s // blk
    rep = lambda v: jnp.repeat(v, SSD_HEAD_DIM)[None, :]
    a = -jnp.exp(a_log)
    expand = jnp.repeat(jnp.eye(SSD_HEADS, dtype=F32), SSD_HEAD_DIM, axis=1)
    const2 = lambda bi, c: (0, 0)
    return pl.pallas_call(
        _ssd_kernel,
        grid=(b, nc),
        in_specs=[
            pl.BlockSpec((blk, CONV_DIM), lambda bi, c: (bi * nc + c, C_XBC // CONV_DIM)),
            pl.BlockSpec((CONV_TAIL, CONV_DIM),
                         lambda bi, c: (jnp.maximum((bi * nc + c) * (blk // CONV_TAIL) - 1, 0), C_XBC // CONV_DIM)),
            pl.BlockSpec((blk, D_INNER), lambda bi, c: (bi * nc + c, C_Z // D_INNER)),
            pl.BlockSpec((blk, 128), lambda bi, c: (bi * nc + c, 0)),
            pl.BlockSpec((CONV_WIDTH, CONV_DIM), const2),
            pl.BlockSpec((1, CONV_DIM), const2),
            pl.BlockSpec((1, SSD_HEADS), const2),
            pl.BlockSpec((SSD_HEADS, 1), const2),
            pl.BlockSpec((1, SSD_HEADS), const2),
            pl.BlockSpec((SSD_HEADS, 1), const2),
            pl.BlockSpec((1, D_INNER), const2),
            pl.BlockSpec((SSD_HEADS, D_INNER), const2),
            pl.BlockSpec((1, D_INNER), const2),
        ],
        out_specs=pl.BlockSpec((blk, D_INNER), lambda bi, c: (bi * nc + c, 0)),
        out_shape=jax.ShapeDtypeStruct((b * s, D_INNER), F32),
        scratch_shapes=[pltpu.VMEM((D_STATE, D_INNER), F32)],
        compiler_params=_cparams(("parallel", "arbitrary")),
        name="ssd",
    )(proj, proj, proj, small, cw, cb[None, :], dtb[None, :], dtb[:, None], a[None, :], a[:, None],
      rep(dsk), expand, ng[None, :])


def _merge_kernel(ya_ref, xp_ref, halo_ref, ys_ref, ga_ref, gp_ref, gs_ref, h_ref, pw_ref, ps_ref,
                  wa_ref, wp_ref, ws_ref, wo_ref, o_ref, *, seq):
    tm = h_ref.shape[0]
    start = (pl.program_id(0) * tm) % seq
    halo = jnp.where(start > 0, halo_ref[...].astype(F32), 0.0)
    pos = start + lax.broadcasted_iota(jnp.int32, (tm, POOL_GROUP), 0)
    yp = _pool_rows(xp_ref[...].astype(F32), halo, pos, pw_ref, ps_ref)
    m = _sigmoid(ga_ref[...].astype(F32)) * _dot(ya_ref[...].astype(_MXU), wa_ref[...])
    m = m + _sigmoid(gp_ref[...].astype(F32)) * _dot(yp.astype(_MXU), wp_ref[...])
    m = m + _sigmoid(gs_ref[...].astype(F32)) * _dot(ys_ref[...].astype(_MXU), ws_ref[...])
    o_ref[...] = h_ref[...] + _dot(m.astype(_MXU), wo_ref[...])


def _merge(ya, ys, proj, h, pw, ps, wa, wp, ws, wo, *, seq, tm):
    t, d = h.shape
    row = lambda i: (i, 0)
    const = lambda i: (0, 0)
    gate = lambda k: (lambda i: (i, C_GATE // d + k))
    return pl.pallas_call(
        functools.partial(_merge_kernel, seq=seq),
        grid=(t // tm,),
        in_specs=[
            pl.BlockSpec((tm, ya.shape[1]), row),
            pl.BlockSpec((tm, POOL_WIDTH), lambda i: (i, C_POOL // POOL_WIDTH)),
            pl.BlockSpec((16, POOL_WIDTH), lambda i: (jnp.maximum(i * (tm // 16) - 1, 0), C_POOL // POOL_WIDTH)),
            pl.BlockSpec((tm, ys.shape[1]), row),
            pl.BlockSpec((tm, d), gate(0)),
            pl.BlockSpec((tm, d), gate(1)),
            pl.BlockSpec((tm, d), gate(2)),
            pl.BlockSpec((tm, d), row),
            pl.BlockSpec(pw.shape, lambda i: (0, 0, 0)),
            pl.BlockSpec(ps.shape, const),
            pl.BlockSpec(wa.shape, const),
            pl.BlockSpec(wp.shape, const),
            pl.BlockSpec(ws.shape, const),
            pl.BlockSpec(wo.shape, const),
        ],
        out_specs=pl.BlockSpec((tm, d), row),
        out_shape=jax.ShapeDtypeStruct((t, d), F32),
        compiler_params=_cparams(("parallel",)),
        name="merge",
    )(ya, proj, proj, ys, proj, proj, proj, h, pw, ps, wa, wp, ws, wo)


def _ple_kernel(h_ref, p_ref, g_ref, wg_ref, wp_ref, fg_ref, o_ref, *, final):
    h = h_ref[...]
    gate = _sigmoid(_dot(_rms(h, g_ref[...]).astype(_MXU), wg_ref[...]))
    h = h + gate * _dot(p_ref[...].astype(_MXU), wp_ref[...])
    if final:
        h = _rms(h, fg_ref[...])
    o_ref[...] = h


def _ple(h, p, g, wg, wp, fg, *, tm, final):
    t, d = h.shape
    row = lambda i: (i, 0)
    const = lambda i: (0, 0)
    return pl.pallas_call(
        functools.partial(_ple_kernel, final=final),
        grid=(t // tm,),
        in_specs=[
            pl.BlockSpec((tm, d), row),
            pl.BlockSpec((tm, p.shape[1]), row),
            pl.BlockSpec((1, d), const),
            pl.BlockSpec(wg.shape, const),
            pl.BlockSpec(wp.shape, const),
            pl.BlockSpec((1, d), const),
        ],
        out_specs=pl.BlockSpec((tm, d), row),
        out_shape=jax.ShapeDtypeStruct((t, d), F32),
        compiler_params=_cparams(("parallel",)),
        name="ple",
    )(h, p, g, wg, wp, fg)


def _reorder_w_in(w):
    o_q, o_ckv, o_qidx, o_widx, o_kidx, o_pool, o_z, o_xbc, o_dt, o_gate = (
        0, 512, 768, 1280, 1288, 1352, 1864, 2888, 4424, 4440)
    d = w.shape[0]
    small = jnp.concatenate([
        w[:, o_kidx:o_kidx + IDX_DIM], w[:, o_widx:o_widx + IDX_HEADS], w[:, o_dt:o_dt + SSD_HEADS],
        jnp.zeros((d, 128 - IDX_DIM - IDX_HEADS - SSD_HEADS), w.dtype)], axis=1)
    wb = w.astype(_MXU)
    used = CONV_DIM + 512 + D_INNER + 3 * D_MODEL + 512 + POOL_WIDTH + KV_RANK
    wide = jnp.concatenate([
        wb[:, o_xbc:o_xbc + CONV_DIM], wb[:, o_q:o_q + 512], wb[:, o_z:o_z + D_INNER],
        wb[:, o_gate:o_gate + 3 * D_MODEL], wb[:, o_qidx:o_qidx + 512], wb[:, o_pool:o_pool + POOL_WIDTH],
        wb[:, o_ckv:o_ckv + KV_RANK], jnp.zeros((d, PROJ_COLS - used), _MXU)], axis=1)
    return wide, small.astype(_MXU)


def kernel(x, p, ffn1_norm, ffn1_w_gate, ffn1_w_up, ffn1_w_down, mix_norm, w_in, kv_norm, idx_k_norm, w_uk, w_uv, pool_w, pool_scale, conv_w, conv_b, dt_bias, a_log, d_skip, ssd_norm, w_br_attn, w_br_pool, w_br_ssd, w_out, ffn2_norm, ffn2_w_gate, ffn2_w_up, ffn2_w_down, ple_norm, ple_w_gate, ple_w_proj, final_norm):
    b, s, d = x.shape
    depth = w_in.shape[0]
    t = b * s
    tm = min(1024, t)
    tm_small = min(512, t)
    kb = min(512, s)
    mx = lambda a: a.astype(_MXU)

    h = x.reshape(t, d)
    for i in range(depth):
        h = _ffn(h, ffn1_norm[i][None], mx(ffn1_w_gate[i]), mx(ffn1_w_up[i]), mx(ffn1_w_down[i]),
                 tm=tm_small)
        w_wide, w_small = _reorder_w_in(w_in[i])
        proj, small, ckv, ckvT, kidx = _inproj(h, mix_norm[i][None], w_wide, w_small, kv_norm[i][None],
                                               idx_k_norm[i][None], b=b, s=s, tm=kb, tn=2560)
        wuk = mx(jnp.transpose(w_uk[i], (1, 0, 2)))
        wuvT = mx(jnp.transpose(w_uv[i], (1, 2, 0)))
        y_attn = _attention(proj, small, ckv, ckvT, kidx, wuk, wuvT, b=b, s=s, kb=kb)
        y_ssd = _ssd(proj, small, conv_w[i], conv_b[i], dt_bias[i], a_log[i], d_skip[i], ssd_norm[i], b=b, s=s)
        h = _merge(y_attn, y_ssd, proj, h, mx(pool_w[i]), pool_scale[i][None], mx(w_br_attn[i]),
                   mx(w_br_pool[i]), mx(w_br_ssd[i]), mx(w_out[i]), seq=s, tm=tm_small)
        h = _ffn(h, ffn2_norm[i][None], mx(ffn2_w_gate[i]), mx(ffn2_w_up[i]), mx(ffn2_w_down[i]),
                 tm=tm_small)
        h = _ple(h, p[i].reshape(t, -1), ple_norm[i][None], mx(ple_w_gate[i]), mx(ple_w_proj[i]),
                 final_norm[None], tm=tm, final=(i == depth - 1))
    return h.reshape(b, s, d)
```

```python
import functools

import jax
import jax.numpy as jnp
from jax import lax
from jax.experimental import pallas as pl
from jax.experimental.pallas import tpu as pltpu

_MXU = jnp.bfloat16
F32 = jnp.float32

D_MODEL = 1024
N_HEADS = 8
HEAD_DIM = 64
KV_RANK = 256
IDX_HEADS = 8
IDX_DIM = 64
TOPK_MAX = 256
Q_BLOCK = 256
POOL_WINDOWS = (2, 4, 8, 16)
POOL_WIDTH = 512
POOL_GROUP = 128
SSD_HEADS = 16
SSD_HEAD_DIM = 64
D_INNER = 1024
N_GROUPS = 2
D_STATE = 128
CONV_WIDTH = 4
CONV_DIM = D_INNER + 2 * N_GROUPS * D_STATE
CHUNK = 128
D_FF = 2816
PLE_DIM = 256
EPS = 1e-6

C_XBC = 0
C_Q = 1536
C_Z = 2048
C_GATE = 3072
C_QIDX = 6144
C_POOL = 6656
C_CKV = 7168
PROJ_COLS = 7680
SM_KIDX = 0
SM_WIDX = 64
SM_DT = 72

INT_MIN = -2147483648
LOG2E = 1.4426950408889634
RC = 32
ONES_ROWS = 16
SSD_SUB = 8
CONV_TAIL = 16
NEG_BIG = -1e30
VMEM_LIMIT = 56 * 1024 * 1024


def _cparams(sem):
    return pltpu.CompilerParams(dimension_semantics=sem, vmem_limit_bytes=VMEM_LIMIT)


def _rms(x, g):
    return x * lax.rsqrt(jnp.mean(x * x, axis=-1, keepdims=True) + EPS) * g


def _dot(a, b):
    return jnp.dot(a, b, preferred_element_type=F32)


def _sigmoid(x):
    return 0.5 * jnp.tanh(0.5 * x) + 0.5


def _silu(x):
    h = 0.5 * x
    return h * jnp.tanh(h) + h


def _split3(a):
    hi = a.astype(jnp.bfloat16)
    r1 = a - hi.astype(F32)
    mid = r1.astype(jnp.bfloat16)
    lo = (r1 - mid.astype(F32)).astype(jnp.bfloat16)
    return hi, mid, lo


def _dot01_right(a, b01):
    b = b01.astype(jnp.bfloat16)
    hi, mid, lo = _split3(a)
    return _dot(hi, b) + _dot(mid, b) + _dot(lo, b)


def _select_rows(a01, b):
    if b.dtype == jnp.bfloat16:
        return _dot(a01.astype(jnp.bfloat16), b)
    return _dot01_left(a01, b)


def _dot01_left(a01, b):
    a = a01.astype(jnp.bfloat16)
    hi, mid, lo = _split3(b)
    return _dot(a, hi) + _dot(a, mid) + _dot(a, lo)


def _ffn_kernel(h_ref, g_ref, wg_ref, wu_ref, wd_ref, *rest, ple, final):
    o_ref = rest[-1]
    h = h_ref[...]
    xn = _rms(h, g_ref[...]).astype(_MXU)
    a = _silu(_dot(xn, wg_ref[...])) * _dot(xn, wu_ref[...])
    h = h + 0.5 * _dot(a.astype(_MXU), wd_ref[...])
    if ple:
        p_ref, gp_ref, wpg_ref, wpp_ref, fg_ref = rest[:-1]
        gate = _sigmoid(_dot(_rms(h, gp_ref[...]).astype(_MXU), wpg_ref[...]))
        h = h + gate * _dot(p_ref[...].astype(_MXU), wpp_ref[...])
        if final:
            h = _rms(h, fg_ref[...])
    o_ref[...] = h


def _ffn(h, g, wg, wu, wd, ple=None, *, tm, final=False):
    t, d = h.shape
    ff = wg.shape[1]
    once = dict(pipeline_mode=pl.Buffered(1))
    row = lambda i: (i, 0)
    const = lambda i: (0, 0)
    in_specs = [
        pl.BlockSpec((tm, d), row),
        pl.BlockSpec((1, d), const),
        pl.BlockSpec((d, ff), const, **once),
        pl.BlockSpec((d, ff), const, **once),
        pl.BlockSpec((ff, d), const, **once),
    ]
    args = [h, g, wg, wu, wd]
    if ple is not None:
        p, gp, wpg, wpp, fg = ple
        in_specs += [
            pl.BlockSpec((tm, p.shape[1]), row),
            pl.BlockSpec((1, d), const),
            pl.BlockSpec(wpg.shape, const, **once),
            pl.BlockSpec(wpp.shape, const, **once),
            pl.BlockSpec((1, d), const),
        ]
        args += [p, gp, wpg, wpp, fg]
    return pl.pallas_call(
        functools.partial(_ffn_kernel, ple=ple is not None, final=final),
        grid=(t // tm,),
        in_specs=in_specs,
        out_specs=pl.BlockSpec((tm, d), row),
        out_shape=jax.ShapeDtypeStruct((t, d), F32),
        compiler_params=_cparams(("parallel",)),
        name="ffn",
    )(*args)


def _inproj_kernel(h_ref, g_ref, w_ref, ws_ref, gkv_ref, gk_ref, o_ref, os_ref, ckv_o, ckvT_o, kidx_o, *, tn):
    xn = _rms(h_ref[...], g_ref[...]).astype(_MXU)
    small = _dot(xn, ws_ref[...])
    os_ref[...] = small
    for j in range(w_ref.shape[1] // tn):
        res = _dot(xn, w_ref[:, j * tn:(j + 1) * tn])
        o_ref[:, j * tn:(j + 1) * tn] = res.astype(o_ref.dtype)
        if j * tn <= C_CKV and C_CKV + KV_RANK <= (j + 1) * tn:
            c_raw = res[:, C_CKV - j * tn:C_CKV - j * tn + KV_RANK]
    c = _rms(c_raw, gkv_ref[...])
    ckv_o[0] = c.astype(ckv_o.dtype)
    ones = jnp.ones((ONES_ROWS, c.shape[0]), F32)
    ckvT_o[0, 0] = jnp.concatenate([c.T, ones], axis=0).astype(ckvT_o.dtype)
    kidx_o[0] = _rms(small[:, SM_KIDX:SM_KIDX + IDX_DIM], gk_ref[...]).astype(kidx_o.dtype)


def _inproj(h, g, w, ws, gkv, gk, *, b, s, tm, tn):
    t, d = h.shape
    n = w.shape[1]
    nk = s // tm
    once = dict(pipeline_mode=pl.Buffered(1))
    return pl.pallas_call(
        functools.partial(_inproj_kernel, tn=tn),
        grid=(t // tm,),
        in_specs=[
            pl.BlockSpec((tm, d), lambda i: (i, 0)),
            pl.BlockSpec((1, d), lambda i: (0, 0)),
            pl.BlockSpec((d, n), lambda i: (0, 0), **once),
            pl.BlockSpec((d, 128), lambda i: (0, 0), **once),
            pl.BlockSpec((1, KV_RANK), lambda i: (0, 0)),
            pl.BlockSpec((1, IDX_DIM), lambda i: (0, 0)),
        ],
        out_specs=[
            pl.BlockSpec((tm, n), lambda i: (i, 0)),
            pl.BlockSpec((tm, 128), lambda i: (i, 0)),
            pl.BlockSpec((1, tm, KV_RANK), lambda i: (i // nk, i % nk, 0)),
            pl.BlockSpec((1, 1, KV_RANK + ONES_ROWS, tm), lambda i: (i // nk, i % nk, 0, 0)),
            pl.BlockSpec((1, tm, IDX_DIM), lambda i: (i // nk, i % nk, 0)),
        ],
        out_shape=[
            jax.ShapeDtypeStruct((t, n), _MXU),
            jax.ShapeDtypeStruct((t, 128), F32),
            jax.ShapeDtypeStruct((b, s, KV_RANK), _MXU),
            jax.ShapeDtypeStruct((b, nk, KV_RANK + ONES_ROWS, tm), _MXU),
            jax.ShapeDtypeStruct((b, s, IDX_DIM), _MXU),
        ],
        compiler_params=_cparams(("parallel",)),
        name="inproj",
    )(h, g, w, ws, gkv, gk)


def _bit_transpose32(words):
    words = list(words)
    j, m = 16, 0x0000FFFF
    while j:
        mi = jnp.int32(m - (1 << 32) if m >= (1 << 31) else m)
        for k in range(32):
            if k & j == 0:
                t = (words[k] ^ lax.shift_right_logical(words[k + j], jnp.int32(j))) & mi
                words[k] = words[k] ^ t
                words[k + j] = words[k + j] ^ lax.shift_left(t, jnp.int32(j))
        j >>= 1
        m = (m ^ (m << j)) & 0xFFFFFFFF
    return words


def _attn_kernel(q_ref, qi_ref, sm_ref, ckv_ref, ckvT_ref, kidx_ref, wuk_ref, wuvT_ref, o_ref,
                 key_ref, planes_ref, qlat_ref, s_ref, p_ref, acc_ref, m_ref, alpha_ref,
                 *, k_top, kb):
    j = pl.program_id(1)
    nblk = (j * Q_BLOCK + Q_BLOCK + kb - 1) // kb
    nh = N_HEADS
    qb = Q_BLOCK

    qT = q_ref[...].astype(F32).T
    qiT = qi_ref[...].astype(F32).T
    smT = sm_ref[...].T
    w_row = jnp.concatenate([smT[SM_WIDX + h:SM_WIDX + h + 1, :] for h in range(nh)], axis=1)
    qiT_stack = jnp.concatenate(
        [qiT[h * IDX_DIM:(h + 1) * IDX_DIM, :] for h in range(nh)], axis=1).astype(_MXU)
    scale = HEAD_DIM ** -0.5
    qlatT = jnp.concatenate(
        [_dot(wuk_ref[h], qT[h * HEAD_DIM:(h + 1) * HEAD_DIM, :].astype(_MXU)) for h in range(nh)],
        axis=1)
    qlat_ref[...] = (qlatT * (scale * LOG2E)).astype(_MXU)
    s_ref[0] = _dot(ckv_ref[0, 0:kb, :], qlat_ref[...])

    qpos = j * qb + lax.broadcasted_iota(jnp.int32, (kb, qb), 1)

    def idx_block(bi):
        off = pl.multiple_of(bi * kb, kb)
        kblk = kidx_ref[0, pl.ds(off, kb), :]
        r = jnp.maximum(_dot(kblk, qiT_stack), 0.0) * w_row
        sc = r[:, 0:qb]
        for h in range(1, nh):
            sc = sc + r[:, h * qb:(h + 1) * qb]
        bits = pltpu.bitcast(sc, jnp.int32)
        key = jnp.where(bits < 0, bits ^ jnp.int32(0x7FFFFFFF), bits)
        kpos = off + lax.broadcasted_iota(jnp.int32, (kb, qb), 0)
        key = jnp.where(kpos <= qpos, key, jnp.int32(INT_MIN))
        key_ref[pl.ds(off, kb), :] = key
        u = key ^ jnp.int32(INT_MIN)
        for g in range(kb // 256):
            words = _bit_transpose32([u[g * 256 + 8 * t:g * 256 + 8 * t + 8, :] for t in range(32)])
            row = pl.multiple_of((bi * (kb // 256) + g) * 8, 8)
            for i in range(32):
                planes_ref[i, pl.ds(row, 8), :] = words[i]

    def idx_pair(i2, carry):
        idx_block(2 * i2)
        idx_block(2 * i2 + 1)
        return carry

    lax.fori_loop(0, nblk // 2, idx_pair, 0)

    @pl.when(nblk % 2 == 1)
    def _():
        idx_block(nblk - 1)

    ngrp_all = planes_ref.shape[1] // 8
    ngrp_live = nblk * (kb // 256)

    def radix_select(ng):
        rows = ng * 8
        grp = lax.broadcasted_iota(jnp.int32, (rows, qb), 0) // 8
        eq0 = jnp.where(grp < ngrp_live, jnp.int32(-1), jnp.int32(0))

        def count(words):
            c = jnp.sum(lax.population_count(words).reshape(ng, 8, qb), axis=0)
            return jnp.sum(c, axis=0, keepdims=True)

        def bit_body(i, carry):
            eq, n_above, prefix = carry
            x1 = planes_ref[2 * i, 0:rows, :]
            x0 = planes_ref[2 * i + 1, 0:rows, :]
            e1 = eq & x1
            n1, n11, nx0 = count(e1), count(e1 & x0), count(eq & x0)
            take1 = (n_above + n1) >= k_top
            n_above = jnp.where(take1, n_above, n_above + n1)
            n_lo = jnp.where(take1, n11, nx0 - n11)
            take0 = (n_above + n_lo) >= k_top
            n_above = jnp.where(take0, n_above, n_above + n_lo)
            flip1 = jnp.where(take1, jnp.int32(0), jnp.int32(-1))
            flip0 = jnp.where(take0, jnp.int32(0), jnp.int32(-1))
            eq = eq & (x1 ^ flip1) & (x0 ^ flip0)
            two = jnp.where(take1, 2, 0) | jnp.where(take0, 1, 0)
            return eq, n_above, prefix | jnp.left_shift(two, 30 - 2 * i)

        zrow = jnp.zeros((1, qb), jnp.int32)
        return lax.fori_loop(0, 16, bit_body, (eq0, zrow, zrow))[2]

    quarter = max(ngrp_all // 4, 1)
    sizes = list(range(quarter, ngrp_all + 1, quarter))
    which = jnp.minimum((ngrp_live - 1) // quarter, len(sizes) - 1)
    prefix = lax.switch(which, [functools.partial(radix_select, ng) for ng in sizes])
    thr = jnp.maximum(prefix ^ jnp.int32(INT_MIN), jnp.int32(INT_MIN + 1))
    thr_b = jnp.broadcast_to(thr, (RC, qb))

    m_ref[...] = jnp.full(m_ref.shape, NEG_BIG, F32)
    acc_ref[...] = jnp.zeros_like(acc_ref)

    def qk(bi, slot):
        off = pl.multiple_of(jnp.minimum(bi, nblk - 1) * kb, kb)
        s_ref[slot] = _dot(ckv_ref[0, pl.ds(off, kb), :], qlat_ref[...])

    def pv(bi, slot):
        acc_ref[...] = alpha_ref[...] * acc_ref[...] + _dot(ckvT_ref[0, jnp.maximum(bi, 0)], p_ref[slot])

    def softmax(bi, slot):
        off = pl.multiple_of(bi * kb, kb)
        nch = kb // RC

        def masks(c):
            return key_ref[pl.ds(off + c * RC, RC), :] >= thr_b

        mx = [jnp.full((8, qb), NEG_BIG, F32)] * nh
        for c in range(nch):
            mk = masks(c)
            for h in range(nh):
                sc = jnp.where(mk, s_ref[slot, c * RC:(c + 1) * RC, h * qb:(h + 1) * qb], NEG_BIG)
                mx[h] = jnp.maximum(mx[h], jnp.max(sc.reshape(RC // 8, 8, qb), axis=0))
        m_new = []
        for h in range(nh):
            hs = slice(h * qb, (h + 1) * qb)
            m_old = m_ref[:, hs]
            mn = jnp.maximum(m_old, jnp.max(mx[h], axis=0, keepdims=True))
            alpha_ref[:, hs] = jnp.exp2(m_old - mn)
            m_ref[:, hs] = mn
            m_new.append(jnp.broadcast_to(mn, (RC, qb)))
        for c in range(nch):
            mk = masks(c)
            for h in range(nh):
                sc = s_ref[slot, c * RC:(c + 1) * RC, h * qb:(h + 1) * qb]
                pr = jnp.where(mk, jnp.exp2(sc - m_new[h]), 0.0)
                p_ref[slot, c * RC:(c + 1) * RC, h * qb:(h + 1) * qb] = pr.astype(p_ref.dtype)

    def step(bi, cur):
        qk(bi + 1, 1 - cur)
        pv(bi - 1, 1 - cur)
        softmax(bi, cur)

    qk(1, 1)
    softmax(0, 0)

    def pair_body(i2, carry):
        step(2 * i2 + 1, 1)
        step(2 * i2 + 2, 0)
        return carry

    lax.fori_loop(0, (nblk - 1) // 2, pair_body, 0)
    even = nblk % 2 == 0

    @pl.when(even)
    def _():
        pv(nblk - 2, 0)
        softmax(nblk - 1, 1)
        pv(nblk - 1, 1)

    @pl.when(jnp.logical_not(even))
    def _():
        pv(nblk - 1, 0)

    inv_l = 1.0 / acc_ref[KV_RANK:KV_RANK + 1, :]
    o = (acc_ref[:KV_RANK, :] * inv_l).astype(_MXU)
    yT = jnp.concatenate([_dot(wuvT_ref[h], o[:, h * qb:(h + 1) * qb]) for h in range(nh)], axis=0)
    o_ref[...] = yT.T


def _attention(proj, small, ckv, ckvT, kidx, wuk, wuvT, *, b, s, kb):
    nq = s // Q_BLOCK
    k_top = min(TOPK_MAX, s // 4)
    kern = functools.partial(_attn_kernel, k_top=k_top, kb=kb)
    hq = N_HEADS * Q_BLOCK
    return pl.pallas_call(
        kern,
        grid=(b, nq),
        in_specs=[
            pl.BlockSpec((Q_BLOCK, 512), lambda bi, j: (bi * nq + j, C_Q // 512)),
            pl.BlockSpec((Q_BLOCK, 512), lambda bi, j: (bi * nq + j, C_QIDX // 512)),
            pl.BlockSpec((Q_BLOCK, 128), lambda bi, j: (bi * nq + j, 0)),
            pl.BlockSpec((1, s, KV_RANK), lambda bi, j: (bi, 0, 0)),
            pl.BlockSpec((1, s // kb, KV_RANK + ONES_ROWS, kb), lambda bi, j: (bi, 0, 0, 0)),
            pl.BlockSpec((1, s, IDX_DIM), lambda bi, j: (bi, 0, 0)),
            pl.BlockSpec((N_HEADS, KV_RANK, HEAD_DIM), lambda bi, j: (0, 0, 0)),
            pl.BlockSpec((N_HEADS, HEAD_DIM, KV_RANK), lambda bi, j: (0, 0, 0)),
        ],
        out_specs=pl.BlockSpec((Q_BLOCK, N_HEADS * HEAD_DIM), lambda bi, j: (bi * nq + j, 0)),
        out_shape=jax.ShapeDtypeStruct((b * s, N_HEADS * HEAD_DIM), F32),
        scratch_shapes=[
            pltpu.VMEM((s, Q_BLOCK), jnp.int32),
            pltpu.VMEM((32, s // 32, Q_BLOCK), jnp.int32),
            pltpu.VMEM((KV_RANK, hq), _MXU),
            pltpu.VMEM((2, kb, hq), F32),
            pltpu.VMEM((2, kb, hq), _MXU),
            pltpu.VMEM((KV_RANK + ONES_ROWS, hq), F32),
            pltpu.VMEM((1, hq), F32),
            pltpu.VMEM((1, hq), F32),
        ],
        compiler_params=_cparams(("parallel", "arbitrary")),
        name="dsa_attn",
    )(proj, proj, small, ckv, ckvT, kidx, wuk, wuvT)


def _pool_rows(x, halo, pos, pw_ref, ps_ref):
    tm = x.shape[0]
    xe = jnp.concatenate([halo, x], axis=0)
    outs = []
    for g, w in enumerate(POOL_WINDOWS):
        sl = slice(g * POOL_GROUP, (g + 1) * POOL_GROUP)
        cur = xe[:, sl]
        span = 1
        while span < w:
            n = cur.shape[0]
            cur = cur[span:, :] + cur[:n - span, :]
            span *= 2
        win = cur[cur.shape[0] - tm:, :]
        count = jnp.minimum(pos + 1, w).astype(F32)
        pooled = win / count - x[:, sl]
        outs.append(_dot(pooled.astype(_MXU), pw_ref[g]))
    return jnp.concatenate(outs, axis=1) * ps_ref[...]


def _ssd_kernel(xbc_ref, halo_ref, z_ref, sm_ref, cw_ref, cb_ref, dtb_row_ref, dtb_col_ref, a_row_ref, a_col_ref,
                dsk_x_ref, expand_ref, ng_ref, o_ref, state_ref):
    c = pl.program_id(1)
    L = CHUNK
    P = SSD_HEAD_DIM

    @pl.when(c == 0)
    def _():
        state_ref[...] = jnp.zeros_like(state_ref)

    halo0 = halo_ref[...]
    halo0 = jnp.where(c > 0, halo0, jnp.zeros_like(halo0))
    for sub in range(SSD_SUB):
        rows = slice(sub * L, (sub + 1) * L)
        x_nat = xbc_ref[rows, :]
        halo = halo0 if sub == 0 else xbc_ref[sub * L - CONV_TAIL:sub * L, :]
        o_ref[rows, :] = _ssd_chunk(x_nat, halo, z_ref[rows, :], sm_ref[rows, :], cw_ref, cb_ref, dtb_row_ref,
                                    dtb_col_ref, a_row_ref, a_col_ref, dsk_x_ref, expand_ref, ng_ref, state_ref)


def _ssd_chunk(x_nat, halo, z, sm, cw_ref, cb_ref, dtb_row_ref, dtb_col_ref, a_row_ref, a_col_ref,
               dsk_x_ref, expand_ref, ng_ref, state_ref):
    L = CHUNK
    P = SSD_HEAD_DIM
    xe = jnp.concatenate([halo, x_nat], axis=0)
    cw = cw_ref[...]
    conv = cb_ref[...] + cw[3:4, :] * x_nat.astype(F32)
    srow = lax.broadcasted_iota(jnp.int32, (L, L + CONV_TAIL), 0)
    scol = lax.broadcasted_iota(jnp.int32, (L, L + CONV_TAIL), 1)
    for k in range(CONV_WIDTH - 1):
        sh = CONV_WIDTH - 1 - k
        sel = jnp.where(scol == srow + (CONV_TAIL - sh), 1.0, 0.0)
        conv = conv + cw[k:k + 1, :] * _select_rows(sel, xe)
    xc = _silu(conv)
    xs = xc[:, :D_INNER]
    bm = xc[:, D_INNER:D_INNER + N_GROUPS * D_STATE]
    cm = xc[:, D_INNER + N_GROUPS * D_STATE:]

    smT = sm.T
    dt = jax.nn.softplus(sm[:, SM_DT:SM_DT + SSD_HEADS] + dtb_row_ref[...])
    dtT = jax.nn.softplus(smT[SM_DT:SM_DT + SSD_HEADS, :] + dtb_col_ref[...])
    expand = expand_ref[...]
    dtx = _dot01_right(dt, expand)
    da = dt * a_row_ref[...]
    daT = dtT * a_col_ref[...]
    row = lax.broadcasted_iota(jnp.int32, (L, L), 0)
    col = lax.broadcasted_iota(jnp.int32, (L, L), 1)
    lower = row >= col
    tril = jnp.where(lower, 1.0, 0.0).astype(F32)
    triu = jnp.where(row <= col, 1.0, 0.0).astype(F32)
    acs = _dot01_left(tril, da)
    acsT = _dot01_right(daT, triu)
    acs_x = _dot01_right(acs, expand)
    a_last = acs_x[L - 1:L, :]
    ea_x = jnp.exp(acs_x)
    decay_x = jnp.exp(a_last - acs_x)
    cd_x = jnp.exp(a_last)

    X = xs * dtx
    Xb = X.astype(_MXU)
    Xd = (X * decay_x).astype(_MXU)
    state = state_ref[...]
    ys = []
    hpg = SSD_HEADS // N_GROUPS
    for g in range(N_GROUPS):
        bg = bm[:, g * D_STATE:(g + 1) * D_STATE]
        cg = cm[:, g * D_STATE:(g + 1) * D_STATE].astype(_MXU)
        bgT = bg.T.astype(_MXU)
        cb = _dot(cg, bgT)
        gs = slice(g * hpg * P, (g + 1) * hpg * P)
        y_off = _dot(cg, state[:, gs].astype(_MXU)) * ea_x[:, gs]
        state_ref[:, gs] = state[:, gs] * cd_x[:, gs] + _dot(bgT, Xd[:, gs])
        for e in range(hpg):
            h = g * hpg + e
            hs = slice(h * P, (h + 1) * P)
            diff = acs[:, h:h + 1] - acsT[h:h + 1, :]
            lmat = jnp.exp(jnp.where(lower, diff, -jnp.inf))
            m = (cb * lmat).astype(_MXU)
            ys.append(_dot(m, Xb[:, hs]) + y_off[:, e * P:(e + 1) * P])
    y = jnp.concatenate(ys, axis=1) + xs * dsk_x_ref[...]
    y = y * _silu(z.astype(F32))
    return _rms(y, ng_ref[...])


def _ssd(proj, small, cw, cb, dtb, a_log, dsk, ng, *, b, s):
    blk = CHUNK * SSD_SUB
    nc = s // blk
    rep = lambda v: jnp.repeat(v, SSD_HEAD_DIM)[None, :]
    a = -jnp.exp(a_log)
    expand = jnp.repeat(jnp.eye(SSD_HEADS, dtype=F32), SSD_HEAD_DIM, axis=1)
    const2 = lambda bi, c: (0, 0)
    return pl.pallas_call(
        _ssd_kernel,
        grid=(b, nc),
        in_specs=[
            pl.BlockSpec((blk, CONV_DIM), lambda bi, c: (bi * nc + c, C_XBC // CONV_DIM)),
            pl.BlockSpec((CONV_TAIL, CONV_DIM),
                         lambda bi, c: (jnp.maximum((bi * nc + c) * (blk // CONV_TAIL) - 1, 0), C_XBC // CONV_DIM)),
            pl.BlockSpec((blk, D_INNER), lambda bi, c: (bi * nc + c, C_Z // D_INNER)),
            pl.BlockSpec((blk, 128), lambda bi, c: (bi * nc + c, 0)),
            pl.BlockSpec((CONV_WIDTH, CONV_DIM), const2),
            pl.BlockSpec((1, CONV_DIM), const2),
            pl.BlockSpec((1, SSD_HEADS), const2),
            pl.BlockSpec((SSD_HEADS, 1), const2),
            pl.BlockSpec((1, SSD_HEADS), const2),
            pl.BlockSpec((SSD_HEADS, 1), const2),
            pl.BlockSpec((1, D_INNER), const2),
            pl.BlockSpec((SSD_HEADS, D_INNER), const2),
            pl.BlockSpec((1, D_INNER), const2),
        ],
        out_specs=pl.BlockSpec((blk, D_INNER), lambda bi, c: (bi * nc + c, 0)),
        out_shape=jax.ShapeDtypeStruct((b * s, D_INNER), F32),
        scratch_shapes=[pltpu.VMEM((D_STATE, D_INNER), F32)],
        compiler_params=_cparams(("parallel", "arbitrary")),
        name="ssd",
    )(proj, proj, proj, small, cw, cb[None, :], dtb[None, :], dtb[:, None], a[None, :], a[:, None],
      rep(dsk), expand, ng[None, :])


def _merge_kernel(ya_ref, xp_ref, halo_ref, ys_ref, ga_ref, gp_ref, gs_ref, h_ref, pw_ref, ps_ref,
                  wa_ref, wp_ref, ws_ref, wo_ref, o_ref, *, seq):
    tm = h_ref.shape[0]
    start = (pl.program_id(0) * tm) % seq
    halo = jnp.where(start > 0, halo_ref[...].astype(F32), 0.0)
    pos = start + lax.broadcasted_iota(jnp.int32, (tm, POOL_GROUP), 0)
    yp = _pool_rows(xp_ref[...].astype(F32), halo, pos, pw_ref, ps_ref)
    m = _sigmoid(ga_ref[...].astype(F32)) * _dot(ya_ref[...].astype(_MXU), wa_ref[...])
    m = m + _sigmoid(gp_ref[...].astype(F32)) * _dot(yp.astype(_MXU), wp_ref[...])
    m = m + _sigmoid(gs_ref[...].astype(F32)) * _dot(ys_ref[...].astype(_MXU), ws_ref[...])
    o_ref[...] = h_ref[...] + _dot(m.astype(_MXU), wo_ref[...])


def _merge(ya, ys, proj, h, pw, ps, wa, wp, ws, wo, *, seq, tm):
    t, d = h.shape
    row = lambda i: (i, 0)
    const = lambda i: (0, 0)
    gate = lambda k: (lambda i: (i, C_GATE // d + k))
    return pl.pallas_call(
        functools.partial(_merge_kernel, seq=seq),
        grid=(t // tm,),
        in_specs=[
            pl.BlockSpec((tm, ya.shape[1]), row),
            pl.BlockSpec((tm, POOL_WIDTH), lambda i: (i, C_POOL // POOL_WIDTH)),
            pl.BlockSpec((16, POOL_WIDTH), lambda i: (jnp.maximum(i * (tm // 16) - 1, 0), C_POOL // POOL_WIDTH)),
            pl.BlockSpec((tm, ys.shape[1]), row),
            pl.BlockSpec((tm, d), gate(0)),
            pl.BlockSpec((tm, d), gate(1)),
            pl.BlockSpec((tm, d), gate(2)),
            pl.BlockSpec((tm, d), row),
            pl.BlockSpec(pw.shape, lambda i: (0, 0, 0)),
            pl.BlockSpec(ps.shape, const),
            pl.BlockSpec(wa.shape, const),
            pl.BlockSpec(wp.shape, const),
            pl.BlockSpec(ws.shape, const),
            pl.BlockSpec(wo.shape, const),
        ],
        out_specs=pl.BlockSpec((tm, d), row),
        out_shape=jax.ShapeDtypeStruct((t, d), F32),
        compiler_params=_cparams(("parallel",)),
        name="merge",
    )(ya, proj, proj, ys, proj, proj, proj, h, pw, ps, wa, wp, ws, wo)


def _reorder_w_in(w):
    o_q, o_ckv, o_qidx, o_widx, o_kidx, o_pool, o_z, o_xbc, o_dt, o_gate = (
        0, 512, 768, 1280, 1288, 1352, 1864, 2888, 4424, 4440)
    d = w.shape[0]
    small = jnp.concatenate([
        w[:, o_kidx:o_kidx + IDX_DIM], w[:, o_widx:o_widx + IDX_HEADS], w[:, o_dt:o_dt + SSD_HEADS],
        jnp.zeros((d, 128 - IDX_DIM - IDX_HEADS - SSD_HEADS), w.dtype)], axis=1)
    wb = w.astype(_MXU)
    used = CONV_DIM + 512 + D_INNER + 3 * D_MODEL + 512 + POOL_WIDTH + KV_RANK
    wide = jnp.concatenate([
        wb[:, o_xbc:o_xbc + CONV_DIM], wb[:, o_q:o_q + 512], wb[:, o_z:o_z + D_INNER],
        wb[:, o_gate:o_gate + 3 * D_MODEL], wb[:, o_qidx:o_qidx + 512], wb[:, o_pool:o_pool + POOL_WIDTH],
        wb[:, o_ckv:o_ckv + KV_RANK], jnp.zeros((d, PROJ_COLS - used), _MXU)], axis=1)
    return wide, small.astype(_MXU)


def kernel(x, p, ffn1_norm, ffn1_w_gate, ffn1_w_up, ffn1_w_down, mix_norm, w_in, kv_norm, idx_k_norm, w_uk, w_uv, pool_w, pool_scale, conv_w, conv_b, dt_bias, a_log, d_skip, ssd_norm, w_br_attn, w_br_pool, w_br_ssd, w_out, ffn2_norm, ffn2_w_gate, ffn2_w_up, ffn2_w_down, ple_norm, ple_w_gate, ple_w_proj, final_norm):
    b, s, d = x.shape
    depth = w_in.shape[0]
    t = b * s
    tm_small = min(512, t)
    kb = min(512, s)
    mx = lambda a: a.astype(_MXU)

    h = x.reshape(t, d)
    for i in range(depth):
        h = _ffn(h, ffn1_norm[i][None], mx(ffn1_w_gate[i]), mx(ffn1_w_up[i]), mx(ffn1_w_down[i]),
                 tm=tm_small)
        w_wide, w_small = _reorder_w_in(w_in[i])
        proj, small, ckv, ckvT, kidx = _inproj(h, mix_norm[i][None], w_wide, w_small, kv_norm[i][None],
                                               idx_k_norm[i][None], b=b, s=s, tm=kb, tn=2560)
        wuk = mx(jnp.transpose(w_uk[i], (1, 0, 2)))
        wuvT = mx(jnp.transpose(w_uv[i], (1, 2, 0)))
        y_attn = _attention(proj, small, ckv, ckvT, kidx, wuk, wuvT, b=b, s=s, kb=kb)
        y_ssd = _ssd(proj, small, conv_w[i], conv_b[i], dt_bias[i], a_log[i], d_skip[i], ssd_norm[i], b=b, s=s)
        h = _merge(y_attn, y_ssd, proj, h, mx(pool_w[i]), pool_scale[i][None], mx(w_br_attn[i]),
                   mx(w_br_pool[i]), mx(w_br_ssd[i]), mx(w_out[i]), seq=s, tm=tm_small)
        ple = (p[i].reshape(t, -1), ple_norm[i][None], mx(ple_w_gate[i]), mx(ple_w_proj[i]), final_norm[None])
        h = _ffn(h, ffn2_norm[i][None], mx(ffn2_w_gate[i]), mx(ffn2_w_up[i]), mx(ffn2_w_down[i]), ple,
                 tm=tm_small, final=(i == depth - 1))
    return h.reshape(b, s, d)
```

```python
import functools

import jax
import jax.numpy as jnp
from jax import lax
from jax.experimental import pallas as pl
from jax.experimental.pallas import tpu as pltpu

_MXU = jnp.bfloat16
F32 = jnp.float32

D_MODEL = 1024
N_HEADS = 8
HEAD_DIM = 64
KV_RANK = 256
IDX_HEADS = 8
IDX_DIM = 64
TOPK_MAX = 256
Q_BLOCK = 256
POOL_WINDOWS = (2, 4, 8, 16)
POOL_WIDTH = 512
POOL_GROUP = 128
SSD_HEADS = 16
SSD_HEAD_DIM = 64
D_INNER = 1024
N_GROUPS = 2
D_STATE = 128
CONV_WIDTH = 4
CONV_DIM = D_INNER + 2 * N_GROUPS * D_STATE
CHUNK = 128
D_FF = 2816
PLE_DIM = 256
EPS = 1e-6

C_XBC = 0
C_Q = 1536
C_Z = 2048
C_GATE = 3072
C_QIDX = 6144
C_POOL = 6656
C_CKV = 7168
PROJ_COLS = 7680
SM_KIDX = 0
SM_WIDX = 64
SM_DT = 72

INT_MIN = -2147483648
LOG2E = 1.4426950408889634
RC = 32
ONES_ROWS = 16
SSD_SUB = 8
CONV_TAIL = 16
NEG_BIG = -1e30
VMEM_LIMIT = 56 * 1024 * 1024


def _cparams(sem):
    return pltpu.CompilerParams(dimension_semantics=sem, vmem_limit_bytes=VMEM_LIMIT)


def _rms(x, g):
    return x * lax.rsqrt(jnp.mean(x * x, axis=-1, keepdims=True) + EPS) * g


def _dot(a, b):
    return jnp.dot(a, b, preferred_element_type=F32)


def _sigmoid(x):
    return 0.5 * jnp.tanh(0.5 * x) + 0.5


def _silu(x):
    h = 0.5 * x
    return h * jnp.tanh(h) + h


def _split3(a):
    hi = a.astype(jnp.bfloat16)
    r1 = a - hi.astype(F32)
    mid = r1.astype(jnp.bfloat16)
    lo = (r1 - mid.astype(F32)).astype(jnp.bfloat16)
    return hi, mid, lo


def _dot01_right(a, b01):
    b = b01.astype(jnp.bfloat16)
    hi, mid, lo = _split3(a)
    return _dot(hi, b) + _dot(mid, b) + _dot(lo, b)


def _select_rows(a01, b):
    if b.dtype == jnp.bfloat16:
        return _dot(a01.astype(jnp.bfloat16), b)
    return _dot01_left(a01, b)


def _dot01_left(a01, b):
    a = a01.astype(jnp.bfloat16)
    hi, mid, lo = _split3(b)
    return _dot(a, hi) + _dot(a, mid) + _dot(a, lo)


def _ffn_kernel(h_ref, g_ref, wg_ref, wu_ref, wd_ref, *rest, ple, final):
    o_ref = rest[-1]
    h = h_ref[...]
    xn = _rms(h, g_ref[...]).astype(_MXU)
    a = _silu(_dot(xn, wg_ref[...])) * _dot(xn, wu_ref[...])
    h = h + 0.5 * _dot(a.astype(_MXU), wd_ref[...])
    if ple:
        p_ref, gp_ref, wpg_ref, wpp_ref, fg_ref = rest[:-1]
        gate = _sigmoid(_dot(_rms(h, gp_ref[...]).astype(_MXU), wpg_ref[...]))
        h = h + gate * _dot(p_ref[...].astype(_MXU), wpp_ref[...])
        if final:
            h = _rms(h, fg_ref[...])
    o_ref[...] = h


def _ffn(h, g, wg, wu, wd, ple=None, *, tm, final=False):
    t, d = h.shape
    ff = wg.shape[1]
    once = dict(pipeline_mode=pl.Buffered(1))
    row = lambda i: (i, 0)
    const = lambda i: (0, 0)
    in_specs = [
        pl.BlockSpec((tm, d), row),
        pl.BlockSpec((1, d), const),
        pl.BlockSpec((d, ff), const, **once),
        pl.BlockSpec((d, ff), const, **once),
        pl.BlockSpec((ff, d), const, **once),
    ]
    args = [h, g, wg, wu, wd]
    if ple is not None:
        p, gp, wpg, wpp, fg = ple
        in_specs += [
            pl.BlockSpec((tm, p.shape[1]), row),
            pl.BlockSpec((1, d), const),
            pl.BlockSpec(wpg.shape, const, **once),
            pl.BlockSpec(wpp.shape, const, **once),
            pl.BlockSpec((1, d), const),
        ]
        args += [p, gp, wpg, wpp, fg]
    return pl.pallas_call(
        functools.partial(_ffn_kernel, ple=ple is not None, final=final),
        grid=(t // tm,),
        in_specs=in_specs,
        out_specs=pl.BlockSpec((tm, d), row),
        out_shape=jax.ShapeDtypeStruct((t, d), F32),
        compiler_params=_cparams(("parallel",)),
        name="ffn",
    )(*args)


def _inproj_kernel(h_ref, g_ref, w_ref, ws_ref, gkv_ref, gk_ref, o_ref, os_ref, ckv_o, ckvT_o, kidx_o, *, tn):
    xn = _rms(h_ref[...], g_ref[...]).astype(_MXU)
    small = _dot(xn, ws_ref[...])
    os_ref[...] = small
    for j in range(w_ref.shape[1] // tn):
        res = _dot(xn, w_ref[:, j * tn:(j + 1) * tn])
        o_ref[:, j * tn:(j + 1) * tn] = res.astype(o_ref.dtype)
        if j * tn <= C_CKV and C_CKV + KV_RANK <= (j + 1) * tn:
            c_raw = res[:, C_CKV - j * tn:C_CKV - j * tn + KV_RANK]
    c = _rms(c_raw, gkv_ref[...])
    ckv_o[0] = c.astype(ckv_o.dtype)
    ones = jnp.ones((ONES_ROWS, c.shape[0]), F32)
    ckvT_o[0, 0] = jnp.concatenate([c.T, ones], axis=0).astype(ckvT_o.dtype)
    kidx_o[0] = _rms(small[:, SM_KIDX:SM_KIDX + IDX_DIM], gk_ref[...]).astype(kidx_o.dtype)


def _inproj(h, g, w, ws, gkv, gk, *, b, s, tm, tn):
    t, d = h.shape
    n = w.shape[1]
    nk = s // tm
    once = dict(pipeline_mode=pl.Buffered(1))
    return pl.pallas_call(
        functools.partial(_inproj_kernel, tn=tn),
        grid=(t // tm,),
        in_specs=[
            pl.BlockSpec((tm, d), lambda i: (i, 0)),
            pl.BlockSpec((1, d), lambda i: (0, 0)),
            pl.BlockSpec((d, n), lambda i: (0, 0), **once),
            pl.BlockSpec((d, 128), lambda i: (0, 0), **once),
            pl.BlockSpec((1, KV_RANK), lambda i: (0, 0)),
            pl.BlockSpec((1, IDX_DIM), lambda i: (0, 0)),
        ],
        out_specs=[
            pl.BlockSpec((tm, n), lambda i: (i, 0)),
            pl.BlockSpec((tm, 128), lambda i: (i, 0)),
            pl.BlockSpec((1, tm, KV_RANK), lambda i: (i // nk, i % nk, 0)),
            pl.BlockSpec((1, 1, KV_RANK + ONES_ROWS, tm), lambda i: (i // nk, i % nk, 0, 0)),
            pl.BlockSpec((1, tm, IDX_DIM), lambda i: (i // nk, i % nk, 0)),
        ],
        out_shape=[
            jax.ShapeDtypeStruct((t, n), _MXU),
            jax.ShapeDtypeStruct((t, 128), F32),
            jax.ShapeDtypeStruct((b, s, KV_RANK), _MXU),
            jax.ShapeDtypeStruct((b, nk, KV_RANK + ONES_ROWS, tm), _MXU),
            jax.ShapeDtypeStruct((b, s, IDX_DIM), _MXU),
        ],
        compiler_params=_cparams(("parallel",)),
        name="inproj",
    )(h, g, w, ws, gkv, gk)


def _bit_transpose32(words):
    words = list(words)
    j, m = 16, 0x0000FFFF
    while j:
        mi = jnp.int32(m - (1 << 32) if m >= (1 << 31) else m)
        for k in range(32):
            if k & j == 0:
                t = (words[k] ^ lax.shift_right_logical(words[k + j], jnp.int32(j))) & mi
                words[k] = words[k] ^ t
                words[k + j] = words[k + j] ^ lax.shift_left(t, jnp.int32(j))
        j >>= 1
        m = (m ^ (m << j)) & 0xFFFFFFFF
    return words


def _attn_kernel(q_ref, qi_ref, sm_ref, ckv_ref, ckvT_ref, kidx_ref, wuk_ref, wuvT_ref, o_ref,
                 key_ref, planes_ref, qlat_ref, s_ref, p_ref, acc_ref, m_ref, alpha_ref,
                 *, k_top, kb):
    j = pl.program_id(1)
    nblk = (j * Q_BLOCK + Q_BLOCK + kb - 1) // kb
    nh = N_HEADS
    qb = Q_BLOCK

    qT = q_ref[...].astype(F32).T
    qiT = qi_ref[...].astype(F32).T
    smT = sm_ref[...].T
    w_row = jnp.concatenate([smT[SM_WIDX + h:SM_WIDX + h + 1, :] for h in range(nh)], axis=1)
    qiT_stack = jnp.concatenate(
        [qiT[h * IDX_DIM:(h + 1) * IDX_DIM, :] for h in range(nh)], axis=1).astype(_MXU)
    scale = HEAD_DIM ** -0.5
    qlatT = jnp.concatenate(
        [_dot(wuk_ref[h], qT[h * HEAD_DIM:(h + 1) * HEAD_DIM, :].astype(_MXU)) for h in range(nh)],
        axis=1)
    qlat_ref[...] = (qlatT * (scale * LOG2E)).astype(_MXU)
    s_ref[0] = _dot(ckv_ref[0, 0:kb, :], qlat_ref[...])

    qpos = j * qb + lax.broadcasted_iota(jnp.int32, (kb, qb), 1)

    def idx_block(bi):
        off = pl.multiple_of(bi * kb, kb)
        kblk = kidx_ref[0, pl.ds(off, kb), :]
        r = jnp.maximum(_dot(kblk, qiT_stack), 0.0) * w_row
        sc = r[:, 0:qb]
        for h in range(1, nh):
            sc = sc + r[:, h * qb:(h + 1) * qb]
        bits = pltpu.bitcast(sc, jnp.int32)
        key = jnp.where(bits < 0, bits ^ jnp.int32(0x7FFFFFFF), bits)
        kpos = off + lax.broadcasted_iota(jnp.int32, (kb, qb), 0)
        key = jnp.where(kpos <= qpos, key, jnp.int32(INT_MIN))
        key_ref[pl.ds(off, kb), :] = key
        u = key ^ jnp.int32(INT_MIN)
        for g in range(kb // 256):
            words = _bit_transpose32([u[g * 256 + 8 * t:g * 256 + 8 * t + 8, :] for t in range(32)])
            row = pl.multiple_of((bi * (kb // 256) + g) * 8, 8)
            for i in range(32):
                planes_ref[i, pl.ds(row, 8), :] = words[i]

    def idx_pair(i2, carry):
        idx_block(2 * i2)
        idx_block(2 * i2 + 1)
        return carry

    lax.fori_loop(0, nblk // 2, idx_pair, 0)

    @pl.when(nblk % 2 == 1)
    def _():
        idx_block(nblk - 1)

    ngrp_all = planes_ref.shape[1] // 8
    ngrp_live = nblk * (kb // 256)

    def radix_select(ng):
        rows = ng * 8
        grp = lax.broadcasted_iota(jnp.int32, (rows, qb), 0) // 8
        eq0 = jnp.where(grp < ngrp_live, jnp.int32(-1), jnp.int32(0))

        def count(words):
            c = jnp.sum(lax.population_count(words).reshape(ng, 8, qb), axis=0)
            return jnp.sum(c, axis=0, keepdims=True)

        def bit_body(i, carry):
            eq, n_above, prefix = carry
            x1 = planes_ref[2 * i, 0:rows, :]
            x0 = planes_ref[2 * i + 1, 0:rows, :]
            e1 = eq & x1
            n1, n11, nx0 = count(e1), count(e1 & x0), count(eq & x0)
            take1 = (n_above + n1) >= k_top
            n_above = jnp.where(take1, n_above, n_above + n1)
            n_lo = jnp.where(take1, n11, nx0 - n11)
            take0 = (n_above + n_lo) >= k_top
            n_above = jnp.where(take0, n_above, n_above + n_lo)
            flip1 = jnp.where(take1, jnp.int32(0), jnp.int32(-1))
            flip0 = jnp.where(take0, jnp.int32(0), jnp.int32(-1))
            eq = eq & (x1 ^ flip1) & (x0 ^ flip0)
            two = jnp.where(take1, 2, 0) | jnp.where(take0, 1, 0)
            return eq, n_above, prefix | jnp.left_shift(two, 30 - 2 * i)

        zrow = jnp.zeros((1, qb), jnp.int32)
        return lax.fori_loop(0, 16, bit_body, (eq0, zrow, zrow))[2]

    quarter = max(ngrp_all // 4, 1)
    sizes = list(range(quarter, ngrp_all + 1, quarter))
    which = jnp.minimum((ngrp_live - 1) // quarter, len(sizes) - 1)
    prefix = lax.switch(which, [functools.partial(radix_select, ng) for ng in sizes])
    thr = jnp.maximum(prefix ^ jnp.int32(INT_MIN), jnp.int32(INT_MIN + 1))
    thr_b = jnp.broadcast_to(thr, (RC, qb))

    m_ref[...] = jnp.full(m_ref.shape, NEG_BIG, F32)
    acc_ref[...] = jnp.zeros_like(acc_ref)

    def qk(bi, slot):
        off = pl.multiple_of(jnp.minimum(bi, nblk - 1) * kb, kb)
        s_ref[slot] = _dot(ckv_ref[0, pl.ds(off, kb), :], qlat_ref[...])

    def pv(bi, slot):
        acc_ref[...] = alpha_ref[...] * acc_ref[...] + _dot(ckvT_ref[0, jnp.maximum(bi, 0)], p_ref[slot])

    def softmax(bi, slot):
        off = pl.multiple_of(bi * kb, kb)
        nch = kb // RC

        def masks(c):
            return key_ref[pl.ds(off + c * RC, RC), :] >= thr_b

        mx = [jnp.full((8, qb), NEG_BIG, F32)] * nh
        for c in range(nch):
            mk = masks(c)
            for h in range(nh):
                sc = jnp.where(mk, s_ref[slot, c * RC:(c + 1) * RC, h * qb:(h + 1) * qb], NEG_BIG)
                s_ref[slot, c * RC:(c + 1) * RC, h * qb:(h + 1) * qb] = sc
                mx[h] = jnp.maximum(mx[h], jnp.max(sc.reshape(RC // 8, 8, qb), axis=0))
        m_new = []
        for h in range(nh):
            hs = slice(h * qb, (h + 1) * qb)
            m_old = m_ref[:, hs]
            mn = jnp.maximum(m_old, jnp.max(mx[h], axis=0, keepdims=True))
            alpha_ref[:, hs] = jnp.exp2(m_old - mn)
            m_ref[:, hs] = mn
            m_new.append(jnp.broadcast_to(mn, (RC, qb)))
        for c in range(nch):
            for h in range(nh):
                sc = s_ref[slot, c * RC:(c + 1) * RC, h * qb:(h + 1) * qb]
                pr = jnp.exp2(sc - m_new[h])
                p_ref[slot, c * RC:(c + 1) * RC, h * qb:(h + 1) * qb] = pr.astype(p_ref.dtype)

    def step(bi, cur):
        qk(bi + 1, 1 - cur)
        pv(bi - 1, 1 - cur)
        softmax(bi, cur)

    qk(1, 1)
    softmax(0, 0)

    def pair_body(i2, carry):
        step(2 * i2 + 1, 1)
        step(2 * i2 + 2, 0)
        return carry

    lax.fori_loop(0, (nblk - 1) // 2, pair_body, 0)
    even = nblk % 2 == 0

    @pl.when(even)
    def _():
        pv(nblk - 2, 0)
        softmax(nblk - 1, 1)
        pv(nblk - 1, 1)

    @pl.when(jnp.logical_not(even))
    def _():
        pv(nblk - 1, 0)

    inv_l = 1.0 / acc_ref[KV_RANK:KV_RANK + 1, :]
    o = (acc_ref[:KV_RANK, :] * inv_l).astype(_MXU)
    yT = jnp.concatenate([_dot(wuvT_ref[h], o[:, h * qb:(h + 1) * qb]) for h in range(nh)], axis=0)
    o_ref[...] = yT.T


def _attention(proj, small, ckv, ckvT, kidx, wuk, wuvT, *, b, s, kb):
    nq = s // Q_BLOCK
    k_top = min(TOPK_MAX, s // 4)
    kern = functools.partial(_attn_kernel, k_top=k_top, kb=kb)
    hq = N_HEADS * Q_BLOCK
    return pl.pallas_call(
        kern,
        grid=(b, nq),
        in_specs=[
            pl.BlockSpec((Q_BLOCK, 512), lambda bi, j: (bi * nq + j, C_Q // 512)),
            pl.BlockSpec((Q_BLOCK, 512), lambda bi, j: (bi * nq + j, C_QIDX // 512)),
            pl.BlockSpec((Q_BLOCK, 128), lambda bi, j: (bi * nq + j, 0)),
            pl.BlockSpec((1, s, KV_RANK), lambda bi, j: (bi, 0, 0)),
            pl.BlockSpec((1, s // kb, KV_RANK + ONES_ROWS, kb), lambda bi, j: (bi, 0, 0, 0)),
            pl.BlockSpec((1, s, IDX_DIM), lambda bi, j: (bi, 0, 0)),
            pl.BlockSpec((N_HEADS, KV_RANK, HEAD_DIM), lambda bi, j: (0, 0, 0)),
            pl.BlockSpec((N_HEADS, HEAD_DIM, KV_RANK), lambda bi, j: (0, 0, 0)),
        ],
        out_specs=pl.BlockSpec((Q_BLOCK, N_HEADS * HEAD_DIM), lambda bi, j: (bi * nq + j, 0)),
        out_shape=jax.ShapeDtypeStruct((b * s, N_HEADS * HEAD_DIM), F32),
        scratch_shapes=[
            pltpu.VMEM((s, Q_BLOCK), jnp.int32),
            pltpu.VMEM((32, s // 32, Q_BLOCK), jnp.int32),
            pltpu.VMEM((KV_RANK, hq), _MXU),
            pltpu.VMEM((2, kb, hq), F32),
            pltpu.VMEM((2, kb, hq), _MXU),
            pltpu.VMEM((KV_RANK + ONES_ROWS, hq), F32),
            pltpu.VMEM((1, hq), F32),
            pltpu.VMEM((1, hq), F32),
        ],
        compiler_params=_cparams(("parallel", "arbitrary")),
        name="dsa_attn",
    )(proj, proj, small, ckv, ckvT, kidx, wuk, wuvT)


def _pool_rows(x, halo, pos, pw_ref, ps_ref):
    tm = x.shape[0]
    xe = jnp.concatenate([halo, x], axis=0)
    outs = []
    for g, w in enumerate(POOL_WINDOWS):
        sl = slice(g * POOL_GROUP, (g + 1) * POOL_GROUP)
        cur = xe[:, sl]
        span = 1
        while span < w:
            n = cur.shape[0]
            cur = cur[span:, :] + cur[:n - span, :]
            span *= 2
        win = cur[cur.shape[0] - tm:, :]
        count = jnp.minimum(pos + 1, w).astype(F32)
        pooled = win / count - x[:, sl]
        outs.append(_dot(pooled.astype(_MXU), pw_ref[g]))
    return jnp.concatenate(outs, axis=1) * ps_ref[...]


def _ssd_kernel(xbc_ref, halo_ref, z_ref, sm_ref, cw_ref, cb_ref, dtb_row_ref, dtb_col_ref, a_row_ref, a_col_ref,
                dsk_x_ref, expand_ref, ng_ref, o_ref, state_ref):
    c = pl.program_id(1)
    L = CHUNK
    P = SSD_HEAD_DIM

    @pl.when(c == 0)
    def _():
        state_ref[...] = jnp.zeros_like(state_ref)

    halo0 = halo_ref[...]
    halo0 = jnp.where(c > 0, halo0, jnp.zeros_like(halo0))
    for sub in range(SSD_SUB):
        rows = slice(sub * L, (sub + 1) * L)
        x_nat = xbc_ref[rows, :]
        halo = halo0 if sub == 0 else xbc_ref[sub * L - CONV_TAIL:sub * L, :]
        o_ref[rows, :] = _ssd_chunk(x_nat, halo, z_ref[rows, :], sm_ref[rows, :], cw_ref, cb_ref, dtb_row_ref,
                                    dtb_col_ref, a_row_ref, a_col_ref, dsk_x_ref, expand_ref, ng_ref, state_ref)


def _ssd_chunk(x_nat, halo, z, sm, cw_ref, cb_ref, dtb_row_ref, dtb_col_ref, a_row_ref, a_col_ref,
               dsk_x_ref, expand_ref, ng_ref, state_ref):
    L = CHUNK
    P = SSD_HEAD_DIM
    xe = jnp.concatenate([halo, x_nat], axis=0)
    cw = cw_ref[...]
    conv = cb_ref[...] + cw[3:4, :] * x_nat.astype(F32)
    srow = lax.broadcasted_iota(jnp.int32, (L, L + CONV_TAIL), 0)
    scol = lax.broadcasted_iota(jnp.int32, (L, L + CONV_TAIL), 1)
    for k in range(CONV_WIDTH - 1):
        sh = CONV_WIDTH - 1 - k
        sel = jnp.where(scol == srow + (CONV_TAIL - sh), 1.0, 0.0)
        conv = conv + cw[k:k + 1, :] * _select_rows(sel, xe)
    xc = _silu(conv)
    xs = xc[:, :D_INNER]
    bm = xc[:, D_INNER:D_INNER + N_GROUPS * D_STATE]
    cm = xc[:, D_INNER + N_GROUPS * D_STATE:]

    smT = sm.T
    dt = jax.nn.softplus(sm[:, SM_DT:SM_DT + SSD_HEADS] + dtb_row_ref[...])
    dtT = jax.nn.softplus(smT[SM_DT:SM_DT + SSD_HEADS, :] + dtb_col_ref[...])
    expand = expand_ref[...]
    dtx = _dot01_right(dt, expand)
    da = dt * a_row_ref[...]
    daT = dtT * a_col_ref[...]
    row = lax.broadcasted_iota(jnp.int32, (L, L), 0)
    col = lax.broadcasted_iota(jnp.int32, (L, L), 1)
    lower = row >= col
    tril = jnp.where(lower, 1.0, 0.0).astype(F32)
    triu = jnp.where(row <= col, 1.0, 0.0).astype(F32)
    acs = _dot01_left(tril, da)
    acsT = _dot01_right(daT, triu)
    acs_x = _dot01_right(acs, expand)
    a_last = acs_x[L - 1:L, :]
    ea_x = jnp.exp(acs_x)
    decay_x = jnp.exp(a_last - acs_x)
    cd_x = jnp.exp(a_last)

    X = xs * dtx
    Xb = X.astype(_MXU)
    Xd = (X * decay_x).astype(_MXU)
    state = state_ref[...]
    ys = []
    hpg = SSD_HEADS // N_GROUPS
    for g in range(N_GROUPS):
        bg = bm[:, g * D_STATE:(g + 1) * D_STATE]
        cg = cm[:, g * D_STATE:(g + 1) * D_STATE].astype(_MXU)
        bgT = bg.T.astype(_MXU)
        cb = _dot(cg, bgT)
        gs = slice(g * hpg * P, (g + 1) * hpg * P)
        y_off = _dot(cg, state[:, gs].astype(_MXU)) * ea_x[:, gs]
        state_ref[:, gs] = state[:, gs] * cd_x[:, gs] + _dot(bgT, Xd[:, gs])
        for e in range(hpg):
            h = g * hpg + e
            hs = slice(h * P, (h + 1) * P)
            diff = acs[:, h:h + 1] - acsT[h:h + 1, :]
            lmat = jnp.exp(jnp.where(lower, diff, -jnp.inf))
            m = (cb * lmat).astype(_MXU)
            ys.append(_dot(m, Xb[:, hs]) + y_off[:, e * P:(e + 1) * P])
    y = jnp.concatenate(ys, axis=1) + xs * dsk_x_ref[...]
    y = y * _silu(z.astype(F32))
    return _rms(y, ng_ref[...])


def _ssd(proj, small, cw, cb, dtb, a_log, dsk, ng, *, b, s):
    blk = CHUNK * SSD_SUB
    nc = s // blk
    rep = lambda v: jnp.repeat(v, SSD_HEAD_DIM)[None, :]
    a = -jnp.exp(a_log)
    expand = jnp.repeat(jnp.eye(SSD_HEADS, dtype=F32), SSD_HEAD_DIM, axis=1)
    const2 = lambda bi, c: (0, 0)
    return pl.pallas_call(
        _ssd_kernel,
        grid=(b, nc),
        in_specs=[
            pl.BlockSpec((blk, CONV_DIM), lambda bi, c: (bi * nc + c, C_XBC // CONV_DIM)),
            pl.BlockSpec((CONV_TAIL, CONV_DIM),
                         lambda bi, c: (jnp.maximum((bi * nc + c) * (blk // CONV_TAIL) - 1, 0), C_XBC // CONV_DIM)),
            pl.BlockSpec((blk, D_INNER), lambda bi, c: (bi * nc + c, C_Z // D_INNER)),
            pl.BlockSpec((blk, 128), lambda bi, c: (bi * nc + c, 0)),
            pl.BlockSpec((CONV_WIDTH, CONV_DIM), const2),
            pl.BlockSpec((1, CONV_DIM), const2),
            pl.BlockSpec((1, SSD_HEADS), const2),
            pl.BlockSpec((SSD_HEADS, 1), const2),
            pl.BlockSpec((1, SSD_HEADS), const2),
            pl.BlockSpec((SSD_HEADS, 1), const2),
            pl.BlockSpec((1, D_INNER), const2),
            pl.BlockSpec((SSD_HEADS, D_INNER), const2),
            pl.BlockSpec((1, D_INNER), const2),
        ],
        out_specs=pl.BlockSpec((blk, D_INNER), lambda bi, c: (bi * nc + c, 0)),
        out_shape=jax.ShapeDtypeStruct((b * s, D_INNER), F32),
        scratch_shapes=[pltpu.VMEM((D_STATE, D_INNER), F32)],
        compiler_params=_cparams(("parallel", "arbitrary")),
        name="ssd",
    )(proj, proj, proj, small, cw, cb[None, :], dtb[None, :], dtb[:, None], a[None, :], a[:, None],
      rep(dsk), expand, ng[None, :])


def _merge_kernel(ya_ref, xp_ref, halo_ref, ys_ref, ga_ref, gp_ref, gs_ref, h_ref, pw_ref, ps_ref,
                  wa_ref, wp_ref, ws_ref, wo_ref, o_ref, *, seq):
    tm = h_ref.shape[0]
    start = (pl.program_id(0) * tm) % seq
    halo = jnp.where(start > 0, halo_ref[...].astype(F32), 0.0)
    pos = start + lax.broadcasted_iota(jnp.int32, (tm, POOL_GROUP), 0)
    yp = _pool_rows(xp_ref[...].astype(F32), halo, pos, pw_ref, ps_ref)
    m = _sigmoid(ga_ref[...].astype(F32)) * _dot(ya_ref[...].astype(_MXU), wa_ref[...])
    m = m + _sigmoid(gp_ref[...].astype(F32)) * _dot(yp.astype(_MXU), wp_ref[...])
    m = m + _sigmoid(gs_ref[...].astype(F32)) * _dot(ys_ref[...].astype(_MXU), ws_ref[...])
    o_ref[...] = h_ref[...] + _dot(m.astype(_MXU), wo_ref[...])


def _merge(ya, ys, proj, h, pw, ps, wa, wp, ws, wo, *, seq, tm):
    t, d = h.shape
    row = lambda i: (i, 0)
    const = lambda i: (0, 0)
    gate = lambda k: (lambda i: (i, C_GATE // d + k))
    return pl.pallas_call(
        functools.partial(_merge_kernel, seq=seq),
        grid=(t // tm,),
        in_specs=[
            pl.BlockSpec((tm, ya.shape[1]), row),
            pl.BlockSpec((tm, POOL_WIDTH), lambda i: (i, C_POOL // POOL_WIDTH)),
            pl.BlockSpec((16, POOL_WIDTH), lambda i: (jnp.maximum(i * (tm // 16) - 1, 0), C_POOL // POOL_WIDTH)),
            pl.BlockSpec((tm, ys.shape[1]), row),
            pl.BlockSpec((tm, d), gate(0)),
            pl.BlockSpec((tm, d), gate(1)),
            pl.BlockSpec((tm, d), gate(2)),
            pl.BlockSpec((tm, d), row),
            pl.BlockSpec(pw.shape, lambda i: (0, 0, 0)),
            pl.BlockSpec(ps.shape, const),
            pl.BlockSpec(wa.shape, const),
            pl.BlockSpec(wp.shape, const),
            pl.BlockSpec(ws.shape, const),
            pl.BlockSpec(wo.shape, const),
        ],
        out_specs=pl.BlockSpec((tm, d), row),
        out_shape=jax.ShapeDtypeStruct((t, d), F32),
        compiler_params=_cparams(("parallel",)),
        name="merge",
    )(ya, proj, proj, ys, proj, proj, proj, h, pw, ps, wa, wp, ws, wo)


def _reorder_w_in(w):
    o_q, o_ckv, o_qidx, o_widx, o_kidx, o_pool, o_z, o_xbc, o_dt, o_gate = (
        0, 512, 768, 1280, 1288, 1352, 1864, 2888, 4424, 4440)
    d = w.shape[0]
    small = jnp.concatenate([
        w[:, o_kidx:o_kidx + IDX_DIM], w[:, o_widx:o_widx + IDX_HEADS], w[:, o_dt:o_dt + SSD_HEADS],
        jnp.zeros((d, 128 - IDX_DIM - IDX_HEADS - SSD_HEADS), w.dtype)], axis=1)
    wb = w.astype(_MXU)
    used = CONV_DIM + 512 + D_INNER + 3 * D_MODEL + 512 + POOL_WIDTH + KV_RANK
    wide = jnp.concatenate([
        wb[:, o_xbc:o_xbc + CONV_DIM], wb[:, o_q:o_q + 512], wb[:, o_z:o_z + D_INNER],
        wb[:, o_gate:o_gate + 3 * D_MODEL], wb[:, o_qidx:o_qidx + 512], wb[:, o_pool:o_pool + POOL_WIDTH],
        wb[:, o_ckv:o_ckv + KV_RANK], jnp.zeros((d, PROJ_COLS - used), _MXU)], axis=1)
    return wide, small.astype(_MXU)


def kernel(x, p, ffn1_norm, ffn1_w_gate, ffn1_w_up, ffn1_w_down, mix_norm, w_in, kv_norm, idx_k_norm, w_uk, w_uv, pool_w, pool_scale, conv_w, conv_b, dt_bias, a_log, d_skip, ssd_norm, w_br_attn, w_br_pool, w_br_ssd, w_out, ffn2_norm, ffn2_w_gate, ffn2_w_up, ffn2_w_down, ple_norm, ple_w_gate, ple_w_proj, final_norm):
    b, s, d = x.shape
    depth = w_in.shape[0]
    t = b * s
    tm_small = min(512, t)
    kb = min(512, s)
    mx = lambda a: a.astype(_MXU)

    h = x.reshape(t, d)
    for i in range(depth):
        h = _ffn(h, ffn1_norm[i][None], mx(ffn1_w_gate[i]), mx(ffn1_w_up[i]), mx(ffn1_w_down[i]),
                 tm=tm_small)
        w_wide, w_small = _reorder_w_in(w_in[i])
        proj, small, ckv, ckvT, kidx = _inproj(h, mix_norm[i][None], w_wide, w_small, kv_norm[i][None],
                                               idx_k_norm[i][None], b=b, s=s, tm=kb, tn=2560)
        wuk = mx(jnp.transpose(w_uk[i], (1, 0, 2)))
        wuvT = mx(jnp.transpose(w_uv[i], (1, 2, 0)))
        y_attn = _attention(proj, small, ckv, ckvT, kidx, wuk, wuvT, b=b, s=s, kb=kb)
        y_ssd = _ssd(proj, small, conv_w[i], conv_b[i], dt_bias[i], a_log[i], d_skip[i], ssd_norm[i], b=b, s=s)
        h = _merge(y_attn, y_ssd, proj, h, mx(pool_w[i]), pool_scale[i][None], mx(w_br_attn[i]),
                   mx(w_br_pool[i]), mx(w_br_ssd[i]), mx(w_out[i]), seq=s, tm=tm_small)
        ple = (p[i].reshape(t, -1), ple_norm[i][None], mx(ple_w_gate[i]), mx(ple_w_proj[i]), final_norm[None])
        h = _ffn(h, ffn2_norm[i][None], mx(ffn2_w_gate[i]), mx(ffn2_w_up[i]), mx(ffn2_w_down[i]), ple,
                 tm=tm_small, final=(i == depth - 1))
    return h.reshape(b, s, d)
```

```python
import functools

import jax
import jax.numpy as jnp
from jax import lax
from jax.experimental import pallas as pl
from jax.experimental.pallas import tpu as pltpu

_MXU = jnp.bfloat16
F32 = jnp.float32

D_MODEL = 1024
N_HEADS = 8
HEAD_DIM = 64
KV_RANK = 256
IDX_HEADS = 8
IDX_DIM = 64
TOPK_MAX = 256
Q_BLOCK = 256
POOL_WINDOWS = (2, 4, 8, 16)
POOL_WIDTH = 512
POOL_GROUP = 128
SSD_HEADS = 16
SSD_HEAD_DIM = 64
D_INNER = 1024
N_GROUPS = 2
D_STATE = 128
CONV_WIDTH = 4
CONV_DIM = D_INNER + 2 * N_GROUPS * D_STATE
CHUNK = 128
D_FF = 2816
PLE_DIM = 256
EPS = 1e-6

C_XBC = 0
C_Q = 1536
C_Z = 2048
C_GATE = 3072
C_QIDX = 6144
C_POOL = 6656
C_CKV = 7168
PROJ_COLS = 7680
SM_KIDX = 0
SM_WIDX = 64
SM_DT = 72

INT_MIN = -2147483648
LOG2E = 1.4426950408889634
RC = 32
ONES_ROWS = 16
SSD_SUB = 8
CONV_TAIL = 16
NEG_BIG = -1e30
VMEM_LIMIT = 56 * 1024 * 1024


def _cparams(sem):
    return pltpu.CompilerParams(dimension_semantics=sem, vmem_limit_bytes=VMEM_LIMIT)


def _rms(x, g):
    return x * lax.rsqrt(jnp.mean(x * x, axis=-1, keepdims=True) + EPS) * g


def _dot(a, b):
    return jnp.dot(a, b, preferred_element_type=F32)


def _sigmoid(x):
    return 0.5 * jnp.tanh(0.5 * x) + 0.5


def _silu(x):
    h = 0.5 * x
    return h * jnp.tanh(h) + h


def _split3(a):
    hi = a.astype(jnp.bfloat16)
    r1 = a - hi.astype(F32)
    mid = r1.astype(jnp.bfloat16)
    lo = (r1 - mid.astype(F32)).astype(jnp.bfloat16)
    return hi, mid, lo


def _dot01_right(a, b01):
    b = b01.astype(jnp.bfloat16)
    hi, mid, lo = _split3(a)
    return _dot(hi, b) + _dot(mid, b) + _dot(lo, b)


def _select_rows(a01, b):
    if b.dtype == jnp.bfloat16:
        return _dot(a01.astype(jnp.bfloat16), b)
    return _dot01_left(a01, b)


def _dot01_left(a01, b):
    a = a01.astype(jnp.bfloat16)
    hi, mid, lo = _split3(b)
    return _dot(a, hi) + _dot(a, mid) + _dot(a, lo)


def _ffn_kernel(h_ref, g_ref, wg_ref, wu_ref, wd_ref, *rest, ple, final):
    o_ref = rest[-1]
    h = h_ref[...]
    xn = _rms(h, g_ref[...]).astype(_MXU)
    a = _silu(_dot(xn, wg_ref[...])) * _dot(xn, wu_ref[...])
    h = h + 0.5 * _dot(a.astype(_MXU), wd_ref[...])
    if ple:
        p_ref, gp_ref, wpg_ref, wpp_ref, fg_ref = rest[:-1]
        gate = _sigmoid(_dot(_rms(h, gp_ref[...]).astype(_MXU), wpg_ref[...]))
        h = h + gate * _dot(p_ref[...].astype(_MXU), wpp_ref[...])
        if final:
            h = _rms(h, fg_ref[...])
    o_ref[...] = h


def _ffn(h, g, wg, wu, wd, ple=None, *, tm, final=False):
    t, d = h.shape
    ff = wg.shape[1]
    once = dict(pipeline_mode=pl.Buffered(1))
    row = lambda i: (i, 0)
    const = lambda i: (0, 0)
    in_specs = [
        pl.BlockSpec((tm, d), row),
        pl.BlockSpec((1, d), const),
        pl.BlockSpec((d, ff), const, **once),
        pl.BlockSpec((d, ff), const, **once),
        pl.BlockSpec((ff, d), const, **once),
    ]
    args = [h, g, wg, wu, wd]
    if ple is not None:
        p, gp, wpg, wpp, fg = ple
        in_specs += [
            pl.BlockSpec((tm, p.shape[1]), row),
            pl.BlockSpec((1, d), const),
            pl.BlockSpec(wpg.shape, const, **once),
            pl.BlockSpec(wpp.shape, const, **once),
            pl.BlockSpec((1, d), const),
        ]
        args += [p, gp, wpg, wpp, fg]
    return pl.pallas_call(
        functools.partial(_ffn_kernel, ple=ple is not None, final=final),
        grid=(t // tm,),
        in_specs=in_specs,
        out_specs=pl.BlockSpec((tm, d), row),
        out_shape=jax.ShapeDtypeStruct((t, d), F32),
        compiler_params=_cparams(("parallel",)),
        name="ffn",
    )(*args)


def _inproj_kernel(h_ref, g_ref, w_ref, ws_ref, gkv_ref, gk_ref, o_ref, os_ref, ckv_o, ckvT_o, kidx_o, *, tn):
    xn = _rms(h_ref[...], g_ref[...]).astype(_MXU)
    small = _dot(xn, ws_ref[...])
    os_ref[...] = small
    for j in range(w_ref.shape[1] // tn):
        res = _dot(xn, w_ref[:, j * tn:(j + 1) * tn])
        o_ref[:, j * tn:(j + 1) * tn] = res.astype(o_ref.dtype)
        if j * tn <= C_CKV and C_CKV + KV_RANK <= (j + 1) * tn:
            c_raw = res[:, C_CKV - j * tn:C_CKV - j * tn + KV_RANK]
    c = _rms(c_raw, gkv_ref[...])
    ckv_o[0] = c.astype(ckv_o.dtype)
    ones = jnp.ones((ONES_ROWS, c.shape[0]), F32)
    ckvT_o[0, 0] = jnp.concatenate([c.T, ones], axis=0).astype(ckvT_o.dtype)
    kidx_o[0] = _rms(small[:, SM_KIDX:SM_KIDX + IDX_DIM], gk_ref[...]).astype(kidx_o.dtype)


def _inproj(h, g, w, ws, gkv, gk, *, b, s, tm, tn):
    t, d = h.shape
    n = w.shape[1]
    nk = s // tm
    once = dict(pipeline_mode=pl.Buffered(1))
    return pl.pallas_call(
        functools.partial(_inproj_kernel, tn=tn),
        grid=(t // tm,),
        in_specs=[
            pl.BlockSpec((tm, d), lambda i: (i, 0)),
            pl.BlockSpec((1, d), lambda i: (0, 0)),
            pl.BlockSpec((d, n), lambda i: (0, 0), **once),
            pl.BlockSpec((d, 128), lambda i: (0, 0), **once),
            pl.BlockSpec((1, KV_RANK), lambda i: (0, 0)),
            pl.BlockSpec((1, IDX_DIM), lambda i: (0, 0)),
        ],
        out_specs=[
            pl.BlockSpec((tm, n), lambda i: (i, 0)),
            pl.BlockSpec((tm, 128), lambda i: (i, 0)),
            pl.BlockSpec((1, tm, KV_RANK), lambda i: (i // nk, i % nk, 0)),
            pl.BlockSpec((1, 1, KV_RANK + ONES_ROWS, tm), lambda i: (i // nk, i % nk, 0, 0)),
            pl.BlockSpec((1, tm, IDX_DIM), lambda i: (i // nk, i % nk, 0)),
        ],
        out_shape=[
            jax.ShapeDtypeStruct((t, n), _MXU),
            jax.ShapeDtypeStruct((t, 128), F32),
            jax.ShapeDtypeStruct((b, s, KV_RANK), _MXU),
            jax.ShapeDtypeStruct((b, nk, KV_RANK + ONES_ROWS, tm), _MXU),
            jax.ShapeDtypeStruct((b, s, IDX_DIM), _MXU),
        ],
        compiler_params=_cparams(("parallel",)),
        name="inproj",
    )(h, g, w, ws, gkv, gk)


def _bit_transpose32(words):
    words = list(words)
    j, m = 16, 0x0000FFFF
    while j:
        mi = jnp.int32(m - (1 << 32) if m >= (1 << 31) else m)
        for k in range(32):
            if k & j == 0:
                t = (words[k] ^ lax.shift_right_logical(words[k + j], jnp.int32(j))) & mi
                words[k] = words[k] ^ t
                words[k + j] = words[k + j] ^ lax.shift_left(t, jnp.int32(j))
        j >>= 1
        m = (m ^ (m << j)) & 0xFFFFFFFF
    return words


def _attn_kernel(q_ref, qi_ref, sm_ref, ckv_ref, ckvT_ref, kidx_ref, wuk_ref, wuvT_ref, o_ref,
                 key_ref, planes_ref, qlat_ref, s_ref, p_ref, acc_ref, m_ref, alpha_ref,
                 *, k_top, kb):
    j = pl.program_id(1)
    nblk = (j * Q_BLOCK + Q_BLOCK + kb - 1) // kb
    nh = N_HEADS
    qb = Q_BLOCK

    qT = q_ref[...].astype(F32).T
    qiT = qi_ref[...].astype(F32).T
    smT = sm_ref[...].T
    w_row = jnp.concatenate([smT[SM_WIDX + h:SM_WIDX + h + 1, :] for h in range(nh)], axis=1)
    qiT_stack = jnp.concatenate(
        [qiT[h * IDX_DIM:(h + 1) * IDX_DIM, :] for h in range(nh)], axis=1).astype(_MXU)
    scale = HEAD_DIM ** -0.5
    qlatT = jnp.concatenate(
        [_dot(wuk_ref[h], qT[h * HEAD_DIM:(h + 1) * HEAD_DIM, :].astype(_MXU)) for h in range(nh)],
        axis=1)
    qlat_ref[...] = (qlatT * (scale * LOG2E)).astype(_MXU)
    s_ref[0] = _dot(ckv_ref[0, 0:kb, :], qlat_ref[...])

    qpos = j * qb + lax.broadcasted_iota(jnp.int32, (kb, qb), 1)

    def idx_block(bi):
        off = pl.multiple_of(bi * kb, kb)
        kblk = kidx_ref[0, pl.ds(off, kb), :]
        r = jnp.maximum(_dot(kblk, qiT_stack), 0.0) * w_row
        sc = r[:, 0:qb]
        for h in range(1, nh):
            sc = sc + r[:, h * qb:(h + 1) * qb]
        bits = pltpu.bitcast(sc, jnp.int32)
        key = jnp.where(bits < 0, bits ^ jnp.int32(0x7FFFFFFF), bits)
        kpos = off + lax.broadcasted_iota(jnp.int32, (kb, qb), 0)
        key = jnp.where(kpos <= qpos, key, jnp.int32(INT_MIN))
        key_ref[pl.ds(off, kb), :] = key
        u = key ^ jnp.int32(INT_MIN)
        for g in range(kb // 256):
            words = _bit_transpose32([u[g * 256 + 8 * t:g * 256 + 8 * t + 8, :] for t in range(32)])
            row = pl.multiple_of((bi * (kb // 256) + g) * 8, 8)
            for i in range(32):
                planes_ref[i, pl.ds(row, 8), :] = words[i]

    def idx_pair(i2, carry):
        idx_block(2 * i2)
        idx_block(2 * i2 + 1)
        return carry

    lax.fori_loop(0, nblk // 2, idx_pair, 0)

    @pl.when(nblk % 2 == 1)
    def _():
        idx_block(nblk - 1)

    ngrp_all = planes_ref.shape[1] // 8
    ngrp_live = nblk * (kb // 256)

    def radix_select(ng):
        rows = ng * 8
        grp = lax.broadcasted_iota(jnp.int32, (rows, qb), 0) // 8
        eq0 = jnp.where(grp < ngrp_live, jnp.int32(-1), jnp.int32(0))

        def count(words):
            c = jnp.sum(lax.population_count(words).reshape(ng, 8, qb), axis=0)
            return jnp.sum(c, axis=0, keepdims=True)

        def bit_body(i, carry):
            eq, n_above, prefix = carry
            x1 = planes_ref[2 * i, 0:rows, :]
            x0 = planes_ref[2 * i + 1, 0:rows, :]
            e1 = eq & x1
            n1, n11, nx0 = count(e1), count(e1 & x0), count(eq & x0)
            take1 = (n_above + n1) >= k_top
            n_above = jnp.where(take1, n_above, n_above + n1)
            n_lo = jnp.where(take1, n11, nx0 - n11)
            take0 = (n_above + n_lo) >= k_top
            n_above = jnp.where(take0, n_above, n_above + n_lo)
            flip1 = jnp.where(take1, jnp.int32(0), jnp.int32(-1))
            flip0 = jnp.where(take0, jnp.int32(0), jnp.int32(-1))
            eq = eq & (x1 ^ flip1) & (x0 ^ flip0)
            two = jnp.where(take1, 2, 0) | jnp.where(take0, 1, 0)
            return eq, n_above, prefix | jnp.left_shift(two, 30 - 2 * i)

        zrow = jnp.zeros((1, qb), jnp.int32)
        return lax.fori_loop(0, 16, bit_body, (eq0, zrow, zrow))[2]

    quarter = max(ngrp_all // 4, 1)
    sizes = list(range(quarter, ngrp_all + 1, quarter))
    which = jnp.minimum((ngrp_live - 1) // quarter, len(sizes) - 1)
    prefix = lax.switch(which, [functools.partial(radix_select, ng) for ng in sizes])
    thr = jnp.maximum(prefix ^ jnp.int32(INT_MIN), jnp.int32(INT_MIN + 1))
    thr_b = jnp.broadcast_to(thr, (RC, qb))

    m_ref[...] = jnp.full(m_ref.shape, NEG_BIG, F32)
    acc_ref[...] = jnp.zeros_like(acc_ref)

    def qk(bi, slot):
        off = pl.multiple_of(jnp.minimum(bi, nblk - 1) * kb, kb)
        sel = key_ref[pl.ds(off, kb), :] >= thr
        s = _dot(ckv_ref[0, pl.ds(off, kb), :], qlat_ref[...])
        s_ref[slot] = jnp.concatenate(
            [jnp.where(sel, s[:, h * qb:(h + 1) * qb], NEG_BIG) for h in range(nh)], axis=1)

    def pv(bi, slot):
        acc_ref[...] = alpha_ref[...] * acc_ref[...] + _dot(ckvT_ref[0, jnp.maximum(bi, 0)], p_ref[slot])

    def softmax(bi, slot, masked=True):
        off = pl.multiple_of(bi * kb, kb)
        nch = kb // RC

        def masks(c):
            return key_ref[pl.ds(off + c * RC, RC), :] >= thr_b

        mx = [jnp.full((8, qb), NEG_BIG, F32)] * nh
        for c in range(nch):
            mk = None if masked else masks(c)
            for h in range(nh):
                sc = s_ref[slot, c * RC:(c + 1) * RC, h * qb:(h + 1) * qb]
                if not masked:
                    sc = jnp.where(mk, sc, NEG_BIG)
                    s_ref[slot, c * RC:(c + 1) * RC, h * qb:(h + 1) * qb] = sc
                mx[h] = jnp.maximum(mx[h], jnp.max(sc.reshape(RC // 8, 8, qb), axis=0))
        m_new = []
        for h in range(nh):
            hs = slice(h * qb, (h + 1) * qb)
            m_old = m_ref[:, hs]
            mn = jnp.maximum(m_old, jnp.max(mx[h], axis=0, keepdims=True))
            alpha_ref[:, hs] = jnp.exp2(m_old - mn)
            m_ref[:, hs] = mn
            m_new.append(jnp.broadcast_to(mn, (RC, qb)))
        for c in range(nch):
            for h in range(nh):
                sc = s_ref[slot, c * RC:(c + 1) * RC, h * qb:(h + 1) * qb]
                pr = jnp.exp2(sc - m_new[h])
                p_ref[slot, c * RC:(c + 1) * RC, h * qb:(h + 1) * qb] = pr.astype(p_ref.dtype)

    def step(bi, cur):
        qk(bi + 1, 1 - cur)
        pv(bi - 1, 1 - cur)
        softmax(bi, cur)

    qk(1, 1)
    softmax(0, 0, masked=False)

    def pair_body(i2, carry):
        step(2 * i2 + 1, 1)
        step(2 * i2 + 2, 0)
        return carry

    lax.fori_loop(0, (nblk - 1) // 2, pair_body, 0)
    even = nblk % 2 == 0

    @pl.when(even)
    def _():
        pv(nblk - 2, 0)
        softmax(nblk - 1, 1)
        pv(nblk - 1, 1)

    @pl.when(jnp.logical_not(even))
    def _():
        pv(nblk - 1, 0)

    inv_l = 1.0 / acc_ref[KV_RANK:KV_RANK + 1, :]
    o = (acc_ref[:KV_RANK, :] * inv_l).astype(_MXU)
    yT = jnp.concatenate([_dot(wuvT_ref[h], o[:, h * qb:(h + 1) * qb]) for h in range(nh)], axis=0)
    o_ref[...] = yT.T


def _attention(proj, small, ckv, ckvT, kidx, wuk, wuvT, *, b, s, kb):
    nq = s // Q_BLOCK
    k_top = min(TOPK_MAX, s // 4)
    kern = functools.partial(_attn_kernel, k_top=k_top, kb=kb)
    hq = N_HEADS * Q_BLOCK
    return pl.pallas_call(
        kern,
        grid=(b, nq),
        in_specs=[
            pl.BlockSpec((Q_BLOCK, 512), lambda bi, j: (bi * nq + j, C_Q // 512)),
            pl.BlockSpec((Q_BLOCK, 512), lambda bi, j: (bi * nq + j, C_QIDX // 512)),
            pl.BlockSpec((Q_BLOCK, 128), lambda bi, j: (bi * nq + j, 0)),
            pl.BlockSpec((1, s, KV_RANK), lambda bi, j: (bi, 0, 0)),
            pl.BlockSpec((1, s // kb, KV_RANK + ONES_ROWS, kb), lambda bi, j: (bi, 0, 0, 0)),
            pl.BlockSpec((1, s, IDX_DIM), lambda bi, j: (bi, 0, 0)),
            pl.BlockSpec((N_HEADS, KV_RANK, HEAD_DIM), lambda bi, j: (0, 0, 0)),
            pl.BlockSpec((N_HEADS, HEAD_DIM, KV_RANK), lambda bi, j: (0, 0, 0)),
        ],
        out_specs=pl.BlockSpec((Q_BLOCK, N_HEADS * HEAD_DIM), lambda bi, j: (bi * nq + j, 0)),
        out_shape=jax.ShapeDtypeStruct((b * s, N_HEADS * HEAD_DIM), F32),
        scratch_shapes=[
            pltpu.VMEM((s, Q_BLOCK), jnp.int32),
            pltpu.VMEM((32, s // 32, Q_BLOCK), jnp.int32),
            pltpu.VMEM((KV_RANK, hq), _MXU),
            pltpu.VMEM((2, kb, hq), F32),
            pltpu.VMEM((2, kb, hq), _MXU),
            pltpu.VMEM((KV_RANK + ONES_ROWS, hq), F32),
            pltpu.VMEM((1, hq), F32),
            pltpu.VMEM((1, hq), F32),
        ],
        compiler_params=_cparams(("parallel", "arbitrary")),
        name="dsa_attn",
    )(proj, proj, small, ckv, ckvT, kidx, wuk, wuvT)


def _pool_rows(x, halo, pos, pw_ref, ps_ref):
    tm = x.shape[0]
    xe = jnp.concatenate([halo, x], axis=0)
    outs = []
    for g, w in enumerate(POOL_WINDOWS):
        sl = slice(g * POOL_GROUP, (g + 1) * POOL_GROUP)
        cur = xe[:, sl]
        span = 1
        while span < w:
            n = cur.shape[0]
            cur = cur[span:, :] + cur[:n - span, :]
            span *= 2
        win = cur[cur.shape[0] - tm:, :]
        count = jnp.minimum(pos + 1, w).astype(F32)
        pooled = win / count - x[:, sl]
        outs.append(_dot(pooled.astype(_MXU), pw_ref[g]))
    return jnp.concatenate(outs, axis=1) * ps_ref[...]


def _ssd_kernel(xbc_ref, halo_ref, z_ref, sm_ref, cw_ref, cb_ref, dtb_row_ref, dtb_col_ref, a_row_ref, a_col_ref,
                dsk_x_ref, expand_ref, ng_ref, o_ref, state_ref):
    c = pl.program_id(1)
    L = CHUNK
    P = SSD_HEAD_DIM

    @pl.when(c == 0)
    def _():
        state_ref[...] = jnp.zeros_like(state_ref)

    halo0 = halo_ref[...]
    halo0 = jnp.where(c > 0, halo0, jnp.zeros_like(halo0))
    for sub in range(SSD_SUB):
        rows = slice(sub * L, (sub + 1) * L)
        x_nat = xbc_ref[rows, :]
        halo = halo0 if sub == 0 else xbc_ref[sub * L - CONV_TAIL:sub * L, :]
        o_ref[rows, :] = _ssd_chunk(x_nat, halo, z_ref[rows, :], sm_ref[rows, :], cw_ref, cb_ref, dtb_row_ref,
                                    dtb_col_ref, a_row_ref, a_col_ref, dsk_x_ref, expand_ref, ng_ref, state_ref)


def _ssd_chunk(x_nat, halo, z, sm, cw_ref, cb_ref, dtb_row_ref, dtb_col_ref, a_row_ref, a_col_ref,
               dsk_x_ref, expand_ref, ng_ref, state_ref):
    L = CHUNK
    P = SSD_HEAD_DIM
    xe = jnp.concatenate([halo, x_nat], axis=0)
    cw = cw_ref[...]
    conv = cb_ref[...] + cw[3:4, :] * x_nat.astype(F32)
    srow = lax.broadcasted_iota(jnp.int32, (L, L + CONV_TAIL), 0)
    scol = lax.broadcasted_iota(jnp.int32, (L, L + CONV_TAIL), 1)
    for k in range(CONV_WIDTH - 1):
        sh = CONV_WIDTH - 1 - k
        sel = jnp.where(scol == srow + (CONV_TAIL - sh), 1.0, 0.0)
        conv = conv + cw[k:k + 1, :] * _select_rows(sel, xe)
    xc = _silu(conv)
    xs = xc[:, :D_INNER]
    bm = xc[:, D_INNER:D_INNER + N_GROUPS * D_STATE]
    cm = xc[:, D_INNER + N_GROUPS * D_STATE:]

    smT = sm.T
    dt = jax.nn.softplus(sm[:, SM_DT:SM_DT + SSD_HEADS] + dtb_row_ref[...])
    dtT = jax.nn.softplus(smT[SM_DT:SM_DT + SSD_HEADS, :] + dtb_col_ref[...])
    expand = expand_ref[...]
    dtx = _dot01_right(dt, expand)
    da = dt * a_row_ref[...]
    daT = dtT * a_col_ref[...]
    row = lax.broadcasted_iota(jnp.int32, (L, L), 0)
    col = lax.broadcasted_iota(jnp.int32, (L, L), 1)
    lower = row >= col
    tril = jnp.where(lower, 1.0, 0.0).astype(F32)
    triu = jnp.where(row <= col, 1.0, 0.0).astype(F32)
    acs = _dot01_left(tril, da)
    acsT = _dot01_right(daT, triu)
    acs_x = _dot01_right(acs, expand)
    a_last = acs_x[L - 1:L, :]
    ea_x = jnp.exp(acs_x)
    decay_x = jnp.exp(a_last - acs_x)
    cd_x = jnp.exp(a_last)

    X = xs * dtx
    Xb = X.astype(_MXU)
    Xd = (X * decay_x).astype(_MXU)
    state = state_ref[...]
    ys = []
    hpg = SSD_HEADS // N_GROUPS
    for g in range(N_GROUPS):
        bg = bm[:, g * D_STATE:(g + 1) * D_STATE]
        cg = cm[:, g * D_STATE:(g + 1) * D_STATE].astype(_MXU)
        bgT = bg.T.astype(_MXU)
        cb = _dot(cg, bgT)
        gs = slice(g * hpg * P, (g + 1) * hpg * P)
        y_off = _dot(cg, state[:, gs].astype(_MXU)) * ea_x[:, gs]
        state_ref[:, gs] = state[:, gs] * cd_x[:, gs] + _dot(bgT, Xd[:, gs])
        for e in range(hpg):
            h = g * hpg + e
            hs = slice(h * P, (h + 1) * P)
            diff = acs[:, h:h + 1] - acsT[h:h + 1, :]
            lmat = jnp.exp(jnp.where(lower, diff, -jnp.inf))
            m = (cb * lmat).astype(_MXU)
            ys.append(_dot(m, Xb[:, hs]) + y_off[:, e * P:(e + 1) * P])
    y = jnp.concatenate(ys, axis=1) + xs * dsk_x_ref[...]
    y = y * _silu(z.astype(F32))
    return _rms(y, ng_ref[...])


def _ssd(proj, small, cw, cb, dtb, a_log, dsk, ng, *, b, s):
    blk = CHUNK * SSD_SUB
    nc = s // blk
    rep = lambda v: jnp.repeat(v, SSD_HEAD_DIM)[None, :]
    a = -jnp.exp(a_log)
    expand = jnp.repeat(jnp.eye(SSD_HEADS, dtype=F32), SSD_HEAD_DIM, axis=1)
    const2 = lambda bi, c: (0, 0)
    return pl.pallas_call(
        _ssd_kernel,
        grid=(b, nc),
        in_specs=[
            pl.BlockSpec((blk, CONV_DIM), lambda bi, c: (bi * nc + c, C_XBC // CONV_DIM)),
            pl.BlockSpec((CONV_TAIL, CONV_DIM),
                         lambda bi, c: (jnp.maximum((bi * nc + c) * (blk // CONV_TAIL) - 1, 0), C_XBC // CONV_DIM)),
            pl.BlockSpec((blk, D_INNER), lambda bi, c: (bi * nc + c, C_Z // D_INNER)),
            pl.BlockSpec((blk, 128), lambda bi, c: (bi * nc + c, 0)),
            pl.BlockSpec((CONV_WIDTH, CONV_DIM), const2),
            pl.BlockSpec((1, CONV_DIM), const2),
            pl.BlockSpec((1, SSD_HEADS), const2),
            pl.BlockSpec((SSD_HEADS, 1), const2),
            pl.BlockSpec((1, SSD_HEADS), const2),
            pl.BlockSpec((SSD_HEADS, 1), const2),
            pl.BlockSpec((1, D_INNER), const2),
            pl.BlockSpec((SSD_HEADS, D_INNER), const2),
            pl.BlockSpec((1, D_INNER), const2),
        ],
        out_specs=pl.BlockSpec((blk, D_INNER), lambda bi, c: (bi * nc + c, 0)),
        out_shape=jax.ShapeDtypeStruct((b * s, D_INNER), F32),
        scratch_shapes=[pltpu.VMEM((D_STATE, D_INNER), F32)],
        compiler_params=_cparams(("parallel", "arbitrary")),
        name="ssd",
    )(proj, proj, proj, small, cw, cb[None, :], dtb[None, :], dtb[:, None], a[None, :], a[:, None],
      rep(dsk), expand, ng[None, :])


def _merge_kernel(ya_ref, xp_ref, halo_ref, ys_ref, ga_ref, gp_ref, gs_ref, h_ref, pw_ref, ps_ref,
                  wa_ref, wp_ref, ws_ref, wo_ref, o_ref, *, seq):
    tm = h_ref.shape[0]
    start = (pl.program_id(0) * tm) % seq
    halo = jnp.where(start > 0, halo_ref[...].astype(F32), 0.0)
    pos = start + lax.broadcasted_iota(jnp.int32, (tm, POOL_GROUP), 0)
    yp = _pool_rows(xp_ref[...].astype(F32), halo, pos, pw_ref, ps_ref)
    m = _sigmoid(ga_ref[...].astype(F32)) * _dot(ya_ref[...].astype(_MXU), wa_ref[...])
    m = m + _sigmoid(gp_ref[...].astype(F32)) * _dot(yp.astype(_MXU), wp_ref[...])
    m = m + _sigmoid(gs_ref[...].astype(F32)) * _dot(ys_ref[...].astype(_MXU), ws_ref[...])
    o_ref[...] = h_ref[...] + _dot(m.astype(_MXU), wo_ref[...])


def _merge(ya, ys, proj, h, pw, ps, wa, wp, ws, wo, *, seq, tm):
    t, d = h.shape
    row = lambda i: (i, 0)
    const = lambda i: (0, 0)
    gate = lambda k: (lambda i: (i, C_GATE // d + k))
    return pl.pallas_call(
        functools.partial(_merge_kernel, seq=seq),
        grid=(t // tm,),
        in_specs=[
            pl.BlockSpec((tm, ya.shape[1]), row),
            pl.BlockSpec((tm, POOL_WIDTH), lambda i: (i, C_POOL // POOL_WIDTH)),
            pl.BlockSpec((16, POOL_WIDTH), lambda i: (jnp.maximum(i * (tm // 16) - 1, 0), C_POOL // POOL_WIDTH)),
            pl.BlockSpec((tm, ys.shape[1]), row),
            pl.BlockSpec((tm, d), gate(0)),
            pl.BlockSpec((tm, d), gate(1)),
            pl.BlockSpec((tm, d), gate(2)),
            pl.BlockSpec((tm, d), row),
            pl.BlockSpec(pw.shape, lambda i: (0, 0, 0)),
            pl.BlockSpec(ps.shape, const),
            pl.BlockSpec(wa.shape, const),
            pl.BlockSpec(wp.shape, const),
            pl.BlockSpec(ws.shape, const),
            pl.BlockSpec(wo.shape, const),
        ],
        out_specs=pl.BlockSpec((tm, d), row),
        out_shape=jax.ShapeDtypeStruct((t, d), F32),
        compiler_params=_cparams(("parallel",)),
        name="merge",
    )(ya, proj, proj, ys, proj, proj, proj, h, pw, ps, wa, wp, ws, wo)


def _reorder_w_in(w):
    o_q, o_ckv, o_qidx, o_widx, o_kidx, o_pool, o_z, o_xbc, o_dt, o_gate = (
        0, 512, 768, 1280, 1288, 1352, 1864, 2888, 4424, 4440)
    d = w.shape[0]
    small = jnp.concatenate([
        w[:, o_kidx:o_kidx + IDX_DIM], w[:, o_widx:o_widx + IDX_HEADS], w[:, o_dt:o_dt + SSD_HEADS],
        jnp.zeros((d, 128 - IDX_DIM - IDX_HEADS - SSD_HEADS), w.dtype)], axis=1)
    wb = w.astype(_MXU)
    used = CONV_DIM + 512 + D_INNER + 3 * D_MODEL + 512 + POOL_WIDTH + KV_RANK
    wide = jnp.concatenate([
        wb[:, o_xbc:o_xbc + CONV_DIM], wb[:, o_q:o_q + 512], wb[:, o_z:o_z + D_INNER],
        wb[:, o_gate:o_gate + 3 * D_MODEL], wb[:, o_qidx:o_qidx + 512], wb[:, o_pool:o_pool + POOL_WIDTH],
        wb[:, o_ckv:o_ckv + KV_RANK], jnp.zeros((d, PROJ_COLS - used), _MXU)], axis=1)
    return wide, small.astype(_MXU)


def kernel(x, p, ffn1_norm, ffn1_w_gate, ffn1_w_up, ffn1_w_down, mix_norm, w_in, kv_norm, idx_k_norm, w_uk, w_uv, pool_w, pool_scale, conv_w, conv_b, dt_bias, a_log, d_skip, ssd_norm, w_br_attn, w_br_pool, w_br_ssd, w_out, ffn2_norm, ffn2_w_gate, ffn2_w_up, ffn2_w_down, ple_norm, ple_w_gate, ple_w_proj, final_norm):
    b, s, d = x.shape
    depth = w_in.shape[0]
    t = b * s
    tm_small = min(512, t)
    kb = min(512, s)
    mx = lambda a: a.astype(_MXU)

    h = x.reshape(t, d)
    for i in range(depth):
        h = _ffn(h, ffn1_norm[i][None], mx(ffn1_w_gate[i]), mx(ffn1_w_up[i]), mx(ffn1_w_down[i]),
                 tm=tm_small)
        w_wide, w_small = _reorder_w_in(w_in[i])
        proj, small, ckv, ckvT, kidx = _inproj(h, mix_norm[i][None], w_wide, w_small, kv_norm[i][None],
                                               idx_k_norm[i][None], b=b, s=s, tm=kb, tn=2560)
        wuk = mx(jnp.transpose(w_uk[i], (1, 0, 2)))
        wuvT = mx(jnp.transpose(w_uv[i], (1, 2, 0)))
        y_attn = _attention(proj, small, ckv, ckvT, kidx, wuk, wuvT, b=b, s=s, kb=kb)
        y_ssd = _ssd(proj, small, conv_w[i], conv_b[i], dt_bias[i], a_log[i], d_skip[i], ssd_norm[i], b=b, s=s)
        h = _merge(y_attn, y_ssd, proj, h, mx(pool_w[i]), pool_scale[i][None], mx(w_br_attn[i]),
                   mx(w_br_pool[i]), mx(w_br_ssd[i]), mx(w_out[i]), seq=s, tm=tm_small)
        ple = (p[i].reshape(t, -1), ple_norm[i][None], mx(ple_w_gate[i]), mx(ple_w_proj[i]), final_norm[None])
        h = _ffn(h, ffn2_norm[i][None], mx(ffn2_w_gate[i]), mx(ffn2_w_up[i]), mx(ffn2_w_down[i]), ple,
                 tm=tm_small, final=(i == depth - 1))
    return h.reshape(b, s, d)
```

```python
import functools

import jax
import jax.numpy as jnp
from jax import lax
from jax.experimental import pallas as pl
from jax.experimental.pallas import tpu as pltpu

_MXU = jnp.bfloat16
F32 = jnp.float32

D_MODEL = 1024
N_HEADS = 8
HEAD_DIM = 64
KV_RANK = 256
IDX_HEADS = 8
IDX_DIM = 64
TOPK_MAX = 256
Q_BLOCK = 256
POOL_WINDOWS = (2, 4, 8, 16)
POOL_WIDTH = 512
POOL_GROUP = 128
SSD_HEADS = 16
SSD_HEAD_DIM = 64
D_INNER = 1024
N_GROUPS = 2
D_STATE = 128
CONV_WIDTH = 4
CONV_DIM = D_INNER + 2 * N_GROUPS * D_STATE
CHUNK = 128
D_FF = 2816
PLE_DIM = 256
EPS = 1e-6

C_XBC = 0
C_Q = 1536
C_Z = 2048
C_GATE = 3072
C_QIDX = 6144
C_POOL = 6656
C_CKV = 7168
PROJ_COLS = 7680
SM_KIDX = 0
SM_WIDX = 64
SM_DT = 72

INT_MIN = -2147483648
LOG2E = 1.4426950408889634
RC = 32
ONES_ROWS = 16
SSD_SUB = 8
CONV_TAIL = 16
NEG_BIG = -1e30
VMEM_LIMIT = 56 * 1024 * 1024


def _cparams(sem):
    return pltpu.CompilerParams(dimension_semantics=sem, vmem_limit_bytes=VMEM_LIMIT)


def _rms(x, g):
    return x * lax.rsqrt(jnp.mean(x * x, axis=-1, keepdims=True) + EPS) * g


def _dot(a, b):
    return jnp.dot(a, b, preferred_element_type=F32)


def _sigmoid(x):
    return 0.5 * jnp.tanh(0.5 * x) + 0.5


def _silu(x):
    h = 0.5 * x
    return h * jnp.tanh(h) + h


def _split3(a):
    hi = a.astype(jnp.bfloat16)
    r1 = a - hi.astype(F32)
    mid = r1.astype(jnp.bfloat16)
    lo = (r1 - mid.astype(F32)).astype(jnp.bfloat16)
    return hi, mid, lo


def _dot01_right(a, b01):
    b = b01.astype(jnp.bfloat16)
    hi, mid, lo = _split3(a)
    return _dot(hi, b) + _dot(mid, b) + _dot(lo, b)


def _select_rows(a01, b):
    if b.dtype == jnp.bfloat16:
        return _dot(a01.astype(jnp.bfloat16), b)
    return _dot01_left(a01, b)


def _dot01_left(a01, b):
    a = a01.astype(jnp.bfloat16)
    hi, mid, lo = _split3(b)
    return _dot(a, hi) + _dot(a, mid) + _dot(a, lo)


def _ffn_kernel(h_ref, g_ref, wg_ref, wu_ref, wd_ref, *rest, ple, final):
    o_ref = rest[-1]
    h = h_ref[...]
    xn = _rms(h, g_ref[...]).astype(_MXU)
    a = _silu(_dot(xn, wg_ref[...])) * _dot(xn, wu_ref[...])
    h = h + 0.5 * _dot(a.astype(_MXU), wd_ref[...])
    if ple:
        p_ref, gp_ref, wpg_ref, wpp_ref, fg_ref = rest[:-1]
        gate = _sigmoid(_dot(_rms(h, gp_ref[...]).astype(_MXU), wpg_ref[...]))
        h = h + gate * _dot(p_ref[...].astype(_MXU), wpp_ref[...])
        if final:
            h = _rms(h, fg_ref[...])
    o_ref[...] = h


def _ffn(h, g, wg, wu, wd, ple=None, *, tm, final=False):
    t, d = h.shape
    ff = wg.shape[1]
    once = dict(pipeline_mode=pl.Buffered(1))
    row = lambda i: (i, 0)
    const = lambda i: (0, 0)
    in_specs = [
        pl.BlockSpec((tm, d), row),
        pl.BlockSpec((1, d), const),
        pl.BlockSpec((d, ff), const, **once),
        pl.BlockSpec((d, ff), const, **once),
        pl.BlockSpec((ff, d), const, **once),
    ]
    args = [h, g, wg, wu, wd]
    if ple is not None:
        p, gp, wpg, wpp, fg = ple
        in_specs += [
            pl.BlockSpec((tm, p.shape[1]), row),
            pl.BlockSpec((1, d), const),
            pl.BlockSpec(wpg.shape, const, **once),
            pl.BlockSpec(wpp.shape, const, **once),
            pl.BlockSpec((1, d), const),
        ]
        args += [p, gp, wpg, wpp, fg]
    return pl.pallas_call(
        functools.partial(_ffn_kernel, ple=ple is not None, final=final),
        grid=(t // tm,),
        in_specs=in_specs,
        out_specs=pl.BlockSpec((tm, d), row),
        out_shape=jax.ShapeDtypeStruct((t, d), F32),
        compiler_params=_cparams(("parallel",)),
        name="ffn",
    )(*args)


def _inproj_kernel(h_ref, g_ref, w_ref, ws_ref, gkv_ref, gk_ref, o_ref, os_ref, ckv_o, ckvT_o, kidx_o, *, tn):
    xn = _rms(h_ref[...], g_ref[...]).astype(_MXU)
    small = _dot(xn, ws_ref[...])
    os_ref[...] = small
    for j in range(w_ref.shape[1] // tn):
        res = _dot(xn, w_ref[:, j * tn:(j + 1) * tn])
        o_ref[:, j * tn:(j + 1) * tn] = res.astype(o_ref.dtype)
        if j * tn <= C_CKV and C_CKV + KV_RANK <= (j + 1) * tn:
            c_raw = res[:, C_CKV - j * tn:C_CKV - j * tn + KV_RANK]
    c = _rms(c_raw, gkv_ref[...])
    ckv_o[0] = c.astype(ckv_o.dtype)
    ones = jnp.ones((ONES_ROWS, c.shape[0]), F32)
    ckvT_o[0, 0] = jnp.concatenate([c.T, ones], axis=0).astype(ckvT_o.dtype)
    kidx_o[0] = _rms(small[:, SM_KIDX:SM_KIDX + IDX_DIM], gk_ref[...]).astype(kidx_o.dtype)


def _inproj(h, g, w, ws, gkv, gk, *, b, s, tm, tn):
    t, d = h.shape
    n = w.shape[1]
    nk = s // tm
    once = dict(pipeline_mode=pl.Buffered(1))
    return pl.pallas_call(
        functools.partial(_inproj_kernel, tn=tn),
        grid=(t // tm,),
        in_specs=[
            pl.BlockSpec((tm, d), lambda i: (i, 0)),
            pl.BlockSpec((1, d), lambda i: (0, 0)),
            pl.BlockSpec((d, n), lambda i: (0, 0), **once),
            pl.BlockSpec((d, 128), lambda i: (0, 0), **once),
            pl.BlockSpec((1, KV_RANK), lambda i: (0, 0)),
            pl.BlockSpec((1, IDX_DIM), lambda i: (0, 0)),
        ],
        out_specs=[
            pl.BlockSpec((tm, n), lambda i: (i, 0)),
            pl.BlockSpec((tm, 128), lambda i: (i, 0)),
            pl.BlockSpec((1, tm, KV_RANK), lambda i: (i // nk, i % nk, 0)),
            pl.BlockSpec((1, 1, KV_RANK + ONES_ROWS, tm), lambda i: (i // nk, i % nk, 0, 0)),
            pl.BlockSpec((1, tm, IDX_DIM), lambda i: (i // nk, i % nk, 0)),
        ],
        out_shape=[
            jax.ShapeDtypeStruct((t, n), _MXU),
            jax.ShapeDtypeStruct((t, 128), F32),
            jax.ShapeDtypeStruct((b, s, KV_RANK), _MXU),
            jax.ShapeDtypeStruct((b, nk, KV_RANK + ONES_ROWS, tm), _MXU),
            jax.ShapeDtypeStruct((b, s, IDX_DIM), _MXU),
        ],
        compiler_params=_cparams(("parallel",)),
        name="inproj",
    )(h, g, w, ws, gkv, gk)


def _bit_transpose32(words):
    words = list(words)
    j, m = 16, 0x0000FFFF
    while j:
        mi = jnp.int32(m - (1 << 32) if m >= (1 << 31) else m)
        for k in range(32):
            if k & j == 0:
                t = (words[k] ^ lax.shift_right_logical(words[k + j], jnp.int32(j))) & mi
                words[k] = words[k] ^ t
                words[k + j] = words[k + j] ^ lax.shift_left(t, jnp.int32(j))
        j >>= 1
        m = (m ^ (m << j)) & 0xFFFFFFFF
    return words


def _attn_kernel(q_ref, qi_ref, sm_ref, ckv_ref, ckvT_ref, kidx_ref, wuk_ref, wuvT_ref, o_ref,
                 key_ref, planes_ref, qlat_ref, s_ref, p_ref, acc_ref, m_ref, alpha_ref, bmax_ref,
                 *, k_top, kb):
    j = pl.program_id(1)
    nblk = (j * Q_BLOCK + Q_BLOCK + kb - 1) // kb
    nh = N_HEADS
    qb = Q_BLOCK

    qT = q_ref[...].astype(F32).T
    qiT = qi_ref[...].astype(F32).T
    smT = sm_ref[...].T
    w_row = jnp.concatenate([smT[SM_WIDX + h:SM_WIDX + h + 1, :] for h in range(nh)], axis=1)
    qiT_stack = jnp.concatenate(
        [qiT[h * IDX_DIM:(h + 1) * IDX_DIM, :] for h in range(nh)], axis=1).astype(_MXU)
    scale = HEAD_DIM ** -0.5
    qlatT = jnp.concatenate(
        [_dot(wuk_ref[h], qT[h * HEAD_DIM:(h + 1) * HEAD_DIM, :].astype(_MXU)) for h in range(nh)],
        axis=1)
    qlat_ref[...] = (qlatT * (scale * LOG2E)).astype(_MXU)
    s_ref[0] = _dot(ckv_ref[0, 0:kb, :], qlat_ref[...])

    qpos = j * qb + lax.broadcasted_iota(jnp.int32, (kb, qb), 1)

    def idx_block(bi):
        off = pl.multiple_of(bi * kb, kb)
        kblk = kidx_ref[0, pl.ds(off, kb), :]
        r = jnp.maximum(_dot(kblk, qiT_stack), 0.0) * w_row
        sc = r[:, 0:qb]
        for h in range(1, nh):
            sc = sc + r[:, h * qb:(h + 1) * qb]
        bits = pltpu.bitcast(sc, jnp.int32)
        key = jnp.where(bits < 0, bits ^ jnp.int32(0x7FFFFFFF), bits)
        kpos = off + lax.broadcasted_iota(jnp.int32, (kb, qb), 0)
        key = jnp.where(kpos <= qpos, key, jnp.int32(INT_MIN))
        key_ref[pl.ds(off, kb), :] = key
        u = key ^ jnp.int32(INT_MIN)
        for g in range(kb // 256):
            words = _bit_transpose32([u[g * 256 + 8 * t:g * 256 + 8 * t + 8, :] for t in range(32)])
            row = pl.multiple_of((bi * (kb // 256) + g) * 8, 8)
            for i in range(32):
                planes_ref[i, pl.ds(row, 8), :] = words[i]

    def idx_pair(i2, carry):
        idx_block(2 * i2)
        idx_block(2 * i2 + 1)
        return carry

    lax.fori_loop(0, nblk // 2, idx_pair, 0)

    @pl.when(nblk % 2 == 1)
    def _():
        idx_block(nblk - 1)

    ngrp_all = planes_ref.shape[1] // 8
    ngrp_live = nblk * (kb // 256)

    def radix_select(ng):
        rows = ng * 8
        grp = lax.broadcasted_iota(jnp.int32, (rows, qb), 0) // 8
        eq0 = jnp.where(grp < ngrp_live, jnp.int32(-1), jnp.int32(0))

        def count(words):
            c = jnp.sum(lax.population_count(words).reshape(ng, 8, qb), axis=0)
            return jnp.sum(c, axis=0, keepdims=True)

        def bit_body(i, carry):
            eq, n_above, prefix = carry
            x1 = planes_ref[2 * i, 0:rows, :]
            x0 = planes_ref[2 * i + 1, 0:rows, :]
            e1 = eq & x1
            n1, n11, nx0 = count(e1), count(e1 & x0), count(eq & x0)
            take1 = (n_above + n1) >= k_top
            n_above = jnp.where(take1, n_above, n_above + n1)
            n_lo = jnp.where(take1, n11, nx0 - n11)
            take0 = (n_above + n_lo) >= k_top
            n_above = jnp.where(take0, n_above, n_above + n_lo)
            flip1 = jnp.where(take1, jnp.int32(0), jnp.int32(-1))
            flip0 = jnp.where(take0, jnp.int32(0), jnp.int32(-1))
            eq = eq & (x1 ^ flip1) & (x0 ^ flip0)
            two = jnp.where(take1, 2, 0) | jnp.where(take0, 1, 0)
            return eq, n_above, prefix | jnp.left_shift(two, 30 - 2 * i)

        zrow = jnp.zeros((1, qb), jnp.int32)
        return lax.fori_loop(0, 16, bit_body, (eq0, zrow, zrow))[2]

    quarter = max(ngrp_all // 4, 1)
    sizes = list(range(quarter, ngrp_all + 1, quarter))
    which = jnp.minimum((ngrp_live - 1) // quarter, len(sizes) - 1)
    prefix = lax.switch(which, [functools.partial(radix_select, ng) for ng in sizes])
    thr = jnp.maximum(prefix ^ jnp.int32(INT_MIN), jnp.int32(INT_MIN + 1))
    thr_b = jnp.broadcast_to(thr, (RC, qb))

    m_ref[...] = jnp.full(m_ref.shape, NEG_BIG, F32)
    acc_ref[...] = jnp.zeros_like(acc_ref)

    def qk(bi, slot):
        off = pl.multiple_of(jnp.minimum(bi, nblk - 1) * kb, kb)
        sel = key_ref[pl.ds(off, kb), :] >= thr
        s = _dot(ckv_ref[0, pl.ds(off, kb), :], qlat_ref[...])
        cols = [jnp.where(sel, s[:, h * qb:(h + 1) * qb], NEG_BIG) for h in range(nh)]
        s_ref[slot] = jnp.concatenate(cols, axis=1)
        bmax_ref[slot] = jnp.concatenate([jnp.max(c, axis=0, keepdims=True) for c in cols], axis=1)

    def pv(bi, slot):
        acc_ref[...] = alpha_ref[...] * acc_ref[...] + _dot(ckvT_ref[0, jnp.maximum(bi, 0)], p_ref[slot])

    def softmax(bi, slot, masked=True):
        off = pl.multiple_of(bi * kb, kb)
        nch = kb // RC

        def masks(c):
            return key_ref[pl.ds(off + c * RC, RC), :] >= thr_b

        if masked:
            mx = [bmax_ref[slot, :, h * qb:(h + 1) * qb] for h in range(nh)]
        else:
            mx = [jnp.full((8, qb), NEG_BIG, F32)] * nh
            for c in range(nch):
                mk = masks(c)
                for h in range(nh):
                    sc = jnp.where(mk, s_ref[slot, c * RC:(c + 1) * RC, h * qb:(h + 1) * qb], NEG_BIG)
                    s_ref[slot, c * RC:(c + 1) * RC, h * qb:(h + 1) * qb] = sc
                    mx[h] = jnp.maximum(mx[h], jnp.max(sc.reshape(RC // 8, 8, qb), axis=0))
        m_new = []
        for h in range(nh):
            hs = slice(h * qb, (h + 1) * qb)
            m_old = m_ref[:, hs]
            mn = jnp.maximum(m_old, jnp.max(mx[h], axis=0, keepdims=True))
            alpha_ref[:, hs] = jnp.exp2(m_old - mn)
            m_ref[:, hs] = mn
            m_new.append(jnp.broadcast_to(mn, (RC, qb)))
        for c in range(nch):
            for h in range(nh):
                sc = s_ref[slot, c * RC:(c + 1) * RC, h * qb:(h + 1) * qb]
                pr = jnp.exp2(sc - m_new[h])
                p_ref[slot, c * RC:(c + 1) * RC, h * qb:(h + 1) * qb] = pr.astype(p_ref.dtype)

    def step(bi, cur):
        qk(bi + 1, 1 - cur)
        pv(bi - 1, 1 - cur)
        softmax(bi, cur)

    qk(1, 1)
    softmax(0, 0, masked=False)

    def pair_body(i2, carry):
        step(2 * i2 + 1, 1)
        step(2 * i2 + 2, 0)
        return carry

    lax.fori_loop(0, (nblk - 1) // 2, pair_body, 0)
    even = nblk % 2 == 0

    @pl.when(even)
    def _():
        pv(nblk - 2, 0)
        softmax(nblk - 1, 1)
        pv(nblk - 1, 1)

    @pl.when(jnp.logical_not(even))
    def _():
        pv(nblk - 1, 0)

    inv_l = 1.0 / acc_ref[KV_RANK:KV_RANK + 1, :]
    o = (acc_ref[:KV_RANK, :] * inv_l).astype(_MXU)
    yT = jnp.concatenate([_dot(wuvT_ref[h], o[:, h * qb:(h + 1) * qb]) for h in range(nh)], axis=0)
    o_ref[...] = yT.T


def _attention(proj, small, ckv, ckvT, kidx, wuk, wuvT, *, b, s, kb):
    nq = s // Q_BLOCK
    k_top = min(TOPK_MAX, s // 4)
    kern = functools.partial(_attn_kernel, k_top=k_top, kb=kb)
    hq = N_HEADS * Q_BLOCK
    return pl.pallas_call(
        kern,
        grid=(b, nq),
        in_specs=[
            pl.BlockSpec((Q_BLOCK, 512), lambda bi, j: (bi * nq + j, C_Q // 512)),
            pl.BlockSpec((Q_BLOCK, 512), lambda bi, j: (bi * nq + j, C_QIDX // 512)),
            pl.BlockSpec((Q_BLOCK, 128), lambda bi, j: (bi * nq + j, 0)),
            pl.BlockSpec((1, s, KV_RANK), lambda bi, j: (bi, 0, 0)),
            pl.BlockSpec((1, s // kb, KV_RANK + ONES_ROWS, kb), lambda bi, j: (bi, 0, 0, 0)),
            pl.BlockSpec((1, s, IDX_DIM), lambda bi, j: (bi, 0, 0)),
            pl.BlockSpec((N_HEADS, KV_RANK, HEAD_DIM), lambda bi, j: (0, 0, 0)),
            pl.BlockSpec((N_HEADS, HEAD_DIM, KV_RANK), lambda bi, j: (0, 0, 0)),
        ],
        out_specs=pl.BlockSpec((Q_BLOCK, N_HEADS * HEAD_DIM), lambda bi, j: (bi * nq + j, 0)),
        out_shape=jax.ShapeDtypeStruct((b * s, N_HEADS * HEAD_DIM), F32),
        scratch_shapes=[
            pltpu.VMEM((s, Q_BLOCK), jnp.int32),
            pltpu.VMEM((32, s // 32, Q_BLOCK), jnp.int32),
            pltpu.VMEM((KV_RANK, hq), _MXU),
            pltpu.VMEM((2, kb, hq), F32),
            pltpu.VMEM((2, kb, hq), _MXU),
            pltpu.VMEM((KV_RANK + ONES_ROWS, hq), F32),
            pltpu.VMEM((1, hq), F32),
            pltpu.VMEM((1, hq), F32),
            pltpu.VMEM((2, 1, hq), F32),
        ],
        compiler_params=_cparams(("parallel", "arbitrary")),
        name="dsa_attn",
    )(proj, proj, small, ckv, ckvT, kidx, wuk, wuvT)


def _pool_rows(x, halo, pos, pw_ref, ps_ref):
    tm = x.shape[0]
    xe = jnp.concatenate([halo, x], axis=0)
    outs = []
    for g, w in enumerate(POOL_WINDOWS):
        sl = slice(g * POOL_GROUP, (g + 1) * POOL_GROUP)
        cur = xe[:, sl]
        span = 1
        while span < w:
            n = cur.shape[0]
            cur = cur[span:, :] + cur[:n - span, :]
            span *= 2
        win = cur[cur.shape[0] - tm:, :]
        count = jnp.minimum(pos + 1, w).astype(F32)
        pooled = win / count - x[:, sl]
        outs.append(_dot(pooled.astype(_MXU), pw_ref[g]))
    return jnp.concatenate(outs, axis=1) * ps_ref[...]


def _ssd_kernel(xbc_ref, halo_ref, z_ref, sm_ref, cw_ref, cb_ref, dtb_row_ref, dtb_col_ref, a_row_ref, a_col_ref,
                dsk_x_ref, expand_ref, ng_ref, o_ref, state_ref):
    c = pl.program_id(1)
    L = CHUNK
    P = SSD_HEAD_DIM

    @pl.when(c == 0)
    def _():
        state_ref[...] = jnp.zeros_like(state_ref)

    halo0 = halo_ref[...]
    halo0 = jnp.where(c > 0, halo0, jnp.zeros_like(halo0))
    for sub in range(SSD_SUB):
        rows = slice(sub * L, (sub + 1) * L)
        x_nat = xbc_ref[rows, :]
        halo = halo0 if sub == 0 else xbc_ref[sub * L - CONV_TAIL:sub * L, :]
        o_ref[rows, :] = _ssd_chunk(x_nat, halo, z_ref[rows, :], sm_ref[rows, :], cw_ref, cb_ref, dtb_row_ref,
                                    dtb_col_ref, a_row_ref, a_col_ref, dsk_x_ref, expand_ref, ng_ref, state_ref)


def _ssd_chunk(x_nat, halo, z, sm, cw_ref, cb_ref, dtb_row_ref, dtb_col_ref, a_row_ref, a_col_ref,
               dsk_x_ref, expand_ref, ng_ref, state_ref):
    L = CHUNK
    P = SSD_HEAD_DIM
    xe = jnp.concatenate([halo, x_nat], axis=0)
    cw = cw_ref[...]
    conv = cb_ref[...] + cw[3:4, :] * x_nat.astype(F32)
    srow = lax.broadcasted_iota(jnp.int32, (L, L + CONV_TAIL), 0)
    scol = lax.broadcasted_iota(jnp.int32, (L, L + CONV_TAIL), 1)
    for k in range(CONV_WIDTH - 1):
        sh = CONV_WIDTH - 1 - k
        sel = jnp.where(scol == srow + (CONV_TAIL - sh), 1.0, 0.0)
        conv = conv + cw[k:k + 1, :] * _select_rows(sel, xe)
    xc = _silu(conv)
    xs = xc[:, :D_INNER]
    bm = xc[:, D_INNER:D_INNER + N_GROUPS * D_STATE]
    cm = xc[:, D_INNER + N_GROUPS * D_STATE:]

    smT = sm.T
    dt = jax.nn.softplus(sm[:, SM_DT:SM_DT + SSD_HEADS] + dtb_row_ref[...])
    dtT = jax.nn.softplus(smT[SM_DT:SM_DT + SSD_HEADS, :] + dtb_col_ref[...])
    expand = expand_ref[...]
    dtx = _dot01_right(dt, expand)
    da = dt * a_row_ref[...]
    daT = dtT * a_col_ref[...]
    row = lax.broadcasted_iota(jnp.int32, (L, L), 0)
    col = lax.broadcasted_iota(jnp.int32, (L, L), 1)
    lower = row >= col
    tril = jnp.where(lower, 1.0, 0.0).astype(F32)
    triu = jnp.where(row <= col, 1.0, 0.0).astype(F32)
    acs = _dot01_left(tril, da)
    acsT = _dot01_right(daT, triu)
    acs_x = _dot01_right(acs, expand)
    a_last = acs_x[L - 1:L, :]
    ea_x = jnp.exp(acs_x)
    decay_x = jnp.exp(a_last - acs_x)
    cd_x = jnp.exp(a_last)

    X = xs * dtx
    Xb = X.astype(_MXU)
    Xd = (X * decay_x).astype(_MXU)
    state = state_ref[...]
    ys = []
    hpg = SSD_HEADS // N_GROUPS
    for g in range(N_GROUPS):
        bg = bm[:, g * D_STATE:(g + 1) * D_STATE]
        cg = cm[:, g * D_STATE:(g + 1) * D_STATE].astype(_MXU)
        bgT = bg.T.astype(_MXU)
        cb = _dot(cg, bgT)
        gs = slice(g * hpg * P, (g + 1) * hpg * P)
        y_off = _dot(cg, state[:, gs].astype(_MXU)) * ea_x[:, gs]
        state_ref[:, gs] = state[:, gs] * cd_x[:, gs] + _dot(bgT, Xd[:, gs])
        for e in range(hpg):
            h = g * hpg + e
            hs = slice(h * P, (h + 1) * P)
            diff = acs[:, h:h + 1] - acsT[h:h + 1, :]
            lmat = jnp.exp(jnp.where(lower, diff, -jnp.inf))
            m = (cb * lmat).astype(_MXU)
            ys.append(_dot(m, Xb[:, hs]) + y_off[:, e * P:(e + 1) * P])
    y = jnp.concatenate(ys, axis=1) + xs * dsk_x_ref[...]
    y = y * _silu(z.astype(F32))
    return _rms(y, ng_ref[...])


def _ssd(proj, small, cw, cb, dtb, a_log, dsk, ng, *, b, s):
    blk = CHUNK * SSD_SUB
    nc = s // blk
    rep = lambda v: jnp.repeat(v, SSD_HEAD_DIM)[None, :]
    a = -jnp.exp(a_log)
    expand = jnp.repeat(jnp.eye(SSD_HEADS, dtype=F32), SSD_HEAD_DIM, axis=1)
    const2 = lambda bi, c: (0, 0)
    return pl.pallas_call(
        _ssd_kernel,
        grid=(b, nc),
        in_specs=[
            pl.BlockSpec((blk, CONV_DIM), lambda bi, c: (bi * nc + c, C_XBC // CONV_DIM)),
            pl.BlockSpec((CONV_TAIL, CONV_DIM),
                         lambda bi, c: (jnp.maximum((bi * nc + c) * (blk // CONV_TAIL) - 1, 0), C_XBC // CONV_DIM)),
            pl.BlockSpec((blk, D_INNER), lambda bi, c: (bi * nc + c, C_Z // D_INNER)),
            pl.BlockSpec((blk, 128), lambda bi, c: (bi * nc + c, 0)),
            pl.BlockSpec((CONV_WIDTH, CONV_DIM), const2),
            pl.BlockSpec((1, CONV_DIM), const2),
            pl.BlockSpec((1, SSD_HEADS), const2),
            pl.BlockSpec((SSD_HEADS, 1), const2),
            pl.BlockSpec((1, SSD_HEADS), const2),
            pl.BlockSpec((SSD_HEADS, 1), const2),
            pl.BlockSpec((1, D_INNER), const2),
            pl.BlockSpec((SSD_HEADS, D_INNER), const2),
            pl.BlockSpec((1, D_INNER), const2),
        ],
        out_specs=pl.BlockSpec((blk, D_INNER), lambda bi, c: (bi * nc + c, 0)),
        out_shape=jax.ShapeDtypeStruct((b * s, D_INNER), F32),
        scratch_shapes=[pltpu.VMEM((D_STATE, D_INNER), F32)],
        compiler_params=_cparams(("parallel", "arbitrary")),
        name="ssd",
    )(proj, proj, proj, small, cw, cb[None, :], dtb[None, :], dtb[:, None], a[None, :], a[:, None],
      rep(dsk), expand, ng[None, :])


def _merge_kernel(ya_ref, xp_ref, halo_ref, ys_ref, ga_ref, gp_ref, gs_ref, h_ref, pw_ref, ps_ref,
                  wa_ref, wp_ref, ws_ref, wo_ref, o_ref, *, seq):
    tm = h_ref.shape[0]
    start = (pl.program_id(0) * tm) % seq
    halo = jnp.where(start > 0, halo_ref[...].astype(F32), 0.0)
    pos = start + lax.broadcasted_iota(jnp.int32, (tm, POOL_GROUP), 0)
    yp = _pool_rows(xp_ref[...].astype(F32), halo, pos, pw_ref, ps_ref)
    m = _sigmoid(ga_ref[...].astype(F32)) * _dot(ya_ref[...].astype(_MXU), wa_ref[...])
    m = m + _sigmoid(gp_ref[...].astype(F32)) * _dot(yp.astype(_MXU), wp_ref[...])
    m = m + _sigmoid(gs_ref[...].astype(F32)) * _dot(ys_ref[...].astype(_MXU), ws_ref[...])
    o_ref[...] = h_ref[...] + _dot(m.astype(_MXU), wo_ref[...])


def _merge(ya, ys, proj, h, pw, ps, wa, wp, ws, wo, *, seq, tm):
    t, d = h.shape
    row = lambda i: (i, 0)
    const = lambda i: (0, 0)
    gate = lambda k: (lambda i: (i, C_GATE // d + k))
    return pl.pallas_call(
        functools.partial(_merge_kernel, seq=seq),
        grid=(t // tm,),
        in_specs=[
            pl.BlockSpec((tm, ya.shape[1]), row),
            pl.BlockSpec((tm, POOL_WIDTH), lambda i: (i, C_POOL // POOL_WIDTH)),
            pl.BlockSpec((16, POOL_WIDTH), lambda i: (jnp.maximum(i * (tm // 16) - 1, 0), C_POOL // POOL_WIDTH)),
            pl.BlockSpec((tm, ys.shape[1]), row),
            pl.BlockSpec((tm, d), gate(0)),
            pl.BlockSpec((tm, d), gate(1)),
            pl.BlockSpec((tm, d), gate(2)),
            pl.BlockSpec((tm, d), row),
            pl.BlockSpec(pw.shape, lambda i: (0, 0, 0)),
            pl.BlockSpec(ps.shape, const),
            pl.BlockSpec(wa.shape, const),
            pl.BlockSpec(wp.shape, const),
            pl.BlockSpec(ws.shape, const),
            pl.BlockSpec(wo.shape, const),
        ],
        out_specs=pl.BlockSpec((tm, d), row),
        out_shape=jax.ShapeDtypeStruct((t, d), F32),
        compiler_params=_cparams(("parallel",)),
        name="merge",
    )(ya, proj, proj, ys, proj, proj, proj, h, pw, ps, wa, wp, ws, wo)


def _reorder_w_in(w):
    o_q, o_ckv, o_qidx, o_widx, o_kidx, o_pool, o_z, o_xbc, o_dt, o_gate = (
        0, 512, 768, 1280, 1288, 1352, 1864, 2888, 4424, 4440)
    d = w.shape[0]
    small = jnp.concatenate([
        w[:, o_kidx:o_kidx + IDX_DIM], w[:, o_widx:o_widx + IDX_HEADS], w[:, o_dt:o_dt + SSD_HEADS],
        jnp.zeros((d, 128 - IDX_DIM - IDX_HEADS - SSD_HEADS), w.dtype)], axis=1)
    wb = w.astype(_MXU)
    used = CONV_DIM + 512 + D_INNER + 3 * D_MODEL + 512 + POOL_WIDTH + KV_RANK
    wide = jnp.concatenate([
        wb[:, o_xbc:o_xbc + CONV_DIM], wb[:, o_q:o_q + 512], wb[:, o_z:o_z + D_INNER],
        wb[:, o_gate:o_gate + 3 * D_MODEL], wb[:, o_qidx:o_qidx + 512], wb[:, o_pool:o_pool + POOL_WIDTH],
        wb[:, o_ckv:o_ckv + KV_RANK], jnp.zeros((d, PROJ_COLS - used), _MXU)], axis=1)
    return wide, small.astype(_MXU)


def kernel(x, p, ffn1_norm, ffn1_w_gate, ffn1_w_up, ffn1_w_down, mix_norm, w_in, kv_norm, idx_k_norm, w_uk, w_uv, pool_w, pool_scale, conv_w, conv_b, dt_bias, a_log, d_skip, ssd_norm, w_br_attn, w_br_pool, w_br_ssd, w_out, ffn2_norm, ffn2_w_gate, ffn2_w_up, ffn2_w_down, ple_norm, ple_w_gate, ple_w_proj, final_norm):
    b, s, d = x.shape
    depth = w_in.shape[0]
    t = b * s
    tm_small = min(512, t)
    kb = min(512, s)
    mx = lambda a: a.astype(_MXU)

    h = x.reshape(t, d)
    for i in range(depth):
        h = _ffn(h, ffn1_norm[i][None], mx(ffn1_w_gate[i]), mx(ffn1_w_up[i]), mx(ffn1_w_down[i]),
                 tm=tm_small)
        w_wide, w_small = _reorder_w_in(w_in[i])
        proj, small, ckv, ckvT, kidx = _inproj(h, mix_norm[i][None], w_wide, w_small, kv_norm[i][None],
                                               idx_k_norm[i][None], b=b, s=s, tm=kb, tn=2560)
        wuk = mx(jnp.transpose(w_uk[i], (1, 0, 2)))
        wuvT = mx(jnp.transpose(w_uv[i], (1, 2, 0)))
        y_attn = _attention(proj, small, ckv, ckvT, kidx, wuk, wuvT, b=b, s=s, kb=kb)
        y_ssd = _ssd(proj, small, conv_w[i], conv_b[i], dt_bias[i], a_log[i], d_skip[i], ssd_norm[i], b=b, s=s)
        h = _merge(y_attn, y_ssd, proj, h, mx(pool_w[i]), pool_scale[i][None], mx(w_br_attn[i]),
                   mx(w_br_pool[i]), mx(w_br_ssd[i]), mx(w_out[i]), seq=s, tm=tm_small)
        ple = (p[i].reshape(t, -1), ple_norm[i][None], mx(ple_w_gate[i]), mx(ple_w_proj[i]), final_norm[None])
        h = _ffn(h, ffn2_norm[i][None], mx(ffn2_w_gate[i]), mx(ffn2_w_up[i]), mx(ffn2_w_down[i]), ple,
                 tm=tm_small, final=(i == depth - 1))
    return h.reshape(b, s, d)
```

```python
import functools

import jax
import jax.numpy as jnp
from jax import lax
from jax.experimental import pallas as pl
from jax.experimental.pallas import tpu as pltpu

_MXU = jnp.bfloat16
F32 = jnp.float32

D_MODEL = 1024
N_HEADS = 8
HEAD_DIM = 64
KV_RANK = 256
IDX_HEADS = 8
IDX_DIM = 64
TOPK_MAX = 256
Q_BLOCK = 256
POOL_WINDOWS = (2, 4, 8, 16)
POOL_WIDTH = 512
POOL_GROUP = 128
SSD_HEADS = 16
SSD_HEAD_DIM = 64
D_INNER = 1024
N_GROUPS = 2
D_STATE = 128
CONV_WIDTH = 4
CONV_DIM = D_INNER + 2 * N_GROUPS * D_STATE
CHUNK = 128
D_FF = 2816
PLE_DIM = 256
EPS = 1e-6

C_XBC = 0
C_Q = 1536
C_Z = 2048
C_GATE = 3072
C_QIDX = 6144
C_POOL = 6656
C_CKV = 7168
PROJ_COLS = 7680
SM_KIDX = 0
SM_WIDX = 64
SM_DT = 72

INT_MIN = -2147483648
LOG2E = 1.4426950408889634
RC = 32
ONES_ROWS = 16
SSD_SUB = 8
CONV_TAIL = 16
NEG_BIG = -1e30
VMEM_LIMIT = 56 * 1024 * 1024


def _cparams(sem):
    return pltpu.CompilerParams(dimension_semantics=sem, vmem_limit_bytes=VMEM_LIMIT)


def _rms(x, g):
    return x * lax.rsqrt(jnp.mean(x * x, axis=-1, keepdims=True) + EPS) * g


def _dot(a, b):
    return jnp.dot(a, b, preferred_element_type=F32)


def _sigmoid(x):
    return 0.5 * jnp.tanh(0.5 * x) + 0.5


def _silu(x):
    h = 0.5 * x
    return h * jnp.tanh(h) + h


def _split3(a):
    hi = a.astype(jnp.bfloat16)
    r1 = a - hi.astype(F32)
    mid = r1.astype(jnp.bfloat16)
    lo = (r1 - mid.astype(F32)).astype(jnp.bfloat16)
    return hi, mid, lo


def _dot01_right(a, b01):
    b = b01.astype(jnp.bfloat16)
    hi, mid, lo = _split3(a)
    return _dot(hi, b) + _dot(mid, b) + _dot(lo, b)


def _select_rows(a01, b):
    if b.dtype == jnp.bfloat16:
        return _dot(a01.astype(jnp.bfloat16), b)
    return _dot01_left(a01, b)


def _dot01_left(a01, b):
    a = a01.astype(jnp.bfloat16)
    hi, mid, lo = _split3(b)
    return _dot(a, hi) + _dot(a, mid) + _dot(a, lo)


def _ffn_kernel(h_ref, g_ref, wg_ref, wu_ref, wd_ref, *rest, ple, final):
    o_ref = rest[-1]
    h = h_ref[...]
    xn = _rms(h, g_ref[...]).astype(_MXU)
    a = _silu(_dot(xn, wg_ref[...])) * _dot(xn, wu_ref[...])
    h = h + 0.5 * _dot(a.astype(_MXU), wd_ref[...])
    if ple:
        p_ref, gp_ref, wpg_ref, wpp_ref, fg_ref = rest[:-1]
        gate = _sigmoid(_dot(_rms(h, gp_ref[...]).astype(_MXU), wpg_ref[...]))
        h = h + gate * _dot(p_ref[...].astype(_MXU), wpp_ref[...])
        if final:
            h = _rms(h, fg_ref[...])
    o_ref[...] = h


def _ffn(h, g, wg, wu, wd, ple=None, *, tm, final=False):
    t, d = h.shape
    ff = wg.shape[1]
    once = dict(pipeline_mode=pl.Buffered(1))
    row = lambda i: (i, 0)
    const = lambda i: (0, 0)
    in_specs = [
        pl.BlockSpec((tm, d), row),
        pl.BlockSpec((1, d), const),
        pl.BlockSpec((d, ff), const, **once),
        pl.BlockSpec((d, ff), const, **once),
        pl.BlockSpec((ff, d), const, **once),
    ]
    args = [h, g, wg, wu, wd]
    if ple is not None:
        p, gp, wpg, wpp, fg = ple
        in_specs += [
            pl.BlockSpec((tm, p.shape[1]), row),
            pl.BlockSpec((1, d), const),
            pl.BlockSpec(wpg.shape, const, **once),
            pl.BlockSpec(wpp.shape, const, **once),
            pl.BlockSpec((1, d), const),
        ]
        args += [p, gp, wpg, wpp, fg]
    return pl.pallas_call(
        functools.partial(_ffn_kernel, ple=ple is not None, final=final),
        grid=(t // tm,),
        in_specs=in_specs,
        out_specs=pl.BlockSpec((tm, d), row),
        out_shape=jax.ShapeDtypeStruct((t, d), F32),
        compiler_params=_cparams(("parallel",)),
        name="ffn",
    )(*args)


def _inproj_kernel(h_ref, g_ref, w_ref, ws_ref, gkv_ref, gk_ref, o_ref, os_ref, ckv_o, ckvT_o, kidx_o, *, tn):
    xn = _rms(h_ref[...], g_ref[...]).astype(_MXU)
    small = _dot(xn, ws_ref[...])
    os_ref[...] = small
    for j in range(w_ref.shape[1] // tn):
        res = _dot(xn, w_ref[:, j * tn:(j + 1) * tn])
        o_ref[:, j * tn:(j + 1) * tn] = res.astype(o_ref.dtype)
        if j * tn <= C_CKV and C_CKV + KV_RANK <= (j + 1) * tn:
            c_raw = res[:, C_CKV - j * tn:C_CKV - j * tn + KV_RANK]
    c = _rms(c_raw, gkv_ref[...])
    ckv_o[0] = c.astype(ckv_o.dtype)
    ones = jnp.ones((ONES_ROWS, c.shape[0]), F32)
    ckvT_o[0, 0] = jnp.concatenate([c.T, ones], axis=0).astype(ckvT_o.dtype)
    kidx_o[0] = _rms(small[:, SM_KIDX:SM_KIDX + IDX_DIM], gk_ref[...]).astype(kidx_o.dtype)


def _inproj(h, g, w, ws, gkv, gk, *, b, s, tm, tn):
    t, d = h.shape
    n = w.shape[1]
    nk = s // tm
    once = dict(pipeline_mode=pl.Buffered(1))
    return pl.pallas_call(
        functools.partial(_inproj_kernel, tn=tn),
        grid=(t // tm,),
        in_specs=[
            pl.BlockSpec((tm, d), lambda i: (i, 0)),
            pl.BlockSpec((1, d), lambda i: (0, 0)),
            pl.BlockSpec((d, n), lambda i: (0, 0), **once),
            pl.BlockSpec((d, 128), lambda i: (0, 0), **once),
            pl.BlockSpec((1, KV_RANK), lambda i: (0, 0)),
            pl.BlockSpec((1, IDX_DIM), lambda i: (0, 0)),
        ],
        out_specs=[
            pl.BlockSpec((tm, n), lambda i: (i, 0)),
            pl.BlockSpec((tm, 128), lambda i: (i, 0)),
            pl.BlockSpec((1, tm, KV_RANK), lambda i: (i // nk, i % nk, 0)),
            pl.BlockSpec((1, 1, KV_RANK + ONES_ROWS, tm), lambda i: (i // nk, i % nk, 0, 0)),
            pl.BlockSpec((1, tm, IDX_DIM), lambda i: (i // nk, i % nk, 0)),
        ],
        out_shape=[
            jax.ShapeDtypeStruct((t, n), _MXU),
            jax.ShapeDtypeStruct((t, 128), F32),
            jax.ShapeDtypeStruct((b, s, KV_RANK), _MXU),
            jax.ShapeDtypeStruct((b, nk, KV_RANK + ONES_ROWS, tm), _MXU),
            jax.ShapeDtypeStruct((b, s, IDX_DIM), _MXU),
        ],
        compiler_params=_cparams(("parallel",)),
        name="inproj",
    )(h, g, w, ws, gkv, gk)


def _bit_transpose32(words):
    words = list(words)
    j, m = 16, 0x0000FFFF
    while j:
        mi = jnp.int32(m - (1 << 32) if m >= (1 << 31) else m)
        for k in range(32):
            if k & j == 0:
                t = (words[k] ^ lax.shift_right_logical(words[k + j], jnp.int32(j))) & mi
                words[k] = words[k] ^ t
                words[k + j] = words[k + j] ^ lax.shift_left(t, jnp.int32(j))
        j >>= 1
        m = (m ^ (m << j)) & 0xFFFFFFFF
    return words


def _attn_kernel(q_ref, qi_ref, sm_ref, ckv_ref, ckvT_ref, kidx_ref, wuk_ref, wuvT_ref, o_ref,
                 key_ref, planes_ref, qlat_ref, s_ref, p_ref, acc_ref, m_ref, alpha_ref, bmax_ref,
                 *, k_top, kb):
    j = pl.program_id(1)
    nblk = (j * Q_BLOCK + Q_BLOCK + kb - 1) // kb
    nh = N_HEADS
    qb = Q_BLOCK

    qT = q_ref[...].astype(F32).T
    qiT = qi_ref[...].astype(F32).T
    smT = sm_ref[...].T
    w_row = jnp.concatenate([smT[SM_WIDX + h:SM_WIDX + h + 1, :] for h in range(nh)], axis=1)
    qiT_stack = jnp.concatenate(
        [qiT[h * IDX_DIM:(h + 1) * IDX_DIM, :] for h in range(nh)], axis=1).astype(_MXU)
    scale = HEAD_DIM ** -0.5
    qlatT = jnp.concatenate(
        [_dot(wuk_ref[h], qT[h * HEAD_DIM:(h + 1) * HEAD_DIM, :].astype(_MXU)) for h in range(nh)],
        axis=1)
    qlat_ref[...] = (qlatT * (scale * LOG2E)).astype(_MXU)
    s_ref[0] = _dot(ckv_ref[0, 0:kb, :], qlat_ref[...])

    qpos = j * qb + lax.broadcasted_iota(jnp.int32, (kb, qb), 1)

    def idx_block(bi):
        off = pl.multiple_of(bi * kb, kb)
        kblk = kidx_ref[0, pl.ds(off, kb), :]
        r = jnp.maximum(_dot(kblk, qiT_stack), 0.0) * w_row
        sc = r[:, 0:qb]
        for h in range(1, nh):
            sc = sc + r[:, h * qb:(h + 1) * qb]
        bits = pltpu.bitcast(sc, jnp.int32)
        key = jnp.where(bits < 0, bits ^ jnp.int32(0x7FFFFFFF), bits)
        kpos = off + lax.broadcasted_iota(jnp.int32, (kb, qb), 0)
        key = jnp.where(kpos <= qpos, key, jnp.int32(INT_MIN))
        key_ref[pl.ds(off, kb), :] = key
        u = key ^ jnp.int32(INT_MIN)
        for g in range(kb // 256):
            words = _bit_transpose32([u[g * 256 + 8 * t:g * 256 + 8 * t + 8, :] for t in range(32)])
            row = pl.multiple_of((bi * (kb // 256) + g) * 8, 8)
            for i in range(32):
                planes_ref[i, pl.ds(row, 8), :] = words[i]

    def idx_quad(i4, carry):
        for u in range(4):
            idx_block(4 * i4 + u)
        return carry

    lax.fori_loop(0, nblk // 4, idx_quad, 0)
    rem = nblk % 4
    base4 = nblk - rem

    @pl.when(rem >= 2)
    def _():
        idx_block(base4)
        idx_block(base4 + 1)

    @pl.when(rem % 2 == 1)
    def _():
        idx_block(nblk - 1)

    ngrp_all = planes_ref.shape[1] // 8
    ngrp_live = nblk * (kb // 256)

    def radix_select(ng):
        rows = ng * 8
        grp = lax.broadcasted_iota(jnp.int32, (rows, qb), 0) // 8
        eq0 = jnp.where(grp < ngrp_live, jnp.int32(-1), jnp.int32(0))

        def count(words):
            c = jnp.sum(lax.population_count(words).reshape(ng, 8, qb), axis=0)
            return jnp.sum(c, axis=0, keepdims=True)

        def bit_body(i, carry):
            eq, n_above, prefix = carry
            x1 = planes_ref[2 * i, 0:rows, :]
            x0 = planes_ref[2 * i + 1, 0:rows, :]
            e1 = eq & x1
            n1, n11, nx0 = count(e1), count(e1 & x0), count(eq & x0)
            take1 = (n_above + n1) >= k_top
            n_above = jnp.where(take1, n_above, n_above + n1)
            n_lo = jnp.where(take1, n11, nx0 - n11)
            take0 = (n_above + n_lo) >= k_top
            n_above = jnp.where(take0, n_above, n_above + n_lo)
            flip1 = jnp.where(take1, jnp.int32(0), jnp.int32(-1))
            flip0 = jnp.where(take0, jnp.int32(0), jnp.int32(-1))
            eq = eq & (x1 ^ flip1) & (x0 ^ flip0)
            two = jnp.where(take1, 2, 0) | jnp.where(take0, 1, 0)
            return eq, n_above, prefix | jnp.left_shift(two, 30 - 2 * i)

        zrow = jnp.zeros((1, qb), jnp.int32)
        return lax.fori_loop(0, 16, bit_body, (eq0, zrow, zrow))[2]

    quarter = max(ngrp_all // 4, 1)
    sizes = list(range(quarter, ngrp_all + 1, quarter))
    which = jnp.minimum((ngrp_live - 1) // quarter, len(sizes) - 1)
    prefix = lax.switch(which, [functools.partial(radix_select, ng) for ng in sizes])
    thr = jnp.maximum(prefix ^ jnp.int32(INT_MIN), jnp.int32(INT_MIN + 1))
    thr_b = jnp.broadcast_to(thr, (RC, qb))

    m_ref[...] = jnp.full(m_ref.shape, NEG_BIG, F32)
    acc_ref[...] = jnp.zeros_like(acc_ref)

    def qk(bi, slot):
        off = pl.multiple_of(jnp.minimum(bi, nblk - 1) * kb, kb)
        sel = key_ref[pl.ds(off, kb), :] >= thr
        s = _dot(ckv_ref[0, pl.ds(off, kb), :], qlat_ref[...])
        cols = [jnp.where(sel, s[:, h * qb:(h + 1) * qb], NEG_BIG) for h in range(nh)]
        s_ref[slot] = jnp.concatenate(cols, axis=1)
        bmax_ref[slot] = jnp.concatenate([jnp.max(c, axis=0, keepdims=True) for c in cols], axis=1)

    def pv(bi, slot):
        acc_ref[...] = alpha_ref[...] * acc_ref[...] + _dot(ckvT_ref[0, jnp.maximum(bi, 0)], p_ref[slot])

    def softmax(bi, slot, masked=True):
        off = pl.multiple_of(bi * kb, kb)
        nch = kb // RC

        def masks(c):
            return key_ref[pl.ds(off + c * RC, RC), :] >= thr_b

        if masked:
            mx = [bmax_ref[slot, :, h * qb:(h + 1) * qb] for h in range(nh)]
        else:
            mx = [jnp.full((8, qb), NEG_BIG, F32)] * nh
            for c in range(nch):
                mk = masks(c)
                for h in range(nh):
                    sc = jnp.where(mk, s_ref[slot, c * RC:(c + 1) * RC, h * qb:(h + 1) * qb], NEG_BIG)
                    s_ref[slot, c * RC:(c + 1) * RC, h * qb:(h + 1) * qb] = sc
                    mx[h] = jnp.maximum(mx[h], jnp.max(sc.reshape(RC // 8, 8, qb), axis=0))
        m_new = []
        for h in range(nh):
            hs = slice(h * qb, (h + 1) * qb)
            m_old = m_ref[:, hs]
            mn = jnp.maximum(m_old, jnp.max(mx[h], axis=0, keepdims=True))
            alpha_ref[:, hs] = jnp.exp2(m_old - mn)
            m_ref[:, hs] = mn
            m_new.append(jnp.broadcast_to(mn, (RC, qb)))
        for c in range(nch):
            for h in range(nh):
                sc = s_ref[slot, c * RC:(c + 1) * RC, h * qb:(h + 1) * qb]
                pr = jnp.exp2(sc - m_new[h])
                p_ref[slot, c * RC:(c + 1) * RC, h * qb:(h + 1) * qb] = pr.astype(p_ref.dtype)

    def step(bi, cur):
        qk(bi + 1, 1 - cur)
        pv(bi - 1, 1 - cur)
        softmax(bi, cur)

    qk(1, 1)
    softmax(0, 0, masked=False)

    def pair_body(i2, carry):
        step(2 * i2 + 1, 1)
        step(2 * i2 + 2, 0)
        return carry

    lax.fori_loop(0, (nblk - 1) // 2, pair_body, 0)
    even = nblk % 2 == 0

    @pl.when(even)
    def _():
        pv(nblk - 2, 0)
        softmax(nblk - 1, 1)
        pv(nblk - 1, 1)

    @pl.when(jnp.logical_not(even))
    def _():
        pv(nblk - 1, 0)

    inv_l = 1.0 / acc_ref[KV_RANK:KV_RANK + 1, :]
    o = (acc_ref[:KV_RANK, :] * inv_l).astype(_MXU)
    yT = jnp.concatenate([_dot(wuvT_ref[h], o[:, h * qb:(h + 1) * qb]) for h in range(nh)], axis=0)
    o_ref[...] = yT.T


def _attention(proj, small, ckv, ckvT, kidx, wuk, wuvT, *, b, s, kb):
    nq = s // Q_BLOCK
    k_top = min(TOPK_MAX, s // 4)
    kern = functools.partial(_attn_kernel, k_top=k_top, kb=kb)
    hq = N_HEADS * Q_BLOCK
    return pl.pallas_call(
        kern,
        grid=(b, nq),
        in_specs=[
            pl.BlockSpec((Q_BLOCK, 512), lambda bi, j: (bi * nq + j, C_Q // 512)),
            pl.BlockSpec((Q_BLOCK, 512), lambda bi, j: (bi * nq + j, C_QIDX // 512)),
            pl.BlockSpec((Q_BLOCK, 128), lambda bi, j: (bi * nq + j, 0)),
            pl.BlockSpec((1, s, KV_RANK), lambda bi, j: (bi, 0, 0)),
            pl.BlockSpec((1, s // kb, KV_RANK + ONES_ROWS, kb), lambda bi, j: (bi, 0, 0, 0)),
            pl.BlockSpec((1, s, IDX_DIM), lambda bi, j: (bi, 0, 0)),
            pl.BlockSpec((N_HEADS, KV_RANK, HEAD_DIM), lambda bi, j: (0, 0, 0)),
            pl.BlockSpec((N_HEADS, HEAD_DIM, KV_RANK), lambda bi, j: (0, 0, 0)),
        ],
        out_specs=pl.BlockSpec((Q_BLOCK, N_HEADS * HEAD_DIM), lambda bi, j: (bi * nq + j, 0)),
        out_shape=jax.ShapeDtypeStruct((b * s, N_HEADS * HEAD_DIM), F32),
        scratch_shapes=[
            pltpu.VMEM((s, Q_BLOCK), jnp.int32),
            pltpu.VMEM((32, s // 32, Q_BLOCK), jnp.int32),
            pltpu.VMEM((KV_RANK, hq), _MXU),
            pltpu.VMEM((2, kb, hq), F32),
            pltpu.VMEM((2, kb, hq), _MXU),
            pltpu.VMEM((KV_RANK + ONES_ROWS, hq), F32),
            pltpu.VMEM((1, hq), F32),
            pltpu.VMEM((1, hq), F32),
            pltpu.VMEM((2, 1, hq), F32),
        ],
        compiler_params=_cparams(("parallel", "arbitrary")),
        name="dsa_attn",
    )(proj, proj, small, ckv, ckvT, kidx, wuk, wuvT)


def _pool_rows(x, halo, pos, pw_ref, ps_ref):
    tm = x.shape[0]
    xe = jnp.concatenate([halo, x], axis=0)
    outs = []
    for g, w in enumerate(POOL_WINDOWS):
        sl = slice(g * POOL_GROUP, (g + 1) * POOL_GROUP)
        cur = xe[:, sl]
        span = 1
        while span < w:
            n = cur.shape[0]
            cur = cur[span:, :] + cur[:n - span, :]
            span *= 2
        win = cur[cur.shape[0] - tm:, :]
        count = jnp.minimum(pos + 1, w).astype(F32)
        pooled = win / count - x[:, sl]
        outs.append(_dot(pooled.astype(_MXU), pw_ref[g]))
    return jnp.concatenate(outs, axis=1) * ps_ref[...]


def _ssd_kernel(xbc_ref, halo_ref, z_ref, sm_ref, cw_ref, cb_ref, dtb_row_ref, dtb_col_ref, a_row_ref, a_col_ref,
                dsk_x_ref, expand_ref, ng_ref, o_ref, state_ref):
    c = pl.program_id(1)
    L = CHUNK
    P = SSD_HEAD_DIM

    @pl.when(c == 0)
    def _():
        state_ref[...] = jnp.zeros_like(state_ref)

    halo0 = halo_ref[...]
    halo0 = jnp.where(c > 0, halo0, jnp.zeros_like(halo0))
    for sub in range(SSD_SUB):
        rows = slice(sub * L, (sub + 1) * L)
        x_nat = xbc_ref[rows, :]
        halo = halo0 if sub == 0 else xbc_ref[sub * L - CONV_TAIL:sub * L, :]
        o_ref[rows, :] = _ssd_chunk(x_nat, halo, z_ref[rows, :], sm_ref[rows, :], cw_ref, cb_ref, dtb_row_ref,
                                    dtb_col_ref, a_row_ref, a_col_ref, dsk_x_ref, expand_ref, ng_ref, state_ref)


def _ssd_chunk(x_nat, halo, z, sm, cw_ref, cb_ref, dtb_row_ref, dtb_col_ref, a_row_ref, a_col_ref,
               dsk_x_ref, expand_ref, ng_ref, state_ref):
    L = CHUNK
    P = SSD_HEAD_DIM
    xe = jnp.concatenate([halo, x_nat], axis=0)
    cw = cw_ref[...]
    conv = cb_ref[...] + cw[3:4, :] * x_nat.astype(F32)
    srow = lax.broadcasted_iota(jnp.int32, (L, L + CONV_TAIL), 0)
    scol = lax.broadcasted_iota(jnp.int32, (L, L + CONV_TAIL), 1)
    for k in range(CONV_WIDTH - 1):
        sh = CONV_WIDTH - 1 - k
        sel = jnp.where(scol == srow + (CONV_TAIL - sh), 1.0, 0.0)
        conv = conv + cw[k:k + 1, :] * _select_rows(sel, xe)
    xc = _silu(conv)
    xs = xc[:, :D_INNER]
    bm = xc[:, D_INNER:D_INNER + N_GROUPS * D_STATE]
    cm = xc[:, D_INNER + N_GROUPS * D_STATE:]

    smT = sm.T
    dt = jax.nn.softplus(sm[:, SM_DT:SM_DT + SSD_HEADS] + dtb_row_ref[...])
    dtT = jax.nn.softplus(smT[SM_DT:SM_DT + SSD_HEADS, :] + dtb_col_ref[...])
    expand = expand_ref[...]
    dtx = _dot01_right(dt, expand)
    da = dt * a_row_ref[...]
    daT = dtT * a_col_ref[...]
    row = lax.broadcasted_iota(jnp.int32, (L, L), 0)
    col = lax.broadcasted_iota(jnp.int32, (L, L), 1)
    lower = row >= col
    tril = jnp.where(lower, 1.0, 0.0).astype(F32)
    triu = jnp.where(row <= col, 1.0, 0.0).astype(F32)
    acs = _dot01_left(tril, da)
    acsT = _dot01_right(daT, triu)
    acs_x = _dot01_right(acs, expand)
    a_last = acs_x[L - 1:L, :]
    ea_x = jnp.exp(acs_x)
    decay_x = jnp.exp(a_last - acs_x)
    cd_x = jnp.exp(a_last)

    X = xs * dtx
    Xb = X.astype(_MXU)
    Xd = (X * decay_x).astype(_MXU)
    state = state_ref[...]
    ys = []
    hpg = SSD_HEADS // N_GROUPS
    for g in range(N_GROUPS):
        bg = bm[:, g * D_STATE:(g + 1) * D_STATE]
        cg = cm[:, g * D_STATE:(g + 1) * D_STATE].astype(_MXU)
        bgT = bg.T.astype(_MXU)
        cb = _dot(cg, bgT)
        gs = slice(g * hpg * P, (g + 1) * hpg * P)
        y_off = _dot(cg, state[:, gs].astype(_MXU)) * ea_x[:, gs]
        state_ref[:, gs] = state[:, gs] * cd_x[:, gs] + _dot(bgT, Xd[:, gs])
        for e in range(hpg):
            h = g * hpg + e
            hs = slice(h * P, (h + 1) * P)
            diff = acs[:, h:h + 1] - acsT[h:h + 1, :]
            lmat = jnp.exp(jnp.where(lower, diff, -jnp.inf))
            m = (cb * lmat).astype(_MXU)
            ys.append(_dot(m, Xb[:, hs]) + y_off[:, e * P:(e + 1) * P])
    y = jnp.concatenate(ys, axis=1) + xs * dsk_x_ref[...]
    y = y * _silu(z.astype(F32))
    return _rms(y, ng_ref[...])


def _ssd(proj, small, cw, cb, dtb, a_log, dsk, ng, *, b, s):
    blk = CHUNK * SSD_SUB
    nc = s // blk
    rep = lambda v: jnp.repeat(v, SSD_HEAD_DIM)[None, :]
    a = -jnp.exp(a_log)
    expand = jnp.repeat(jnp.eye(SSD_HEADS, dtype=F32), SSD_HEAD_DIM, axis=1)
    const2 = lambda bi, c: (0, 0)
    return pl.pallas_call(
        _ssd_kernel,
        grid=(b, nc),
        in_specs=[
            pl.BlockSpec((blk, CONV_DIM), lambda bi, c: (bi * nc + c, C_XBC // CONV_DIM)),
            pl.BlockSpec((CONV_TAIL, CONV_DIM),
                         lambda bi, c: (jnp.maximum((bi * nc + c) * (blk // CONV_TAIL) - 1, 0), C_XBC // CONV_DIM)),
            pl.BlockSpec((blk, D_INNER), lambda bi, c: (bi * nc + c, C_Z // D_INNER)),
            pl.BlockSpec((blk, 128), lambda bi, c: (bi * nc + c, 0)),
            pl.BlockSpec((CONV_WIDTH, CONV_DIM), const2),
            pl.BlockSpec((1, CONV_DIM), const2),
            pl.BlockSpec((1, SSD_HEADS), const2),
            pl.BlockSpec((SSD_HEADS, 1), const2),
            pl.BlockSpec((1, SSD_HEADS), const2),
            pl.BlockSpec((SSD_HEADS, 1), const2),
            pl.BlockSpec((1, D_INNER), const2),
            pl.BlockSpec((SSD_HEADS, D_INNER), const2),
            pl.BlockSpec((1, D_INNER), const2),
        ],
        out_specs=pl.BlockSpec((blk, D_INNER), lambda bi, c: (bi * nc + c, 0)),
        out_shape=jax.ShapeDtypeStruct((b * s, D_INNER), F32),
        scratch_shapes=[pltpu.VMEM((D_STATE, D_INNER), F32)],
        compiler_params=_cparams(("parallel", "arbitrary")),
        name="ssd",
    )(proj, proj, proj, small, cw, cb[None, :], dtb[None, :], dtb[:, None], a[None, :], a[:, None],
      rep(dsk), expand, ng[None, :])


def _merge_kernel(ya_ref, xp_ref, halo_ref, ys_ref, ga_ref, gp_ref, gs_ref, h_ref, pw_ref, ps_ref,
                  wa_ref, wp_ref, ws_ref, wo_ref, o_ref, *, seq):
    tm = h_ref.shape[0]
    start = (pl.program_id(0) * tm) % seq
    halo = jnp.where(start > 0, halo_ref[...].astype(F32), 0.0)
    pos = start + lax.broadcasted_iota(jnp.int32, (tm, POOL_GROUP), 0)
    yp = _pool_rows(xp_ref[...].astype(F32), halo, pos, pw_ref, ps_ref)
    m = _sigmoid(ga_ref[...].astype(F32)) * _dot(ya_ref[...].astype(_MXU), wa_ref[...])
    m = m + _sigmoid(gp_ref[...].astype(F32)) * _dot(yp.astype(_MXU), wp_ref[...])
    m = m + _sigmoid(gs_ref[...].astype(F32)) * _dot(ys_ref[...].astype(_MXU), ws_ref[...])
    o_ref[...] = h_ref[...] + _dot(m.astype(_MXU), wo_ref[...])


def _merge(ya, ys, proj, h, pw, ps, wa, wp, ws, wo, *, seq, tm):
    t, d = h.shape
    row = lambda i: (i, 0)
    const = lambda i: (0, 0)
    gate = lambda k: (lambda i: (i, C_GATE // d + k))
    return pl.pallas_call(
        functools.partial(_merge_kernel, seq=seq),
        grid=(t // tm,),
        in_specs=[
            pl.BlockSpec((tm, ya.shape[1]), row),
            pl.BlockSpec((tm, POOL_WIDTH), lambda i: (i, C_POOL // POOL_WIDTH)),
            pl.BlockSpec((16, POOL_WIDTH), lambda i: (jnp.maximum(i * (tm // 16) - 1, 0), C_POOL // POOL_WIDTH)),
            pl.BlockSpec((tm, ys.shape[1]), row),
            pl.BlockSpec((tm, d), gate(0)),
            pl.BlockSpec((tm, d), gate(1)),
            pl.BlockSpec((tm, d), gate(2)),
            pl.BlockSpec((tm, d), row),
            pl.BlockSpec(pw.shape, lambda i: (0, 0, 0)),
            pl.BlockSpec(ps.shape, const),
            pl.BlockSpec(wa.shape, const),
            pl.BlockSpec(wp.shape, const),
            pl.BlockSpec(ws.shape, const),
            pl.BlockSpec(wo.shape, const),
        ],
        out_specs=pl.BlockSpec((tm, d), row),
        out_shape=jax.ShapeDtypeStruct((t, d), F32),
        compiler_params=_cparams(("parallel",)),
        name="merge",
    )(ya, proj, proj, ys, proj, proj, proj, h, pw, ps, wa, wp, ws, wo)


def _reorder_w_in(w):
    o_q, o_ckv, o_qidx, o_widx, o_kidx, o_pool, o_z, o_xbc, o_dt, o_gate = (
        0, 512, 768, 1280, 1288, 1352, 1864, 2888, 4424, 4440)
    d = w.shape[0]
    small = jnp.concatenate([
        w[:, o_kidx:o_kidx + IDX_DIM], w[:, o_widx:o_widx + IDX_HEADS], w[:, o_dt:o_dt + SSD_HEADS],
        jnp.zeros((d, 128 - IDX_DIM - IDX_HEADS - SSD_HEADS), w.dtype)], axis=1)
    wb = w.astype(_MXU)
    used = CONV_DIM + 512 + D_INNER + 3 * D_MODEL + 512 + POOL_WIDTH + KV_RANK
    wide = jnp.concatenate([
        wb[:, o_xbc:o_xbc + CONV_DIM], wb[:, o_q:o_q + 512], wb[:, o_z:o_z + D_INNER],
        wb[:, o_gate:o_gate + 3 * D_MODEL], wb[:, o_qidx:o_qidx + 512], wb[:, o_pool:o_pool + POOL_WIDTH],
        wb[:, o_ckv:o_ckv + KV_RANK], jnp.zeros((d, PROJ_COLS - used), _MXU)], axis=1)
    return wide, small.astype(_MXU)


def kernel(x, p, ffn1_norm, ffn1_w_gate, ffn1_w_up, ffn1_w_down, mix_norm, w_in, kv_norm, idx_k_norm, w_uk, w_uv, pool_w, pool_scale, conv_w, conv_b, dt_bias, a_log, d_skip, ssd_norm, w_br_attn, w_br_pool, w_br_ssd, w_out, ffn2_norm, ffn2_w_gate, ffn2_w_up, ffn2_w_down, ple_norm, ple_w_gate, ple_w_proj, final_norm):
    b, s, d = x.shape
    depth = w_in.shape[0]
    t = b * s
    tm_small = min(512, t)
    kb = min(512, s)
    mx = lambda a: a.astype(_MXU)

    h = x.reshape(t, d)
    for i in range(depth):
        h = _ffn(h, ffn1_norm[i][None], mx(ffn1_w_gate[i]), mx(ffn1_w_up[i]), mx(ffn1_w_down[i]),
                 tm=tm_small)
        w_wide, w_small = _reorder_w_in(w_in[i])
        proj, small, ckv, ckvT, kidx = _inproj(h, mix_norm[i][None], w_wide, w_small, kv_norm[i][None],
                                               idx_k_norm[i][None], b=b, s=s, tm=kb, tn=2560)
        wuk = mx(jnp.transpose(w_uk[i], (1, 0, 2)))
        wuvT = mx(jnp.transpose(w_uv[i], (1, 2, 0)))
        y_attn = _attention(proj, small, ckv, ckvT, kidx, wuk, wuvT, b=b, s=s, kb=kb)
        y_ssd = _ssd(proj, small, conv_w[i], conv_b[i], dt_bias[i], a_log[i], d_skip[i], ssd_norm[i], b=b, s=s)
        h = _merge(y_attn, y_ssd, proj, h, mx(pool_w[i]), pool_scale[i][None], mx(w_br_attn[i]),
                   mx(w_br_pool[i]), mx(w_br_ssd[i]), mx(w_out[i]), seq=s, tm=tm_small)
        ple = (p[i].reshape(t, -1), ple_norm[i][None], mx(ple_w_gate[i]), mx(ple_w_proj[i]), final_norm[None])
        h = _ffn(h, ffn2_norm[i][None], mx(ffn2_w_gate[i]), mx(ffn2_w_up[i]), mx(ffn2_w_down[i]), ple,
                 tm=tm_small, final=(i == depth - 1))
    return h.reshape(b, s, d)
```
